```python
import math
import jax
import jax.numpy as jnp
from jax import lax
import numpy as np

D_MODEL = 1024
BATCH = 8
SEQ = 2048
DEPTH = 2

S5_WIDTH = D_MODEL
S5_GROUP = 16
S5_GROUPS = S5_WIDTH // S5_GROUP
S5_STATE = 64
SSD_HEAD_DIM = 64
SSD_WIDTH = D_MODEL
SSD_HEADS = SSD_WIDTH // SSD_HEAD_DIM
SSD_GROUPS = 2
SSD_STATE = 128
SSD_CONV = 4
SSD_CHUNK = 128
SSD_CONV_DIM = SSD_WIDTH + 2 * SSD_GROUPS * SSD_STATE
MIX_WIDTH = S5_WIDTH + SSD_WIDTH
IN_PROJ = S5_WIDTH + SSD_WIDTH + SSD_CONV_DIM + SSD_HEADS
IN_SPLITS = (S5_WIDTH, S5_WIDTH + SSD_WIDTH, S5_WIDTH + SSD_WIDTH + SSD_CONV_DIM)
FFN_HIDDEN = ((8 * D_MODEL + 3 * 256 - 1) // (3 * 256)) * 256
EPS = 1e-6

kernel_name = 'hybrid_s5_ssd_parallel_heads'


def rmsnorm(x, g):
    xf = x.astype(jnp.float32)
    y = xf * lax.rsqrt(jnp.mean(xf * xf, axis=-1, keepdims=True) + EPS)
    return (y * g.astype(jnp.float32)).astype(x.dtype)


def s5_mixer(u, lam_re, lam_im, log_step, b_re, b_im, c_re, c_im, d_skip, w_glu, b_glu):
    bsz, L, _ = u.shape
    f32 = jnp.float32
    uf = u.astype(f32).reshape(bsz, L, S5_GROUPS, S5_GROUP)
    step = jnp.exp(log_step.astype(f32))[:, None]
    lr = lam_re.astype(f32)
    li = lam_im.astype(f32)
    mag = jnp.exp(lr * step)
    ang = li * step
    abar_re = mag * jnp.cos(ang)
    abar_im = mag * jnp.sin(ang)
    den = lr * lr + li * li
    nr = abar_re - 1.0
    ni = abar_im
    coef_re = (nr * lr + ni * li) / den
    coef_im = (ni * lr - nr * li) / den
    bre = b_re.astype(f32)
    bim = b_im.astype(f32)
    bbar_re = coef_re[..., None] * bre - coef_im[..., None] * bim
    bbar_im = coef_re[..., None] * bim + coef_im[..., None] * bre
    bu_re = jnp.einsum('blgh,gph->blgp', uf, bbar_re)
    bu_im = jnp.einsum('blgh,gph->blgp', uf, bbar_im)
    a_re = jnp.broadcast_to(abar_re, (1, L, S5_GROUPS, S5_STATE))
    a_im = jnp.broadcast_to(abar_im, (1, L, S5_GROUPS, S5_STATE))

    def combine(e1, e2):
        a1r, a1i, b1r, b1i = e1
        a2r, a2i, b2r, b2i = e2
        return (a2r * a1r - a2i * a1i,
                a2r * a1i + a2i * a1r,
                a2r * b1r - a2i * b1i + b2r,
                a2r * b1i + a2i * b1r + b2i)

    _, _, xr, xi = lax.associative_scan(combine, (a_re, a_im, bu_re, bu_im), axis=1)
    y = (jnp.einsum('blgp,ghp->blgh', xr, c_re.astype(f32))
         - jnp.einsum('blgp,ghp->blgh', xi, c_im.astype(f32)))
    y = y.reshape(bsz, L, S5_WIDTH) + d_skip.astype(f32) * uf.reshape(bsz, L, S5_WIDTH)
    g = jax.nn.gelu(y)
    out = g * jax.nn.sigmoid(g @ w_glu.astype(f32) + b_glu.astype(f32))
    return out.astype(u.dtype)


def segsum(a):
    T = a.shape[-1]
    rep = jnp.broadcast_to(a[..., None], a.shape + (T,))
    strict = jnp.tril(jnp.ones((T, T), dtype=bool), -1)
    cs = jnp.cumsum(jnp.where(strict, rep, 0.0), axis=-2)
    incl = jnp.tril(jnp.ones((T, T), dtype=bool))
    return jnp.where(incl, cs, -jnp.inf)


def ssd_mixer(z, xbc, dt, conv_w, conv_b, dt_bias, a_log, d_skip):
    bsz, L, _ = xbc.shape
    f32 = jnp.float32
    nc = L // SSD_CHUNK
    R = SSD_HEADS // SSD_GROUPS
    xbc = lax.conv_general_dilated(
        xbc, conv_w[:, None, :].astype(xbc.dtype), window_strides=(1,),
        padding=[(SSD_CONV - 1, 0)], dimension_numbers=('NWC', 'WIO', 'NWC'),
        feature_group_count=SSD_CONV_DIM) + conv_b
    xbc = jax.nn.silu(xbc).astype(f32)
    xs, bs, cs = jnp.split(xbc, (SSD_WIDTH, SSD_WIDTH + SSD_GROUPS * SSD_STATE), axis=-1)
    dtp = jax.nn.softplus(dt.astype(f32) + dt_bias.astype(f32))
    A = -jnp.exp(a_log.astype(f32))
    dta = (dtp * A).reshape(bsz, nc, SSD_CHUNK, SSD_GROUPS, R).transpose(0, 3, 4, 1, 2)
    xh = xs.reshape(bsz, L, SSD_HEADS, SSD_HEAD_DIM)
    xdt = (xh * dtp[..., None]).reshape(bsz, nc, SSD_CHUNK, SSD_GROUPS, R, SSD_HEAD_DIM)
    bmat = bs.reshape(bsz, nc, SSD_CHUNK, SSD_GROUPS, SSD_STATE)
    cmat = cs.reshape(bsz, nc, SSD_CHUNK, SSD_GROUPS, SSD_STATE)
    a_cum = jnp.cumsum(dta, axis=-1)
    lmat = jnp.exp(segsum(dta))
    y_diag = jnp.einsum('bclgn,bcsgn,bgrcls,bcsgrp->bclgrp', cmat, bmat, lmat, xdt)
    decay_states = jnp.exp(a_cum[..., -1:] - a_cum)
    states = jnp.einsum('bclgn,bgrcl,bclgrp->bcgrpn', bmat, decay_states, xdt)
    chunk_tot = jnp.pad(a_cum[..., -1], ((0, 0), (0, 0), (0, 0), (1, 0)))
    decay_chunk = jnp.exp(segsum(chunk_tot))
    states0 = jnp.concatenate([jnp.zeros_like(states[:, :1]), states], axis=1)
    new_states = jnp.einsum('bgrzc,bcgrpn->bzgrpn', decay_chunk, states0)
    prev = new_states[:, :-1]
    y_off = jnp.einsum('bclgn,bcgrpn,bgrcl->bclgrp', cmat, prev, jnp.exp(a_cum))
    y = (y_diag + y_off).reshape(bsz, L, SSD_HEADS, SSD_HEAD_DIM) + d_skip.astype(f32)[:, None] * xh
    y = y.reshape(bsz, L, SSD_WIDTH) * jax.nn.silu(z.astype(f32))
    return y.astype(z.dtype)


def setup_inputs(seed: int = 0) -> dict:
    key = jax.random.key(seed)
    ks = jax.random.split(key, 32)
    f32 = jnp.float32

    def nrm(k, shape, scale):
        return jax.random.normal(k, shape, f32) * scale

    def gain(k, n):
        return 1.0 + 0.01 * jax.random.normal(k, (DEPTH, n), f32)

    n_idx = jnp.arange(S5_STATE, dtype=f32)
    lam_re = -0.5 + 0.01 * jax.random.normal(ks[3], (DEPTH, S5_GROUPS, S5_STATE), f32)
    lam_im = (jnp.broadcast_to(math.pi * n_idx, (DEPTH, S5_GROUPS, S5_STATE))
              + 0.01 * jax.random.normal(ks[4], (DEPTH, S5_GROUPS, S5_STATE), f32))
    log_step = jax.random.uniform(ks[5], (DEPTH, S5_GROUPS), f32, math.log(1e-3), math.log(1e-1))
    dt0 = jnp.exp(jax.random.uniform(ks[17], (DEPTH, SSD_HEADS), f32, math.log(1e-3), math.log(1e-1)))
    dt_bias = dt0 + jnp.log(-jnp.expm1(-dt0))
    a_log = jnp.log(jax.random.uniform(ks[18], (DEPTH, SSD_HEADS), f32, 1.0, 16.0))
    return {
        'x': jax.random.normal(ks[0], (BATCH, SEQ, D_MODEL), f32),
        'norm_mix': gain(ks[1], D_MODEL),
        'w_in': nrm(ks[2], (DEPTH, D_MODEL, IN_PROJ), D_MODEL ** -0.5),
        's5_lam_re': lam_re,
        's5_lam_im': lam_im,
        's5_log_step': log_step,
        's5_b_re': nrm(ks[6], (DEPTH, S5_GROUPS, S5_STATE, S5_GROUP), (2 * S5_GROUP) ** -0.5),
        's5_b_im': nrm(ks[7], (DEPTH, S5_GROUPS, S5_STATE, S5_GROUP), (2 * S5_GROUP) ** -0.5),
        's5_c_re': nrm(ks[8], (DEPTH, S5_GROUPS, S5_GROUP, S5_STATE), (2 * S5_STATE) ** -0.5),
        's5_c_im': nrm(ks[9], (DEPTH, S5_GROUPS, S5_GROUP, S5_STATE), (2 * S5_STATE) ** -0.5),
        's5_d': nrm(ks[10], (DEPTH, S5_WIDTH), 1.0),
        's5_w_glu': nrm(ks[11], (DEPTH, S5_WIDTH, S5_WIDTH), S5_WIDTH ** -0.5),
        's5_b_glu': nrm(ks[12], (DEPTH, S5_WIDTH), 0.01),
        's5_norm': gain(ks[13], S5_WIDTH),
        'ssd_conv_w': nrm(ks[14], (DEPTH, SSD_CONV, SSD_CONV_DIM), SSD_CONV ** -0.5),
        'ssd_conv_b': nrm(ks[15], (DEPTH, SSD_CONV_DIM), 0.01),
        'ssd_dt_bias': dt_bias,
        'ssd_a_log': a_log,
        'ssd_d': 1.0 + 0.1 * jax.random.normal(ks[19], (DEPTH, SSD_HEADS), f32),
        'ssd_norm': gain(ks[20], SSD_WIDTH),
        'w_out': nrm(ks[21], (DEPTH, MIX_WIDTH, D_MODEL), MIX_WIDTH ** -0.5),
        'norm_ffn': gain(ks[22], D_MODEL),
        'w_gate': nrm(ks[23], (DEPTH, D_MODEL, FFN_HIDDEN), D_MODEL ** -0.5),
        'w_up': nrm(ks[24], (DEPTH, D_MODEL, FFN_HIDDEN), D_MODEL ** -0.5),
        'w_down': nrm(ks[25], (DEPTH, FFN_HIDDEN, D_MODEL), FFN_HIDDEN ** -0.5),
        'norm_final': 1.0 + 0.01 * jax.random.normal(ks[26], (D_MODEL,), f32),
    }


def reference(x, norm_mix, w_in, s5_lam_re, s5_lam_im, s5_log_step, s5_b_re, s5_b_im,
              s5_c_re, s5_c_im, s5_d, s5_w_glu, s5_b_glu, s5_norm, ssd_conv_w, ssd_conv_b,
              ssd_dt_bias, ssd_a_log, ssd_d, ssd_norm, w_out, norm_ffn, w_gate, w_up,
              w_down, norm_final):
    for i in range(DEPTH):
        h = rmsnorm(x, norm_mix[i])
        proj = h @ w_in[i]
        u_a, z_b, xbc_b, dt_b = jnp.split(proj, IN_SPLITS, axis=-1)
        y_a = s5_mixer(u_a, s5_lam_re[i], s5_lam_im[i], s5_log_step[i], s5_b_re[i],
                       s5_b_im[i], s5_c_re[i], s5_c_im[i], s5_d[i], s5_w_glu[i], s5_b_glu[i])
        y_a = rmsnorm(y_a, s5_norm[i])
        y_b = ssd_mixer(z_b, xbc_b, dt_b, ssd_conv_w[i], ssd_conv_b[i], ssd_dt_bias[i],
                        ssd_a_log[i], ssd_d[i])
        y_b = rmsnorm(y_b, ssd_norm[i])
        x = x + jnp.concatenate([y_a, y_b], axis=-1) @ w_out[i]
        h = rmsnorm(x, norm_ffn[i])
        x = x + (jax.nn.silu(h @ w_gate[i]) * (h @ w_up[i])) @ w_down[i]
    return rmsnorm(x, norm_final)
```

```python
import functools

import jax
import jax.numpy as jnp
from jax import lax
from jax.experimental import pallas as pl
from jax.experimental.pallas import tpu as pltpu

F32 = jnp.float32
BF16 = jnp.bfloat16
EPS = 1e-6

D_MODEL = 1024
S5_GROUP = 16
S5_GROUPS = 64
S5_STATE = 64
S5_CHUNK = 32
S5_FLAT = S5_CHUNK * S5_GROUP
SSD_HEAD_DIM = 64
SSD_HEADS = 16
SSD_GROUPS = 2
SSD_STATE = 128
SSD_CONV = 4
SSD_CHUNK = 128
SSD_WIDTH = SSD_HEADS * SSD_HEAD_DIM
SSD_CONV_DIM = SSD_WIDTH + 2 * SSD_GROUPS * SSD_STATE
FFN_HIDDEN = 2816
LANES = 128
SUBLANES = 8
VMEM_LIMIT = 56 * 1024 * 1024


def _params(*sem):
    return pltpu.CompilerParams(dimension_semantics=sem, vmem_limit_bytes=VMEM_LIMIT)


def _rms(x, gain):
    return x * lax.rsqrt(jnp.mean(x * x, axis=-1, keepdims=True) + EPS) * gain


def _split3(v):
    hi = v.astype(BF16)
    r1 = v - hi.astype(F32)
    mid = r1.astype(BF16)
    lo = (r1 - mid.astype(F32)).astype(BF16)
    return hi, mid, lo


def _dot_exact_rhs(v, sel):
    hi, mid, lo = _split3(v)
    d = functools.partial(jnp.dot, preferred_element_type=F32)
    return d(hi, sel) + d(mid, sel) + d(lo, sel)


def _dot_exact_lhs(sel, v):
    hi, mid, lo = _split3(v)
    d = functools.partial(jnp.dot, preferred_element_type=F32)
    return d(sel, hi) + d(sel, mid) + d(sel, lo)


def _inproj_kernel(x_ref, g_ref, wu_ref, wz_ref, wx_ref, wdt_ref,
                   u_ref, z_ref, xbc_ref, dt_ref):
    hb = _rms(x_ref[...], g_ref[...]).astype(BF16)
    d = functools.partial(jnp.dot, preferred_element_type=F32)
    u_ref[...] = d(hb, wu_ref[...]).astype(u_ref.dtype)
    z_ref[...] = d(hb, wz_ref[...]).astype(z_ref.dtype)
    xbc_ref[...] = d(hb, wx_ref[...]).astype(xbc_ref.dtype)
    dt_ref[...] = d(hb, wdt_ref[...])


def _in_proj(x2, gain, wu, wz, wx, wdt, tm):
    n = x2.shape[0]
    row = lambda w: pl.BlockSpec((tm, w), lambda i: (i, 0))
    full = lambda a: pl.BlockSpec(a.shape, lambda i: (0,) * a.ndim)
    return pl.pallas_call(
        _inproj_kernel,
        grid=(n // tm,),
        in_specs=[row(D_MODEL), full(gain), full(wu), full(wz), full(wx), full(wdt)],
        out_specs=[row(D_MODEL), row(SSD_WIDTH), row(SSD_CONV_DIM), row(LANES)],
        out_shape=[jax.ShapeDtypeStruct((n, D_MODEL), BF16),
                   jax.ShapeDtypeStruct((n, SSD_WIDTH), BF16),
                   jax.ShapeDtypeStruct((n, SSD_CONV_DIM), BF16),
                   jax.ShapeDtypeStruct((n, LANES), F32)],
        compiler_params=_params("parallel"),
        name="in_proj",
    )(x2, gain, wu, wz, wx, wdt)


def _s5_prep_kernel(ls_ref, lrc_ref, lic_ref, lrr_ref, lir_ref, cre_ref, cim_ref,
                    bre_ref, bim_ref, mb_ref, coff_ref, apow_ref):
    step = jnp.exp(ls_ref[0])
    lrc = lrc_ref[0]
    lic = lic_ref[0]
    lane = lax.broadcasted_iota(jnp.int32, (1, S5_FLAT), 1)
    t_lane = (lane // S5_GROUP).astype(F32)
    cre = cre_ref[0]
    cim = cim_ref[0]

    def q_of(tau):
        st = step * tau
        mag = jnp.exp(lrc * st)
        ang = lic * st
        pre = mag * jnp.cos(ang)
        pim = mag * jnp.sin(ang)
        return jnp.concatenate([cre * pre - cim * pim, -(cre * pim + cim * pre)], axis=0)

    q0 = q_of(t_lane)
    coff_ref[0] = q_of(t_lane + 1.0).astype(coff_ref.dtype)

    lrr = lrr_ref[0]
    lir = lir_ref[0]
    lane2 = lax.broadcasted_iota(jnp.int32, (1, 2 * S5_STATE), 1)
    first = lane2 < S5_STATE
    sgn = jnp.where(first, -1.0, 1.0).astype(F32)

    def powers(tau):
        st = step * tau
        mag = jnp.exp(lrr * st)
        ang = lir * st
        return mag * jnp.cos(ang), mag * jnp.sin(ang)

    are, aim = powers(1.0)
    den = lrr * lrr + lir * lir
    nr = are - 1.0
    cfr = (nr * lrr + aim * lir) / den
    cfi = (aim * lrr - nr * lir) / den
    bre = bre_ref[0]
    bim = bim_ref[0]
    x1 = cfr * bre - cfi * bim
    x2 = cfr * bim + cfi * bre
    lhs = jnp.where(first, x1, x2)
    slab0 = _dot_exact_rhs_f32(lhs, q0)

    tau_s = (S5_CHUNK - 1) - lax.broadcasted_iota(jnp.int32, (S5_CHUNK, 1), 0)
    pre_s, pim_s = powers(tau_s.astype(F32))
    p1 = jnp.where(first, pre_s, pim_s)
    p2 = jnp.where(first, pim_s, pre_s)
    for s in range(S5_CHUNK):
        rows = slice(s * S5_GROUP, (s + 1) * S5_GROUP)
        if s == 0:
            blk = slab0
        else:
            blk = jnp.where(lane >= s * S5_GROUP,
                            pltpu.roll(slab0, s * S5_GROUP, axis=1), 0.0)
        mb_ref[0, rows, 0:S5_FLAT] = blk.astype(mb_ref.dtype)
        a = p1[s:s + 1, :]
        b = p2[s:s + 1, :]
        mb_ref[0, rows, S5_FLAT:S5_FLAT + LANES] = (x1 * a + sgn * (x2 * b)).astype(mb_ref.dtype)
        mb_ref[0, rows, S5_FLAT + LANES:S5_FLAT + 2 * LANES] = (
            x1 * b - sgn * (x2 * a)).astype(mb_ref.dtype)

    atr, ati = powers(float(S5_CHUNK))
    a2 = sgn * ati
    apow_ref[0] = jnp.concatenate(
        [atr, a2, -a2, jnp.zeros((SUBLANES - 3, 2 * S5_STATE), F32)], axis=0)


def _dot_exact_rhs_f32(a, b):
    a3 = _split3(a)
    b3 = _split3(b)
    d = functools.partial(jnp.dot, preferred_element_type=F32)
    acc = None
    for i in range(3):
        for j in range(3 - i):
            t = d(a3[i], b3[j])
            acc = t if acc is None else acc + t
    return acc


def _s5_prep(ls, lrc, lic, lrr, lir, cre, cim, bre, bim):
    dg = ls.shape[0]
    spec = lambda a: pl.BlockSpec((1,) + a.shape[1:], lambda i: (i, 0, 0))
    ins = (ls, lrc, lic, lrr, lir, cre, cim, bre, bim)
    return pl.pallas_call(
        _s5_prep_kernel,
        grid=(dg,),
        in_specs=[spec(a) for a in ins],
        out_specs=[pl.BlockSpec((1, S5_FLAT, S5_FLAT + 2 * LANES), lambda i: (i, 0, 0)),
                   pl.BlockSpec((1, 2 * S5_STATE, S5_FLAT), lambda i: (i, 0, 0)),
                   pl.BlockSpec((1, SUBLANES, 2 * S5_STATE), lambda i: (i, 0, 0))],
        out_shape=[jax.ShapeDtypeStruct((dg, S5_FLAT, S5_FLAT + 2 * LANES), BF16),
                   jax.ShapeDtypeStruct((dg, 2 * S5_STATE, S5_FLAT), BF16),
                   jax.ShapeDtypeStruct((dg, SUBLANES, 2 * S5_STATE), F32)],
        compiler_params=_params("parallel"),
        name="s5_prep",
    )(*ins)


def _s5_core_kernel(z_ref, mb_ref, coff_ref, ap_ref, y_ref, e_scr, sp_scr, *, batch):
    ng, rows, _ = z_ref.shape
    d = functools.partial(jnp.dot, preferred_element_type=F32)
    for g in range(ng):
        e_scr[g] = d(z_ref[g], mb_ref[g, :, S5_FLAT:])

    def body(c, carry):
        r = pl.multiple_of(c * batch, batch)
        out = []
        for g in range(ng):
            s, sw = carry[g]
            sp_scr[g, pl.ds(r, batch), :] = s
            e = e_scr[g, pl.ds(r, batch), :]
            a1 = ap_ref[g, 0:1, :]
            a2 = ap_ref[g, 1:2, :]
            a2w = ap_ref[g, 2:3, :]
            out.append((a1 * s + a2 * sw + e[:, :LANES], a1 * sw + a2w * s + e[:, LANES:]))
        return tuple(out)

    zero = jnp.zeros((batch, 2 * S5_STATE), F32)
    lax.fori_loop(0, rows // batch, body, tuple((zero, zero) for _ in range(ng)))

    for g in range(ng):
        y = d(z_ref[g], mb_ref[g, :, :S5_FLAT]) + d(sp_scr[g].astype(BF16), coff_ref[g])
        y_ref[g] = y.astype(y_ref.dtype)


def _s5_core(zf, mb, coff, apow, batch, ng):
    groups, rows, _ = zf.shape
    spec = lambda a: pl.BlockSpec((ng,) + a.shape[1:], lambda i: (i, 0, 0))
    return pl.pallas_call(
        functools.partial(_s5_core_kernel, batch=batch),
        grid=(groups // ng,),
        in_specs=[spec(zf), spec(mb), spec(coff), spec(apow)],
        out_specs=spec(zf),
        out_shape=jax.ShapeDtypeStruct(zf.shape, BF16),
        scratch_shapes=[pltpu.VMEM((ng, rows, 2 * LANES), F32),
                        pltpu.VMEM((ng, rows, LANES), F32)],
        compiler_params=_params("parallel"),
        name="s5_core",
    )(zf, mb, coff, apow)


def _ssd_kernel(z_ref, xbc_ref, dt_ref, cw_ref, cb_ref, dtb_ref, alog_ref, dsk_ref,
                gn_ref, e64_ref, y_ref, buf_scr, st_scr):
    lb = z_ref.shape[1]
    j = pl.program_id(1)

    @pl.when(j == 0)
    def _():
        buf_scr[0:SUBLANES, :] = jnp.zeros((SUBLANES, SSD_CONV_DIM), F32)
        st_scr[...] = jnp.zeros_like(st_scr)

    @pl.when(j > 0)
    def _():
        buf_scr[0:SUBLANES, :] = buf_scr[lb:lb + SUBLANES, :]

    buf_scr[SUBLANES:SUBLANES + lb, :] = xbc_ref[0].astype(F32)
    acc = cb_ref[...] + cw_ref[0:1, :] * buf_scr[pl.ds(SUBLANES - 3, lb), :]
    for k in range(1, SSD_CONV):
        acc = acc + cw_ref[k:k + 1, :] * buf_scr[pl.ds(SUBLANES - 3 + k, lb), :]
    xc = acc * jax.nn.sigmoid(acc)

    d = functools.partial(jnp.dot, preferred_element_type=F32)
    ri = lax.broadcasted_iota(jnp.int32, (SSD_CHUNK, SSD_CHUNK), 0)
    ci = lax.broadcasted_iota(jnp.int32, (SSD_CHUNK, SSD_CHUNK), 1)
    causal = ri >= ci
    ltri = causal.astype(BF16)
    lo_half = ci < SSD_HEAD_DIM
    a_neg = -jnp.exp(alog_ref[...])
    e64 = e64_ref[...]
    b0 = SSD_WIDTH
    c0 = SSD_WIDTH + SSD_GROUPS * SSD_STATE
    hpg = SSD_HEADS // SSD_GROUPS

    for c in range(lb // SSD_CHUNK):
        r0 = c * SSD_CHUNK
        xs = xc[r0:r0 + SSD_CHUNK, 0:SSD_WIDTH]
        dtv = dt_ref[0, r0:r0 + SSD_CHUNK, :] + dtb_ref[...]
        dtp = jnp.maximum(dtv, 0.0) + jnp.log1p(jnp.exp(-jnp.abs(dtv)))
        dta = dtp * a_neg
        acum = _dot_exact_lhs(ltri, dta)
        acum_t = acum.T
        dtp64 = _dot_exact_rhs(dtp, e64)
        acum64 = _dot_exact_rhs(acum, e64)
        atot64 = acum64[SSD_CHUNK - 1:SSD_CHUNK, :]
        xdt = xs * dtp64
        xw = (xdt * jnp.exp(atot64 - acum64)).astype(BF16)
        ea64 = jnp.exp(acum64)
        gmats = []
        for g in range(SSD_GROUPS):
            bm = xc[r0:r0 + SSD_CHUNK, b0 + g * SSD_STATE:b0 + (g + 1) * SSD_STATE].astype(BF16)
            cm = xc[r0:r0 + SSD_CHUNK, c0 + g * SSD_STATE:c0 + (g + 1) * SSD_STATE].astype(BF16)
            gm = lax.dot_general(cm, bm, (((1,), (1,)), ((), ())), preferred_element_type=F32)
            gmats.append((bm, cm, gm))
        ys = []
        for p in range(SSD_HEADS // 2):
            bm, cm, gm = gmats[(2 * p) // hpg]
            cols = slice(p * LANES, (p + 1) * LANES)
            xp = xdt[:, cols]
            y = None
            for hh in range(2):
                h = 2 * p + hh
                col = jnp.broadcast_to(acum[:, h:h + 1], (SSD_CHUNK, SSD_CHUNK))
                rowv = acum_t[h:h + 1, :]
                w = jnp.where(causal, gm * jnp.exp(col - rowv), 0.0).astype(BF16)
                xh = jnp.where(lo_half if hh == 0 else ~lo_half, xp, 0.0).astype(BF16)
                t = d(w, xh)
                y = t if y is None else y + t
            st = st_scr[p]
            y = y + d(cm, st.astype(BF16)) * ea64[:, cols]
            st_scr[p] = st * jnp.exp(atot64[:, cols]) + lax.dot_general(
                bm, xw[:, cols], (((0,), (0,)), ((), ())), preferred_element_type=F32)
            ys.append(y + dsk_ref[:, cols] * xs[:, cols])
        yc = jnp.concatenate(ys, axis=1)
        zc = z_ref[0, r0:r0 + SSD_CHUNK, :].astype(F32)
        yc = yc * (zc * jax.nn.sigmoid(zc))
        y_ref[0, r0:r0 + SSD_CHUNK, :] = _rms(yc, gn_ref[...]).astype(y_ref.dtype)


def _ssd(z3, xbc3, dt3, cw, cb, dtb, alog, dsk, gn, e64, lb):
    b, l, _ = z3.shape
    blk = lambda w: pl.BlockSpec((1, lb, w), lambda i, j: (i, j, 0))
    full = lambda a: pl.BlockSpec(a.shape, lambda i, j: (0,) * a.ndim)
    return pl.pallas_call(
        _ssd_kernel,
        grid=(b, l // lb),
        in_specs=[blk(SSD_WIDTH), blk(SSD_CONV_DIM), blk(LANES), full(cw), full(cb),
                  full(dtb), full(alog), full(dsk), full(gn), full(e64)],
        out_specs=blk(SSD_WIDTH),
        out_shape=jax.ShapeDtypeStruct((b, l, SSD_WIDTH), BF16),
        scratch_shapes=[pltpu.VMEM((lb + 2 * SUBLANES, SSD_CONV_DIM), F32),
                        pltpu.VMEM((SSD_HEADS // 2, SSD_STATE, LANES), F32)],
        compiler_params=_params("parallel", "arbitrary"),
        name="ssd",
    )(z3, xbc3, dt3, cw, cb, dtb, alog, dsk, gn, e64)


_FFN_SPLITS = ((0, 1024), (1024, 2048), (2048, FFN_HIDDEN))


def _gelu_tanh(v):
    return 0.5 * v * (1.0 + jnp.tanh(0.7978845608028654 * (v + 0.044715 * (v * v * v))))


def _post_kernel(x_ref, ys_ref, u_ref, yb_ref, d5_ref, wglu_ref, bglu_ref, g5_ref,
                 woa_ref, wob_ref, gf_ref, wg_ref, wu_ref, wd_ref, gfin_ref, o_ref, *, final):
    d = functools.partial(jnp.dot, preferred_element_type=F32)
    y = ys_ref[...].astype(F32) + d5_ref[...] * u_ref[...].astype(F32)
    g = _gelu_tanh(y)
    ya = g * jax.nn.sigmoid(d(g.astype(BF16), wglu_ref[...]) + bglu_ref[...])
    ya = _rms(ya, g5_ref[...]).astype(BF16)
    x = x_ref[...] + d(ya, woa_ref[...]) + d(yb_ref[...], wob_ref[...])
    hb = _rms(x, gf_ref[...]).astype(BF16)
    o_ref[...] = x
    for lo, hi in _FFN_SPLITS:
        gate = d(hb, wg_ref[:, lo:hi])
        act = (gate * jax.nn.sigmoid(gate)) * d(hb, wu_ref[:, lo:hi])
        o_ref[...] += d(act.astype(BF16), wd_ref[lo:hi, :])
    if final:
        o_ref[...] = _rms(o_ref[...], gfin_ref[...])


def _post(x2, ys, u, yb, d5, wglu, bglu, g5, woa, wob, gf, wg, wu, wd, gfin, tm, final):
    n = x2.shape[0]
    row = pl.BlockSpec((tm, D_MODEL), lambda i: (i, 0))
    full = lambda a: pl.BlockSpec(a.shape, lambda i: (0,) * a.ndim,
                                  pipeline_mode=pl.Buffered(1))
    consts = (d5, wglu, bglu, g5, woa, wob, gf, wg, wu, wd, gfin)
    return pl.pallas_call(
        functools.partial(_post_kernel, final=final),
        grid=(n // tm,),
        in_specs=[row, row, row, row] + [full(a) for a in consts],
        out_specs=row,
        out_shape=jax.ShapeDtypeStruct((n, D_MODEL), F32),
        compiler_params=_params("parallel"),
        name="post",
    )(x2, ys, u, yb, *consts)


def kernel(x, norm_mix, w_in, s5_lam_re, s5_lam_im, s5_log_step, s5_b_re, s5_b_im, s5_c_re, s5_c_im, s5_d, s5_w_glu, s5_b_glu, s5_norm, ssd_conv_w, ssd_conv_b, ssd_dt_bias, ssd_a_log, ssd_d, ssd_norm, w_out, norm_ffn, w_gate, w_up, w_down, norm_final):
    bsz, seq, _ = x.shape
    depth = w_in.shape[0]
    n = bsz * seq
    assert bsz == SUBLANES and seq % S5_CHUNK == 0 and seq % SSD_CHUNK == 0
    nc5 = seq // S5_CHUNK
    tm = min(512, n)
    lb = min(256, seq)
    dg = depth * S5_GROUPS
    row = lambda a: a.reshape(1, -1).astype(F32)

    dbl = lambda a: jnp.concatenate([a, a], axis=-1)
    ls = s5_log_step.reshape(dg, 1, 1)
    lrc = s5_lam_re.reshape(dg, S5_STATE, 1)
    lic = s5_lam_im.reshape(dg, S5_STATE, 1)
    lrr = dbl(s5_lam_re).reshape(dg, 1, 2 * S5_STATE)
    lir = dbl(s5_lam_im).reshape(dg, 1, 2 * S5_STATE)
    ctile = lambda c: jnp.tile(jnp.swapaxes(c, 2, 3), (1, 1, 1, S5_CHUNK)).reshape(
        dg, S5_STATE, S5_FLAT)
    bdbl = lambda b: dbl(jnp.swapaxes(b, 2, 3)).reshape(dg, S5_GROUP, 2 * S5_STATE)
    mb, coff, apow = _s5_prep(ls, lrc, lic, lrr, lir, ctile(s5_c_re), ctile(s5_c_im),
                              bdbl(s5_b_re), bdbl(s5_b_im))

    head_of_col = jnp.arange(SSD_WIDTH) // SSD_HEAD_DIM
    e64 = (jnp.arange(LANES)[:, None] == head_of_col[None, :]).astype(BF16)
    pad_h = lambda a: jnp.pad(a.astype(F32), (0, LANES - SSD_HEADS)).reshape(1, LANES)

    x2 = x.reshape(n, D_MODEL)
    for i in range(depth):
        w = w_in[i].astype(BF16)
        wu = w[:, :D_MODEL]
        wz = w[:, D_MODEL:D_MODEL + SSD_WIDTH]
        wx = w[:, D_MODEL + SSD_WIDTH:D_MODEL + SSD_WIDTH + SSD_CONV_DIM]
        wdt = jnp.pad(w[:, D_MODEL + SSD_WIDTH + SSD_CONV_DIM:], ((0, 0), (0, LANES - SSD_HEADS)))
        u, z, xbc, dt = _in_proj(x2, row(norm_mix[i]), wu, wz, wx, wdt, tm)

        zf = u.reshape(bsz, nc5, S5_CHUNK, S5_GROUPS, S5_GROUP).transpose(3, 1, 0, 2, 4)
        zf = zf.reshape(S5_GROUPS, nc5 * bsz, S5_FLAT)
        sl = slice(i * S5_GROUPS, (i + 1) * S5_GROUPS)
        yf = _s5_core(zf, mb[sl], coff[sl], apow[sl], bsz, 8)
        ys = yf.reshape(S5_GROUPS, nc5, bsz, S5_CHUNK, S5_GROUP).transpose(2, 1, 3, 0, 4)
        ys = ys.reshape(n, D_MODEL)

        yb = _ssd(z.reshape(bsz, seq, SSD_WIDTH), xbc.reshape(bsz, seq, SSD_CONV_DIM),
                  dt.reshape(bsz, seq, LANES), ssd_conv_w[i].astype(F32),
                  row(ssd_conv_b[i]), pad_h(ssd_dt_bias[i]), pad_h(ssd_a_log[i]),
                  row(jnp.repeat(ssd_d[i], SSD_HEAD_DIM)), row(ssd_norm[i]), e64, lb)

        wo = w_out[i].astype(BF16)
        x2 = _post(x2, ys, u, yb.reshape(n, SSD_WIDTH), row(s5_d[i]),
                   s5_w_glu[i].astype(BF16), row(s5_b_glu[i]), row(s5_norm[i]),
                   wo[:D_MODEL], wo[D_MODEL:], row(norm_ffn[i]),
                   w_gate[i].astype(BF16), w_up[i].astype(BF16), w_down[i].astype(BF16),
                   row(norm_final), tm, i == depth - 1)
    return x2.reshape(bsz, seq, D_MODEL)
```

```python
import functools

import jax
import jax.numpy as jnp
from jax import lax
from jax.experimental import pallas as pl
from jax.experimental.pallas import tpu as pltpu

F32 = jnp.float32
BF16 = jnp.bfloat16
EPS = 1e-6

D_MODEL = 1024
S5_GROUP = 16
S5_GROUPS = 64
S5_STATE = 64
S5_CHUNK = 32
S5_FLAT = S5_CHUNK * S5_GROUP
SSD_HEAD_DIM = 64
SSD_HEADS = 16
SSD_GROUPS = 2
SSD_STATE = 128
SSD_CONV = 4
SSD_CHUNK = 128
SSD_WIDTH = SSD_HEADS * SSD_HEAD_DIM
SSD_CONV_DIM = SSD_WIDTH + 2 * SSD_GROUPS * SSD_STATE
FFN_HIDDEN = 2816
LANES = 128
SUBLANES = 8
S5_GPB = LANES // S5_GROUP
VMEM_LIMIT = 56 * 1024 * 1024
IN_U = D_MODEL
IN_Z = IN_U + SSD_WIDTH
IN_X = IN_Z + SSD_CONV_DIM


def _params(*sem):
    return pltpu.CompilerParams(dimension_semantics=sem, vmem_limit_bytes=VMEM_LIMIT)


def _rms(x, gain):
    return x * lax.rsqrt(jnp.mean(x * x, axis=-1, keepdims=True) + EPS) * gain


def _dot(a, b):
    return jnp.dot(a, b, preferred_element_type=F32)


def _split3(v):
    hi = v.astype(BF16)
    r1 = v - hi.astype(F32)
    mid = r1.astype(BF16)
    lo = (r1 - mid.astype(F32)).astype(BF16)
    return hi, mid, lo


def _dot_sel_rhs(v, sel):
    hi, mid, lo = _split3(v)
    return _dot(hi, sel) + _dot(mid, sel) + _dot(lo, sel)


def _dot_sel_lhs(sel, v):
    hi, mid, lo = _split3(v)
    return _dot(sel, hi) + _dot(sel, mid) + _dot(sel, lo)


def _dot_f32(a, b):
    a3 = _split3(a)
    b3 = _split3(b)
    acc = None
    for i in range(3):
        for j in range(3 - i):
            t = _dot(a3[i], b3[j])
            acc = t if acc is None else acc + t
    return acc


def _layer_spec(shape, layer, n_grid, col=0):
    zeros = (0,) * (len(shape) - 2)
    if n_grid == 1:
        return pl.BlockSpec((None,) + shape[1:], lambda i: (layer,) + zeros + (col,))
    return pl.BlockSpec((None,) + shape[1:], lambda i, j: (layer,) + zeros + (col,))


def _cast_kernel(w_ref, o_ref):
    o_ref[...] = w_ref[...].astype(o_ref.dtype)


def _cast_bf16(w, rb):
    depth, rows, cols = w.shape
    spec = pl.BlockSpec((1, rb, cols), lambda i, j: (i, j, 0))
    return pl.pallas_call(
        _cast_kernel, grid=(depth, rows // rb), in_specs=[spec], out_specs=spec,
        out_shape=jax.ShapeDtypeStruct(w.shape, BF16),
        compiler_params=_params("parallel", "parallel"), name="cast",
    )(w)


def _cast_win_kernel(w_ref, wu_ref, wz_ref, wx_ref, wdt_ref):
    wu_ref[0] = w_ref[0, :, 0:IN_U].astype(BF16)
    wz_ref[0] = w_ref[0, :, IN_U:IN_Z].astype(BF16)
    wx_ref[0] = w_ref[0, :, IN_Z:IN_X].astype(BF16)
    wdt_ref[0] = jnp.zeros(wdt_ref.shape[1:], BF16)
    wdt_ref[0, :, 0:SSD_HEADS] = w_ref[0, :, IN_X:IN_X + SSD_HEADS].astype(BF16)


def _cast_win(w_in, rb):
    depth, rows, cols = w_in.shape
    spec = lambda c: pl.BlockSpec((1, rb, c), lambda i, j: (i, j, 0))
    shp = lambda c: jax.ShapeDtypeStruct((depth, rows, c), BF16)
    return pl.pallas_call(
        _cast_win_kernel, grid=(depth, rows // rb), in_specs=[spec(cols)],
        out_specs=[spec(D_MODEL), spec(SSD_WIDTH), spec(SSD_CONV_DIM), spec(LANES)],
        out_shape=[shp(D_MODEL), shp(SSD_WIDTH), shp(SSD_CONV_DIM), shp(LANES)],
        compiler_params=_params("parallel", "parallel"), name="cast_win",
    )(w_in)


def _s5_rows(c, j, b):
    return pl.ds(((c * (D_MODEL // LANES) + j) * S5_CHUNK) * SUBLANES + b, S5_CHUNK,
                 stride=SUBLANES)


def _inproj_kernel(x_ref, g_ref, wu_ref, wz_ref, wx_ref, wdt_ref,
                   u_ref, z_ref, xbc_ref, dt_ref):
    bsz, tok, _ = x_ref.shape
    hb = _rms(x_ref[...], g_ref[...]).astype(BF16).reshape(bsz * tok, D_MODEL)
    u = _dot(hb, wu_ref[...])
    for b in range(bsz):
        for c in range(tok // S5_CHUNK):
            r0 = b * tok + c * S5_CHUNK
            for j in range(D_MODEL // LANES):
                u_ref[_s5_rows(c, j, b), :] = u[r0:r0 + S5_CHUNK, j * LANES:(j + 1) * LANES]
    z_ref[...] = _dot(hb, wz_ref[...]).astype(z_ref.dtype).reshape(z_ref.shape)
    xbc_ref[...] = _dot(hb, wx_ref[...]).astype(xbc_ref.dtype).reshape(xbc_ref.shape)
    dt_ref[...] = _dot(hb, wdt_ref[...]).reshape(dt_ref.shape)


def _in_proj(x3, gain, wu, wz, wx, wdt, layer, tok):
    b, seq, _ = x3.shape
    nat = lambda w: pl.BlockSpec((b, tok, w), lambda i: (0, i, 0))
    ls = lambda a: _layer_spec(a.shape, layer, 1)
    return pl.pallas_call(
        _inproj_kernel,
        grid=(seq // tok,),
        in_specs=[nat(D_MODEL), ls(gain), ls(wu), ls(wz), ls(wx), ls(wdt)],
        out_specs=[pl.BlockSpec((tok * b * D_MODEL // LANES, LANES), lambda i: (i, 0)),
                   nat(SSD_WIDTH), nat(SSD_CONV_DIM), nat(LANES)],
        out_shape=[jax.ShapeDtypeStruct((seq * b * D_MODEL // LANES, LANES), F32),
                   jax.ShapeDtypeStruct((b, seq, SSD_WIDTH), BF16),
                   jax.ShapeDtypeStruct((b, seq, SSD_CONV_DIM), BF16),
                   jax.ShapeDtypeStruct((b, seq, LANES), F32)],
        compiler_params=_params("parallel"),
        name="in_proj",
    )(x3, gain, wu, wz, wx, wdt)


def _s5_prep_kernel(ls_ref, lrc_ref, lic_ref, lrr_ref, lir_ref, cre_ref, cim_ref,
                    bre_ref, bim_ref, mt_ref, bt_ref, ct_ref, apow_ref):
    step = jnp.exp(ls_ref[0])
    lrc = lrc_ref[0]
    lic = lic_ref[0]
    p2 = 2 * S5_STATE

    are = jnp.exp(lrc * step) * jnp.cos(lic * step)
    aim = jnp.exp(lrc * step) * jnp.sin(lic * step)
    den = lrc * lrc + lic * lic
    nr = are - 1.0
    cfr = (nr * lrc + aim * lic) / den
    cfi = (aim * lrc - nr * lic) / den
    bre = bre_ref[0]
    bim = bim_ref[0]
    bbr = cfr * bre - cfi * bim
    bbi = cfr * bim + cfi * bre
    k_lane = lax.broadcasted_iota(jnp.int32, (1, LANES), 1).astype(F32)
    st = step * k_lane
    mag = jnp.exp(lrc * st)
    pre_c = mag * jnp.cos(lic * st)
    pim_c = mag * jnp.sin(lic * st)

    lane = lax.broadcasted_iota(jnp.int32, (1, S5_FLAT), 1)
    h_of_lane = lax.broadcasted_iota(jnp.int32, (S5_GROUP, S5_FLAT), 1) % S5_GROUP
    e_h = (h_of_lane == lax.broadcasted_iota(jnp.int32, (S5_GROUP, S5_FLAT), 0)).astype(BF16)
    k_row = lax.broadcasted_iota(jnp.int32, (LANES, S5_FLAT), 0)
    j_lane = lax.broadcasted_iota(jnp.int32, (LANES, S5_FLAT), 1) // S5_GROUP
    e_rev = (k_row == (S5_CHUNK - 1) - j_lane).astype(BF16)
    btr = _dot_sel_rhs(bbr, e_h)
    bti = _dot_sel_rhs(bbi, e_h)
    prr = _dot_sel_rhs(pre_c, e_rev)
    pri = _dot_sel_rhs(pim_c, e_rev)
    wre = prr * btr - pri * bti
    wim = prr * bti + pri * btr
    bt_ref[0] = jnp.concatenate([wre, wim, wim, wre], axis=0).astype(bt_ref.dtype)

    lane2 = lax.broadcasted_iota(jnp.int32, (1, p2), 1)
    first = lane2 < S5_STATE
    c1 = cre_ref[0]
    c2 = cim_ref[0]
    lhs = jnp.where(first, c1, c2)
    rev = _dot_f32(lhs, jnp.concatenate([wre, -wim], axis=0))
    for t in range(S5_CHUNK):
        rows = slice(t * S5_GROUP, (t + 1) * S5_GROUP)
        shift = (S5_FLAT - (S5_CHUNK - 1 - t) * S5_GROUP) % S5_FLAT
        blk = rev if shift == 0 else pltpu.roll(rev, shift, axis=1)
        if t < S5_CHUNK - 1:
            blk = jnp.where(lane < (t + 1) * S5_GROUP, blk, 0.0)
        mt_ref[0, rows, :] = blk.astype(mt_ref.dtype)

    lrr = lrr_ref[0]
    lir = lir_ref[0]
    tau = (lax.broadcasted_iota(jnp.int32, (S5_CHUNK, 1), 0) + 1).astype(F32)
    st_r = step * tau
    mag_r = jnp.exp(lrr * st_r)
    pre_r = mag_r * jnp.cos(lir * st_r)
    pim_r = mag_r * jnp.sin(lir * st_r)
    q1 = jnp.where(first, pre_r, pim_r)
    q2 = jnp.where(first, pim_r, pre_r)
    s1 = jnp.where(first, 1.0, -1.0).astype(F32)
    for t in range(S5_CHUNK):
        rows = slice(t * S5_GROUP, (t + 1) * S5_GROUP)
        ct_ref[0, rows, :] = (s1 * (c1 * q1[t:t + 1, :]) - c2 * q2[t:t + 1, :]).astype(ct_ref.dtype)
    atr = pre_r[S5_CHUNK - 1:S5_CHUNK, :]
    a2 = -s1 * pim_r[S5_CHUNK - 1:S5_CHUNK, :]
    apow_ref[0] = jnp.concatenate(
        [atr, a2, -a2, jnp.zeros((SUBLANES - 3, p2), F32)], axis=0)


def _s5_prep(ls, lrc, lic, lrr, lir, cre, cim, bre, bim):
    dg = ls.shape[0]
    spec = lambda a: pl.BlockSpec((1,) + a.shape[1:], lambda i: (i, 0, 0))
    ins = (ls, lrc, lic, lrr, lir, cre, cim, bre, bim)
    shapes = [(dg, S5_FLAT, S5_FLAT), (dg, 4 * S5_STATE, S5_FLAT),
              (dg, S5_FLAT, 2 * S5_STATE), (dg, SUBLANES, 2 * S5_STATE)]
    dtypes = [BF16, BF16, BF16, F32]
    return pl.pallas_call(
        _s5_prep_kernel,
        grid=(dg,),
        in_specs=[spec(a) for a in ins],
        out_specs=[pl.BlockSpec((1,) + s[1:], lambda i: (i, 0, 0)) for s in shapes],
        out_shape=[jax.ShapeDtypeStruct(s, d) for s, d in zip(shapes, dtypes)],
        compiler_params=_params("parallel"),
        name="s5_prep",
    )(*ins)


def _s5_core_kernel(u_hbm, mt_ref, bt_ref, ct_ref, ap_ref, y_hbm,
                    io_buf, zt_scr, e_scr, sp_scr, yt_scr, sem_in, sem_out, *, batch):
    ng = mt_ref.shape[0]
    nchunk = u_hbm.shape[0]
    cols = nchunk * batch
    j = pl.program_id(0)

    def in_copy(t):
        return pltpu.make_async_copy(u_hbm.at[:, j, t], io_buf.at[t], sem_in.at[t])

    def out_copy(t):
        return pltpu.make_async_copy(io_buf.at[t], y_hbm.at[:, j, t], sem_out.at[t])

    for t in range(S5_CHUNK):
        in_copy(t).start()
    for t in range(S5_CHUNK):
        in_copy(t).wait()
        blk_t = io_buf[t].reshape(cols, LANES).T.astype(BF16)
        for g in range(ng):
            zt_scr[g, t * S5_GROUP:(t + 1) * S5_GROUP, :] = blk_t[g * S5_GROUP:(g + 1) * S5_GROUP, :]

    for g in range(ng):
        e_scr[g] = _dot(bt_ref[g], zt_scr[g]).T

    def body(c, carry):
        r = pl.multiple_of(c * batch, batch)
        out = []
        for g in range(ng):
            s, sw = carry[g]
            sp_scr[g, pl.ds(r, batch), :] = s
            e = e_scr[g, pl.ds(r, batch), :]
            a1 = ap_ref[g, 0:1, :]
            a2 = ap_ref[g, 1:2, :]
            a2w = ap_ref[g, 2:3, :]
            out.append((a1 * s + a2 * sw + e[:, :LANES], a1 * sw + a2w * s + e[:, LANES:]))
        return tuple(out)

    zero = jnp.zeros((batch, 2 * S5_STATE), F32)
    lax.fori_loop(0, nchunk, body, tuple((zero, zero) for _ in range(ng)))

    for g in range(ng):
        y_off = lax.dot_general(ct_ref[g], sp_scr[g].astype(BF16), (((1,), (1,)), ((), ())),
                                preferred_element_type=F32)
        yt_scr[g] = (_dot(mt_ref[g], zt_scr[g]) + y_off).astype(yt_scr.dtype)

    for t in range(S5_CHUNK):
        rows = slice(t * S5_GROUP, (t + 1) * S5_GROUP)
        blk = jnp.concatenate([yt_scr[g, rows, :] for g in range(ng)], axis=0)
        io_buf[t] = blk.astype(F32).T.reshape(nchunk, batch, LANES)
        out_copy(t).start()
    for t in range(S5_CHUNK):
        out_copy(t).wait()


def _s5_core(u5, mt, bt, ct, apow, layer):
    nchunk, nblk, _, batch, _ = u5.shape
    cols = nchunk * batch
    ng = S5_GPB
    op = lambda a: pl.BlockSpec((ng,) + a.shape[1:], lambda i: (layer * nblk + i, 0, 0))
    hbm = pl.BlockSpec(memory_space=pl.ANY)
    return pl.pallas_call(
        functools.partial(_s5_core_kernel, batch=batch),
        grid=(nblk,),
        in_specs=[hbm, op(mt), op(bt), op(ct), op(apow)],
        out_specs=hbm,
        out_shape=jax.ShapeDtypeStruct(u5.shape, F32),
        scratch_shapes=[pltpu.VMEM((S5_CHUNK, nchunk, batch, LANES), F32),
                        pltpu.VMEM((ng, S5_FLAT, cols), BF16),
                        pltpu.VMEM((ng, cols, 4 * S5_STATE), F32),
                        pltpu.VMEM((ng, cols, 2 * S5_STATE), F32),
                        pltpu.VMEM((ng, S5_FLAT, cols), BF16),
                        pltpu.SemaphoreType.DMA((S5_CHUNK,)),
                        pltpu.SemaphoreType.DMA((S5_CHUNK,))],
        compiler_params=_params("arbitrary"),
        name="s5_core",
    )(u5, mt, bt, ct, apow)


def _ssd_kernel(z_ref, xbc_ref, dt_ref, cw_ref, cb_ref, dtb_ref, alog_ref, dsk_ref,
                gn_ref, e64_ref, y_ref, buf_scr, st_scr):
    lb = z_ref.shape[0]
    j = pl.program_id(1)

    @pl.when(j == 0)
    def _():
        buf_scr[0:SUBLANES, :] = jnp.zeros((SUBLANES, SSD_CONV_DIM), F32)
        st_scr[...] = jnp.zeros_like(st_scr)

    @pl.when(j > 0)
    def _():
        buf_scr[0:SUBLANES, :] = buf_scr[lb:lb + SUBLANES, :]

    buf_scr[SUBLANES:SUBLANES + lb, :] = xbc_ref[...].astype(F32)
    acc = cb_ref[...] + cw_ref[0:1, :] * buf_scr[pl.ds(SUBLANES - 3, lb), :]
    for k in range(1, SSD_CONV):
        acc = acc + cw_ref[k:k + 1, :] * buf_scr[pl.ds(SUBLANES - 3 + k, lb), :]
    xc = acc * jax.nn.sigmoid(acc)

    ri = lax.broadcasted_iota(jnp.int32, (SSD_CHUNK, SSD_CHUNK), 0)
    ci = lax.broadcasted_iota(jnp.int32, (SSD_CHUNK, SSD_CHUNK), 1)
    causal = ri >= ci
    ltri = causal.astype(BF16)
    lo_half = ci < SSD_HEAD_DIM
    a_neg = -jnp.exp(alog_ref[...])
    e64 = e64_ref[...]
    b0 = SSD_WIDTH
    c0 = SSD_WIDTH + SSD_GROUPS * SSD_STATE
    hpg = SSD_HEADS // SSD_GROUPS

    for c in range(lb // SSD_CHUNK):
        r0 = c * SSD_CHUNK
        xs = xc[r0:r0 + SSD_CHUNK, 0:SSD_WIDTH]
        dtv = dt_ref[r0:r0 + SSD_CHUNK, :] + dtb_ref[...]
        dtp = jnp.maximum(dtv, 0.0) + jnp.log(1.0 + jnp.exp(-jnp.abs(dtv)))
        dta = dtp * a_neg
        acum = _dot_sel_lhs(ltri, dta)
        acum_t = acum.T
        dtp64 = _dot_sel_rhs(dtp, e64)
        acum64 = _dot_sel_rhs(acum, e64)
        atot64 = acum64[SSD_CHUNK - 1:SSD_CHUNK, :]
        xdt = xs * dtp64
        xw = (xdt * jnp.exp(atot64 - acum64)).astype(BF16)
        ea64 = jnp.exp(acum64)
        gmats = []
        for g in range(SSD_GROUPS):
            bm = xc[r0:r0 + SSD_CHUNK, b0 + g * SSD_STATE:b0 + (g + 1) * SSD_STATE].astype(BF16)
            cm = xc[r0:r0 + SSD_CHUNK, c0 + g * SSD_STATE:c0 + (g + 1) * SSD_STATE].astype(BF16)
            gm = lax.dot_general(cm, bm, (((1,), (1,)), ((), ())), preferred_element_type=F32)
            gmats.append((bm, cm, gm))
        ys = []
        for p in range(SSD_HEADS // 2):
            bm, cm, gm = gmats[(2 * p) // hpg]
            cols = slice(p * LANES, (p + 1) * LANES)
            xp = xdt[:, cols]
            y = None
            for hh in range(2):
                h = 2 * p + hh
                col = jnp.broadcast_to(acum[:, h:h + 1], (SSD_CHUNK, SSD_CHUNK))
                rowv = acum_t[h:h + 1, :]
                w = jnp.where(causal, gm * jnp.exp(col - rowv), 0.0).astype(BF16)
                xh = jnp.where(lo_half if hh == 0 else ~lo_half, xp, 0.0).astype(BF16)
                t = _dot(w, xh)
                y = t if y is None else y + t
            st = st_scr[p]
            y = y + _dot(cm, st.astype(BF16)) * ea64[:, cols]
            st_scr[p] = st * jnp.exp(atot64[:, cols]) + lax.dot_general(
                bm, xw[:, cols], (((0,), (0,)), ((), ())), preferred_element_type=F32)
            ys.append(y + dsk_ref[:, cols] * xs[:, cols])
        yc = jnp.concatenate(ys, axis=1)
        zc = z_ref[r0:r0 + SSD_CHUNK, :].astype(F32)
        yc = yc * (zc * jax.nn.sigmoid(zc))
        y_ref[r0:r0 + SSD_CHUNK, :] = _rms(yc, gn_ref[...]).astype(y_ref.dtype)


def _ssd(z3, xbc3, dt3, cw, cb, dtb, alog, dsk, gn, e64, layer, lb):
    b, l, _ = z3.shape
    blk = lambda w: pl.BlockSpec((None, lb, w), lambda i, j: (i, j, 0))
    ls = lambda a: _layer_spec(a.shape, layer, 2)
    return pl.pallas_call(
        _ssd_kernel,
        grid=(b, l // lb),
        in_specs=[blk(SSD_WIDTH), blk(SSD_CONV_DIM), blk(LANES), ls(cw), ls(cb),
                  ls(dtb), ls(alog), ls(dsk), ls(gn),
                  pl.BlockSpec(e64.shape, lambda i, j: (0, 0))],
        out_specs=blk(SSD_WIDTH),
        out_shape=jax.ShapeDtypeStruct((b, l, SSD_WIDTH), BF16),
        scratch_shapes=[pltpu.VMEM((lb + 2 * SUBLANES, SSD_CONV_DIM), F32),
                        pltpu.VMEM((SSD_HEADS // 2, SSD_STATE, LANES), F32)],
        compiler_params=_params("parallel", "arbitrary"),
        name="ssd",
    )(z3, xbc3, dt3, cw, cb, dtb, alog, dsk, gn, e64)


FFN_CHUNK = 256


def _gelu_tanh(v):
    return 0.5 * v * (1.0 + jnp.tanh(0.7978845608028654 * (v + 0.044715 * (v * v * v))))


def _post_kernel(x_ref, ys_ref, u_ref, yb_ref, d5_ref, wglu_ref, bglu_ref, g5_ref,
                 woa_ref, wob_ref, gf_ref, wg_ref, wu_ref, wd_ref, gfin_ref, o_ref, y_scr,
                 *, final):
    bsz, tok, _ = x_ref.shape
    for b in range(bsz):
        for c in range(tok // S5_CHUNK):
            r0 = b * tok + c * S5_CHUNK
            for j in range(D_MODEL // LANES):
                cols = slice(j * LANES, (j + 1) * LANES)
                rows = _s5_rows(c, j, b)
                y_scr[r0:r0 + S5_CHUNK, cols] = ys_ref[rows, :] + d5_ref[:, cols] * u_ref[rows, :]
    g = _gelu_tanh(y_scr[...])
    ya = g * jax.nn.sigmoid(_dot(g.astype(BF16), wglu_ref[...]) + bglu_ref[...])
    ya = _rms(ya, g5_ref[...]).astype(BF16)
    yb = yb_ref[...].reshape(bsz * tok, SSD_WIDTH)
    x = x_ref[...].reshape(bsz * tok, D_MODEL) + _dot(ya, woa_ref[...]) + _dot(yb, wob_ref[...])
    hb = _rms(x, gf_ref[...]).astype(BF16)
    y_scr[...] = x

    def ffn_chunk(k, carry):
        sl = pl.ds(pl.multiple_of(k * FFN_CHUNK, FFN_CHUNK), FFN_CHUNK)
        gate = _dot(hb, wg_ref[:, sl])
        act = (gate * jax.nn.sigmoid(gate)) * _dot(hb, wu_ref[:, sl])
        y_scr[...] += _dot(act.astype(BF16), wd_ref[sl, :])
        return carry

    lax.fori_loop(0, FFN_HIDDEN // FFN_CHUNK, ffn_chunk, 0)
    out = y_scr[...]
    if final:
        out = _rms(out, gfin_ref[...])
    o_ref[...] = out.reshape(o_ref.shape)


def _post(x3, ys, u, yb3, d5, wglu, bglu, g5, wo, gf, wg, wu, wd, gfin, layer, tok, final):
    b, seq, _ = x3.shape
    nat = pl.BlockSpec((b, tok, D_MODEL), lambda i: (0, i, 0))
    s5 = pl.BlockSpec((tok * b * D_MODEL // LANES, LANES), lambda i: (i, 0))
    one = pl.Buffered(1)
    ls = lambda a: pl.BlockSpec((None,) + a.shape[1:], lambda i: (layer, 0, 0), pipeline_mode=one)
    wo_half = lambda k: pl.BlockSpec((None, D_MODEL, D_MODEL), lambda i: (layer, k, 0),
                                     pipeline_mode=one)
    gfin_spec = pl.BlockSpec(gfin.shape, lambda i: (0, 0), pipeline_mode=one)
    return pl.pallas_call(
        functools.partial(_post_kernel, final=final),
        grid=(seq // tok,),
        in_specs=[nat, s5, s5, nat, ls(d5), ls(wglu), ls(bglu), ls(g5), wo_half(0), wo_half(1),
                  ls(gf), ls(wg), ls(wu), ls(wd), gfin_spec],
        out_specs=nat,
        out_shape=jax.ShapeDtypeStruct(x3.shape, F32),
        scratch_shapes=[pltpu.VMEM((b * tok, D_MODEL), F32)],
        compiler_params=_params("parallel"),
        name="post",
    )(x3, ys, u, yb3, d5, wglu, bglu, g5, wo, wo, gf, wg, wu, wd, gfin)


def kernel(x, norm_mix, w_in, s5_lam_re, s5_lam_im, s5_log_step, s5_b_re, s5_b_im, s5_c_re, s5_c_im, s5_d, s5_w_glu, s5_b_glu, s5_norm, ssd_conv_w, ssd_conv_b, ssd_dt_bias, ssd_a_log, ssd_d, ssd_norm, w_out, norm_ffn, w_gate, w_up, w_down, norm_final):
    bsz, seq, _ = x.shape
    depth = w_in.shape[0]
    assert bsz == SUBLANES and seq % S5_CHUNK == 0 and seq % SSD_CHUNK == 0
    nc5 = seq // S5_CHUNK
    tok = 2 * S5_CHUNK
    lb = min(256, seq)
    dg = depth * S5_GROUPS
    rows = lambda a: a.reshape(depth, 1, -1).astype(F32)

    wu, wz, wx, wdt = _cast_win(w_in, 256)
    wglu = _cast_bf16(s5_w_glu, 256)
    wo = _cast_bf16(w_out, 256)
    wg = _cast_bf16(w_gate, 256)
    wup = _cast_bf16(w_up, 256)
    wd = _cast_bf16(w_down, 256)

    dbl = lambda a: jnp.concatenate([a, a], axis=-1)
    mt, bt, ct, apow = _s5_prep(
        s5_log_step.reshape(dg, 1, 1),
        s5_lam_re.reshape(dg, S5_STATE, 1), s5_lam_im.reshape(dg, S5_STATE, 1),
        dbl(s5_lam_re).reshape(dg, 1, 2 * S5_STATE), dbl(s5_lam_im).reshape(dg, 1, 2 * S5_STATE),
        dbl(s5_c_re).reshape(dg, S5_GROUP, 2 * S5_STATE),
        dbl(s5_c_im).reshape(dg, S5_GROUP, 2 * S5_STATE),
        s5_b_re.reshape(dg, S5_STATE, S5_GROUP), s5_b_im.reshape(dg, S5_STATE, S5_GROUP))

    head_of_col = jnp.arange(SSD_WIDTH) // SSD_HEAD_DIM
    e64 = (jnp.arange(LANES)[:, None] == head_of_col[None, :]).astype(BF16)
    pad_h = lambda a: jnp.pad(a.astype(F32), ((0, 0), (0, LANES - SSD_HEADS))).reshape(
        depth, 1, LANES)
    dtb, alog = pad_h(ssd_dt_bias), pad_h(ssd_a_log)
    dsk = rows(jnp.repeat(ssd_d, SSD_HEAD_DIM, axis=-1))
    g_mix, g_s5, g_ssd, g_ffn = rows(norm_mix), rows(s5_norm), rows(ssd_norm), rows(norm_ffn)
    d5, bglu, cb = rows(s5_d), rows(s5_b_glu), rows(ssd_conv_b)
    cw = ssd_conv_w.astype(F32)
    gfin = norm_final.reshape(1, -1).astype(F32)

    s5_shape = (nc5, D_MODEL // LANES, S5_CHUNK, bsz, LANES)
    for i in range(depth):
        u, z, xbc, dt = _in_proj(x, g_mix, wu, wz, wx, wdt, i, tok)
        ys = _s5_core(u.reshape(s5_shape), mt, bt, ct, apow, i).reshape(u.shape)
        yb = _ssd(z, xbc, dt, cw, cb, dtb, alog, dsk, g_ssd, e64, i, lb)
        x = _post(x, ys, u, yb, d5, wglu, bglu, g_s5, wo, g_ffn, wg, wup, wd, gfin,
                  i, tok, i == depth - 1)
    return x
```

```python
import functools

import jax
import jax.numpy as jnp
from jax import lax
from jax.experimental import pallas as pl
from jax.experimental.pallas import tpu as pltpu

F32 = jnp.float32
BF16 = jnp.bfloat16
EPS = 1e-6

D_MODEL = 1024
S5_GROUP = 16
S5_GROUPS = 64
S5_STATE = 64
S5_CHUNK = 32
S5_FLAT = S5_CHUNK * S5_GROUP
SSD_HEAD_DIM = 64
SSD_HEADS = 16
SSD_GROUPS = 2
SSD_STATE = 128
SSD_CONV = 4
SSD_CHUNK = 128
SSD_WIDTH = SSD_HEADS * SSD_HEAD_DIM
SSD_CONV_DIM = SSD_WIDTH + 2 * SSD_GROUPS * SSD_STATE
FFN_HIDDEN = 2816
LANES = 128
SUBLANES = 8
S5_GPB = LANES // S5_GROUP
VMEM_LIMIT = 56 * 1024 * 1024
IN_U = D_MODEL
IN_Z = IN_U + SSD_WIDTH
IN_X = IN_Z + SSD_CONV_DIM


def _params(*sem):
    return pltpu.CompilerParams(dimension_semantics=sem, vmem_limit_bytes=VMEM_LIMIT)


def _rms(x, gain):
    return x * lax.rsqrt(jnp.mean(x * x, axis=-1, keepdims=True) + EPS) * gain


def _dot(a, b):
    return jnp.dot(a, b, preferred_element_type=F32)


def _split3(v):
    hi = v.astype(BF16)
    r1 = v - hi.astype(F32)
    mid = r1.astype(BF16)
    lo = (r1 - mid.astype(F32)).astype(BF16)
    return hi, mid, lo


def _dot_sel_rhs(v, sel):
    hi, mid, lo = _split3(v)
    return _dot(hi, sel) + _dot(mid, sel) + _dot(lo, sel)


def _dot_sel_lhs(sel, v):
    hi, mid, lo = _split3(v)
    return _dot(sel, hi) + _dot(sel, mid) + _dot(sel, lo)


def _dot_f32(a, b):
    a3 = _split3(a)
    b3 = _split3(b)
    acc = None
    for i in range(3):
        for j in range(3 - i):
            t = _dot(a3[i], b3[j])
            acc = t if acc is None else acc + t
    return acc


def _layer_spec(shape, layer, n_grid, col=0):
    zeros = (0,) * (len(shape) - 2)
    if n_grid == 1:
        return pl.BlockSpec((None,) + shape[1:], lambda i: (layer,) + zeros + (col,))
    return pl.BlockSpec((None,) + shape[1:], lambda i, j: (layer,) + zeros + (col,))


def _cast_kernel(w_ref, o_ref):
    o_ref[...] = w_ref[...].astype(o_ref.dtype)


def _cast_bf16(w, rb):
    depth, rows, cols = w.shape
    spec = pl.BlockSpec((1, rb, cols), lambda i, j: (i, j, 0))
    return pl.pallas_call(
        _cast_kernel, grid=(depth, rows // rb), in_specs=[spec], out_specs=spec,
        out_shape=jax.ShapeDtypeStruct(w.shape, BF16),
        compiler_params=_params("parallel", "parallel"), name="cast",
    )(w)


def _cast_win_kernel(w_ref, wu_ref, wz_ref, wx_ref, wdt_ref):
    wu_ref[0] = w_ref[0, :, 0:IN_U].astype(BF16)
    wz_ref[0] = w_ref[0, :, IN_U:IN_Z].astype(BF16)
    wx_ref[0] = w_ref[0, :, IN_Z:IN_X].astype(BF16)
    wdt_ref[0] = jnp.zeros(wdt_ref.shape[1:], BF16)
    wdt_ref[0, :, 0:SSD_HEADS] = w_ref[0, :, IN_X:IN_X + SSD_HEADS].astype(BF16)


def _cast_win(w_in, rb):
    depth, rows, cols = w_in.shape
    spec = lambda c: pl.BlockSpec((1, rb, c), lambda i, j: (i, j, 0))
    shp = lambda c: jax.ShapeDtypeStruct((depth, rows, c), BF16)
    return pl.pallas_call(
        _cast_win_kernel, grid=(depth, rows // rb), in_specs=[spec(cols)],
        out_specs=[spec(D_MODEL), spec(SSD_WIDTH), spec(SSD_CONV_DIM), spec(LANES)],
        out_shape=[shp(D_MODEL), shp(SSD_WIDTH), shp(SSD_CONV_DIM), shp(LANES)],
        compiler_params=_params("parallel", "parallel"), name="cast_win",
    )(w_in)


def _s5_rows(c, j, b):
    return pl.ds(((c * (D_MODEL // LANES) + j) * S5_CHUNK) * SUBLANES + b, S5_CHUNK,
                 stride=SUBLANES)


def _inproj_kernel(x_ref, g_ref, wu_ref, wz_ref, wx_ref, wdt_ref,
                   u_ref, z_ref, xbc_ref, dt_ref):
    bsz, tok, _ = x_ref.shape
    hb = _rms(x_ref[...], g_ref[...]).astype(BF16).reshape(bsz * tok, D_MODEL)
    u = _dot(hb, wu_ref[...])
    for b in range(bsz):
        for c in range(tok // S5_CHUNK):
            r0 = b * tok + c * S5_CHUNK
            for j in range(D_MODEL // LANES):
                u_ref[_s5_rows(c, j, b), :] = u[r0:r0 + S5_CHUNK, j * LANES:(j + 1) * LANES]
    z_ref[...] = _dot(hb, wz_ref[...]).astype(z_ref.dtype).reshape(z_ref.shape)
    xbc_ref[...] = _dot(hb, wx_ref[...]).astype(xbc_ref.dtype).reshape(xbc_ref.shape)
    dt_ref[...] = _dot(hb, wdt_ref[...]).reshape(dt_ref.shape)


def _in_proj(x3, gain, wu, wz, wx, wdt, layer, tok):
    b, seq, _ = x3.shape
    nat = lambda w: pl.BlockSpec((b, tok, w), lambda i: (0, i, 0))
    ls = lambda a: _layer_spec(a.shape, layer, 1)
    return pl.pallas_call(
        _inproj_kernel,
        grid=(seq // tok,),
        in_specs=[nat(D_MODEL), ls(gain), ls(wu), ls(wz), ls(wx), ls(wdt)],
        out_specs=[pl.BlockSpec((tok * b * D_MODEL // LANES, LANES), lambda i: (i, 0)),
                   nat(SSD_WIDTH), nat(SSD_CONV_DIM), nat(LANES)],
        out_shape=[jax.ShapeDtypeStruct((seq * b * D_MODEL // LANES, LANES), F32),
                   jax.ShapeDtypeStruct((b, seq, SSD_WIDTH), BF16),
                   jax.ShapeDtypeStruct((b, seq, SSD_CONV_DIM), BF16),
                   jax.ShapeDtypeStruct((b, seq, LANES), F32)],
        compiler_params=_params("parallel"),
        name="in_proj",
    )(x3, gain, wu, wz, wx, wdt)


def _s5_prep_kernel(ls_ref, lrc_ref, lic_ref, lrr_ref, lir_ref, cre_ref, cim_ref,
                    bre_ref, bim_ref, mt_ref, bt_ref, ct_ref, apow_ref):
    step = jnp.exp(ls_ref[0])
    lrc = lrc_ref[0]
    lic = lic_ref[0]
    p2 = 2 * S5_STATE

    k_lane = lax.broadcasted_iota(jnp.int32, (1, LANES), 1).astype(F32)
    st = step * k_lane
    mag = jnp.exp(lrc * st)
    pre_c = mag * jnp.cos(lic * st)
    pim_c = mag * jnp.sin(lic * st)
    are = pre_c[:, 1:2]
    aim = pim_c[:, 1:2]
    den = lrc * lrc + lic * lic
    nr = are - 1.0
    cfr = (nr * lrc + aim * lic) / den
    cfi = (aim * lrc - nr * lic) / den
    bre = bre_ref[0]
    bim = bim_ref[0]
    bbr = cfr * bre - cfi * bim
    bbi = cfr * bim + cfi * bre

    lane = lax.broadcasted_iota(jnp.int32, (1, S5_FLAT), 1)
    h_of_lane = lax.broadcasted_iota(jnp.int32, (S5_GROUP, S5_FLAT), 1) % S5_GROUP
    e_h = (h_of_lane == lax.broadcasted_iota(jnp.int32, (S5_GROUP, S5_FLAT), 0)).astype(BF16)
    k_row = lax.broadcasted_iota(jnp.int32, (LANES, S5_FLAT), 0)
    j_lane = lax.broadcasted_iota(jnp.int32, (LANES, S5_FLAT), 1) // S5_GROUP
    e_rev = (k_row == (S5_CHUNK - 1) - j_lane).astype(BF16)
    btr = _dot_sel_rhs(bbr, e_h)
    bti = _dot_sel_rhs(bbi, e_h)
    prr = _dot_sel_rhs(pre_c, e_rev)
    pri = _dot_sel_rhs(pim_c, e_rev)
    wre = prr * btr - pri * bti
    wim = prr * bti + pri * btr
    bt_ref[0] = jnp.concatenate([wre, wim, wim, wre], axis=0).astype(bt_ref.dtype)

    lane2 = lax.broadcasted_iota(jnp.int32, (1, p2), 1)
    first = lane2 < S5_STATE
    c1 = cre_ref[0]
    c2 = cim_ref[0]
    lhs = jnp.where(first, c1, c2)
    rev = _dot_f32(lhs, jnp.concatenate([wre, -wim], axis=0))
    for t in range(S5_CHUNK):
        rows = slice(t * S5_GROUP, (t + 1) * S5_GROUP)
        shift = (S5_FLAT - (S5_CHUNK - 1 - t) * S5_GROUP) % S5_FLAT
        blk = rev if shift == 0 else pltpu.roll(rev, shift, axis=1)
        if t < S5_CHUNK - 1:
            blk = jnp.where(lane < (t + 1) * S5_GROUP, blk, 0.0)
        mt_ref[0, rows, :] = blk.astype(mt_ref.dtype)

    lrr = lrr_ref[0]
    lir = lir_ref[0]
    tau = (lax.broadcasted_iota(jnp.int32, (S5_CHUNK, 1), 0) + 1).astype(F32)
    st_r = step * tau
    mag_r = jnp.exp(lrr * st_r)
    pre_r = mag_r * jnp.cos(lir * st_r)
    pim_r = mag_r * jnp.sin(lir * st_r)
    q1 = jnp.where(first, pre_r, pim_r)
    q2 = jnp.where(first, pim_r, pre_r)
    s1 = jnp.where(first, 1.0, -1.0).astype(F32)
    for t in range(S5_CHUNK):
        rows = slice(t * S5_GROUP, (t + 1) * S5_GROUP)
        ct_ref[0, rows, :] = (s1 * (c1 * q1[t:t + 1, :]) - c2 * q2[t:t + 1, :]).astype(ct_ref.dtype)
    atr = pre_r[S5_CHUNK - 1:S5_CHUNK, :]
    a2 = -s1 * pim_r[S5_CHUNK - 1:S5_CHUNK, :]
    apow_ref[0] = jnp.concatenate(
        [atr, a2, -a2, jnp.zeros((SUBLANES - 3, p2), F32)], axis=0)


def _s5_prep(ls, lrc, lic, lrr, lir, cre, cim, bre, bim):
    dg = ls.shape[0]
    spec = lambda a: pl.BlockSpec((1,) + a.shape[1:], lambda i: (i, 0, 0))
    ins = (ls, lrc, lic, lrr, lir, cre, cim, bre, bim)
    shapes = [(dg, S5_FLAT, S5_FLAT), (dg, 4 * S5_STATE, S5_FLAT),
              (dg, S5_FLAT, 2 * S5_STATE), (dg, SUBLANES, 2 * S5_STATE)]
    dtypes = [BF16, BF16, BF16, F32]
    return pl.pallas_call(
        _s5_prep_kernel,
        grid=(dg,),
        in_specs=[spec(a) for a in ins],
        out_specs=[pl.BlockSpec((1,) + s[1:], lambda i: (i, 0, 0)) for s in shapes],
        out_shape=[jax.ShapeDtypeStruct(s, d) for s, d in zip(shapes, dtypes)],
        compiler_params=_params("parallel"),
        name="s5_prep",
    )(*ins)


def _s5_core_kernel(u_hbm, mt_ref, bt_ref, ct_ref, ap_ref, y_hbm,
                    io_buf, zt_scr, e_scr, sp_scr, yt_scr, sem_in, sem_out, *, batch):
    ng = mt_ref.shape[0]
    nchunk = u_hbm.shape[0]
    cols = nchunk * batch
    half = S5_FLAT // 2
    j = pl.program_id(0)
    last = pl.num_programs(0) - 1
    slot = j % 2
    other = 1 - slot

    def in_copy(step, s, t):
        return pltpu.make_async_copy(u_hbm.at[:, step, t], io_buf.at[s, t], sem_in.at[s, t])

    def out_copy(step, s, t):
        return pltpu.make_async_copy(io_buf.at[s, t], y_hbm.at[:, step, t], sem_out.at[s, t])

    @pl.when(j == 0)
    def _():
        for t in range(S5_CHUNK):
            in_copy(j, slot, t).start()

    for t in range(S5_CHUNK):
        in_copy(j, slot, t).wait()
    for t in range(S5_CHUNK):
        blk_t = io_buf[slot, t].reshape(cols, LANES).T.astype(BF16)
        for g in range(ng):
            zt_scr[g, t * S5_GROUP:(t + 1) * S5_GROUP, :] = blk_t[g * S5_GROUP:(g + 1) * S5_GROUP, :]

    @pl.when(j > 0)
    def _():
        for t in range(S5_CHUNK):
            out_copy(j - 1, other, t).wait()

    @pl.when(j < last)
    def _():
        for t in range(S5_CHUNK):
            in_copy(j + 1, other, t).start()

    for g in range(ng):
        e_scr[g] = _dot(bt_ref[g], zt_scr[g]).T

    def body(c, carry):
        r = pl.multiple_of(c * batch, batch)
        out = []
        for g in range(ng):
            s, sw = carry[g]
            sp_scr[g, pl.ds(r, batch), :] = s
            e = e_scr[g, pl.ds(r, batch), :]
            a1 = ap_ref[g, 0:1, :]
            a2 = ap_ref[g, 1:2, :]
            a2w = ap_ref[g, 2:3, :]
            out.append((a1 * s + a2 * sw + e[:, :LANES], a1 * sw + a2w * s + e[:, LANES:]))
        return tuple(out)

    zero = jnp.zeros((batch, 2 * S5_STATE), F32)
    lax.fori_loop(0, nchunk, body, tuple((zero, zero) for _ in range(ng)))

    for g in range(ng):
        y_off = lax.dot_general(ct_ref[g], sp_scr[g].astype(BF16), (((1,), (1,)), ((), ())),
                                preferred_element_type=F32)
        top = _dot(mt_ref[g, :half, :half], zt_scr[g, :half, :])
        bot = _dot(mt_ref[g, half:, :], zt_scr[g])
        yt_scr[g, :half, :] = (top + y_off[:half]).astype(yt_scr.dtype)
        yt_scr[g, half:, :] = (bot + y_off[half:]).astype(yt_scr.dtype)

    for t in range(S5_CHUNK):
        rows = slice(t * S5_GROUP, (t + 1) * S5_GROUP)
        blk = jnp.concatenate([yt_scr[g, rows, :] for g in range(ng)], axis=0)
        io_buf[slot, t] = blk.astype(F32).T.reshape(nchunk, batch, LANES)
        out_copy(j, slot, t).start()

    @pl.when(j == last)
    def _():
        for t in range(S5_CHUNK):
            out_copy(j, slot, t).wait()


def _s5_core(u5, mt, bt, ct, apow, layer):
    nchunk, nblk, _, batch, _ = u5.shape
    cols = nchunk * batch
    ng = S5_GPB
    op = lambda a: pl.BlockSpec((ng,) + a.shape[1:], lambda i: (layer * nblk + i, 0, 0))
    hbm = pl.BlockSpec(memory_space=pl.ANY)
    return pl.pallas_call(
        functools.partial(_s5_core_kernel, batch=batch),
        grid=(nblk,),
        in_specs=[hbm, op(mt), op(bt), op(ct), op(apow)],
        out_specs=hbm,
        out_shape=jax.ShapeDtypeStruct(u5.shape, F32),
        scratch_shapes=[pltpu.VMEM((2, S5_CHUNK, nchunk, batch, LANES), F32),
                        pltpu.VMEM((ng, S5_FLAT, cols), BF16),
                        pltpu.VMEM((ng, cols, 4 * S5_STATE), F32),
                        pltpu.VMEM((ng, cols, 2 * S5_STATE), F32),
                        pltpu.VMEM((ng, S5_FLAT, cols), BF16),
                        pltpu.SemaphoreType.DMA((2, S5_CHUNK)),
                        pltpu.SemaphoreType.DMA((2, S5_CHUNK))],
        compiler_params=_params("arbitrary"),
        name="s5_core",
    )(u5, mt, bt, ct, apow)


def _ssd_kernel(z_ref, xbc_ref, dt_ref, cw_ref, cb_ref, dtb_ref, alog_ref, dsk_ref,
                gn_ref, e64_ref, y_ref, buf_scr, st_scr):
    lb = z_ref.shape[0]
    j = pl.program_id(1)

    @pl.when(j == 0)
    def _():
        buf_scr[0:SUBLANES, :] = jnp.zeros((SUBLANES, SSD_CONV_DIM), F32)
        st_scr[...] = jnp.zeros_like(st_scr)

    @pl.when(j > 0)
    def _():
        buf_scr[0:SUBLANES, :] = buf_scr[lb:lb + SUBLANES, :]

    buf_scr[SUBLANES:SUBLANES + lb, :] = xbc_ref[...].astype(F32)
    acc = cb_ref[...] + cw_ref[0:1, :] * buf_scr[pl.ds(SUBLANES - 3, lb), :]
    for k in range(1, SSD_CONV):
        acc = acc + cw_ref[k:k + 1, :] * buf_scr[pl.ds(SUBLANES - 3 + k, lb), :]
    xc = acc * jax.nn.sigmoid(acc)

    ri = lax.broadcasted_iota(jnp.int32, (SSD_CHUNK, SSD_CHUNK), 0)
    ci = lax.broadcasted_iota(jnp.int32, (SSD_CHUNK, SSD_CHUNK), 1)
    causal = ri >= ci
    ltri = causal.astype(BF16)
    lo_half = ci < SSD_HEAD_DIM
    a_neg = -jnp.exp(alog_ref[...])
    e64 = e64_ref[...]
    b0 = SSD_WIDTH
    c0 = SSD_WIDTH + SSD_GROUPS * SSD_STATE
    hpg = SSD_HEADS // SSD_GROUPS

    for c in range(lb // SSD_CHUNK):
        r0 = c * SSD_CHUNK
        xs = xc[r0:r0 + SSD_CHUNK, 0:SSD_WIDTH]
        dtv = dt_ref[r0:r0 + SSD_CHUNK, :] + dtb_ref[...]
        dtp = jnp.maximum(dtv, 0.0) + jnp.log(1.0 + jnp.exp(-jnp.abs(dtv)))
        dta = dtp * a_neg
        acum = _dot_sel_lhs(ltri, dta)
        acum_t = acum.T
        atot = acum[SSD_CHUNK - 1:SSD_CHUNK, :]
        fac = jnp.concatenate([dtp, dtp * jnp.exp(atot - acum), jnp.exp(acum)], axis=0)
        fac64 = _dot(fac.astype(BF16), e64)
        xdt = xs * fac64[0:SSD_CHUNK]
        xw = (xs * fac64[SSD_CHUNK:2 * SSD_CHUNK]).astype(BF16)
        ea64 = fac64[2 * SSD_CHUNK:]
        tail = SSD_CHUNK - SUBLANES
        etot64 = jnp.exp(_dot_sel_rhs(acum[tail:, :], e64)[SUBLANES - 1:SUBLANES, :])
        gmats = []
        for g in range(SSD_GROUPS):
            bm = xc[r0:r0 + SSD_CHUNK, b0 + g * SSD_STATE:b0 + (g + 1) * SSD_STATE].astype(BF16)
            cm = xc[r0:r0 + SSD_CHUNK, c0 + g * SSD_STATE:c0 + (g + 1) * SSD_STATE].astype(BF16)
            gm = lax.dot_general(cm, bm, (((1,), (1,)), ((), ())), preferred_element_type=F32)
            gmats.append((bm, cm, gm))
        ys = []
        for p in range(SSD_HEADS // 2):
            bm, cm, gm = gmats[(2 * p) // hpg]
            cols = slice(p * LANES, (p + 1) * LANES)
            xp = xdt[:, cols]
            y = None
            for hh in range(2):
                h = 2 * p + hh
                col = jnp.broadcast_to(acum[:, h:h + 1], (SSD_CHUNK, SSD_CHUNK))
                rowv = acum_t[h:h + 1, :]
                w = jnp.where(causal, gm * jnp.exp(col - rowv), 0.0).astype(BF16)
                xh = jnp.where(lo_half if hh == 0 else ~lo_half, xp, 0.0).astype(BF16)
                t = _dot(w, xh)
                y = t if y is None else y + t
            st = st_scr[p]
            y = y + _dot(cm, st.astype(BF16)) * ea64[:, cols]
            st_scr[p] = st * etot64[:, cols] + lax.dot_general(
                bm, xw[:, cols], (((0,), (0,)), ((), ())), preferred_element_type=F32)
            ys.append(y + dsk_ref[:, cols] * xs[:, cols])
        yc = jnp.concatenate(ys, axis=1)
        zc = z_ref[r0:r0 + SSD_CHUNK, :].astype(F32)
        yc = yc * (zc * jax.nn.sigmoid(zc))
        y_ref[r0:r0 + SSD_CHUNK, :] = _rms(yc, gn_ref[...]).astype(y_ref.dtype)


def _ssd(z3, xbc3, dt3, cw, cb, dtb, alog, dsk, gn, e64, layer, lb):
    b, l, _ = z3.shape
    blk = lambda w: pl.BlockSpec((None, lb, w), lambda i, j: (i, j, 0))
    ls = lambda a: _layer_spec(a.shape, layer, 2)
    return pl.pallas_call(
        _ssd_kernel,
        grid=(b, l // lb),
        in_specs=[blk(SSD_WIDTH), blk(SSD_CONV_DIM), blk(LANES), ls(cw), ls(cb),
                  ls(dtb), ls(alog), ls(dsk), ls(gn),
                  pl.BlockSpec(e64.shape, lambda i, j: (0, 0))],
        out_specs=blk(SSD_WIDTH),
        out_shape=jax.ShapeDtypeStruct((b, l, SSD_WIDTH), BF16),
        scratch_shapes=[pltpu.VMEM((lb + 2 * SUBLANES, SSD_CONV_DIM), F32),
                        pltpu.VMEM((SSD_HEADS // 2, SSD_STATE, LANES), F32)],
        compiler_params=_params("parallel", "arbitrary"),
        name="ssd",
    )(z3, xbc3, dt3, cw, cb, dtb, alog, dsk, gn, e64)


FFN_CHUNK = 256


def _gelu_tanh(v):
    return 0.5 * v * (1.0 + jnp.tanh(0.7978845608028654 * (v + 0.044715 * (v * v * v))))


def _post_kernel(x_ref, ys_ref, u_ref, yb_ref, d5_ref, wglu_ref, bglu_ref, g5_ref,
                 woa_ref, wob_ref, gf_ref, wg_ref, wu_ref, wd_ref, gfin_ref, o_ref, y_scr,
                 *, final):
    bsz, tok, _ = x_ref.shape
    for b in range(bsz):
        for c in range(tok // S5_CHUNK):
            r0 = b * tok + c * S5_CHUNK
            for j in range(D_MODEL // LANES):
                cols = slice(j * LANES, (j + 1) * LANES)
                rows = _s5_rows(c, j, b)
                y_scr[r0:r0 + S5_CHUNK, cols] = ys_ref[rows, :] + d5_ref[:, cols] * u_ref[rows, :]
    g = _gelu_tanh(y_scr[...])
    ya = g * jax.nn.sigmoid(_dot(g.astype(BF16), wglu_ref[...]) + bglu_ref[...])
    ya = _rms(ya, g5_ref[...]).astype(BF16)
    yb = yb_ref[...].reshape(bsz * tok, SSD_WIDTH)
    x = x_ref[...].reshape(bsz * tok, D_MODEL) + _dot(ya, woa_ref[...]) + _dot(yb, wob_ref[...])
    hb = _rms(x, gf_ref[...]).astype(BF16)
    y_scr[...] = x

    def ffn_chunk(k, carry):
        sl = pl.ds(pl.multiple_of(k * FFN_CHUNK, FFN_CHUNK), FFN_CHUNK)
        gate = _dot(hb, wg_ref[:, sl])
        act = (gate * jax.nn.sigmoid(gate)) * _dot(hb, wu_ref[:, sl])
        y_scr[...] += _dot(act.astype(BF16), wd_ref[sl, :])
        return carry

    lax.fori_loop(0, FFN_HIDDEN // FFN_CHUNK, ffn_chunk, 0)
    out = y_scr[...]
    if final:
        out = _rms(out, gfin_ref[...])
    o_ref[...] = out.reshape(o_ref.shape)


def _post(x3, ys, u, yb3, d5, wglu, bglu, g5, wo, gf, wg, wu, wd, gfin, layer, tok, final):
    b, seq, _ = x3.shape
    nat = pl.BlockSpec((b, tok, D_MODEL), lambda i: (0, i, 0))
    s5 = pl.BlockSpec((tok * b * D_MODEL // LANES, LANES), lambda i: (i, 0))
    one = pl.Buffered(1)
    ls = lambda a: pl.BlockSpec((None,) + a.shape[1:], lambda i: (layer, 0, 0), pipeline_mode=one)
    wo_half = lambda k: pl.BlockSpec((None, D_MODEL, D_MODEL), lambda i: (layer, k, 0),
                                     pipeline_mode=one)
    gfin_spec = pl.BlockSpec(gfin.shape, lambda i: (0, 0), pipeline_mode=one)
    return pl.pallas_call(
        functools.partial(_post_kernel, final=final),
        grid=(seq // tok,),
        in_specs=[nat, s5, s5, nat, ls(d5), ls(wglu), ls(bglu), ls(g5), wo_half(0), wo_half(1),
                  ls(gf), ls(wg), ls(wu), ls(wd), gfin_spec],
        out_specs=nat,
        out_shape=jax.ShapeDtypeStruct(x3.shape, F32),
        scratch_shapes=[pltpu.VMEM((b * tok, D_MODEL), F32)],
        compiler_params=_params("parallel"),
        name="post",
    )(x3, ys, u, yb3, d5, wglu, bglu, g5, wo, wo, gf, wg, wu, wd, gfin)


def kernel(x, norm_mix, w_in, s5_lam_re, s5_lam_im, s5_log_step, s5_b_re, s5_b_im, s5_c_re, s5_c_im, s5_d, s5_w_glu, s5_b_glu, s5_norm, ssd_conv_w, ssd_conv_b, ssd_dt_bias, ssd_a_log, ssd_d, ssd_norm, w_out, norm_ffn, w_gate, w_up, w_down, norm_final):
    bsz, seq, _ = x.shape
    depth = w_in.shape[0]
    assert bsz == SUBLANES and seq % S5_CHUNK == 0 and seq % SSD_CHUNK == 0
    nc5 = seq // S5_CHUNK
    tok = 2 * S5_CHUNK
    lb = min(256, seq)
    dg = depth * S5_GROUPS
    rows = lambda a: a.reshape(depth, 1, -1).astype(F32)

    wu, wz, wx, wdt = _cast_win(w_in, 256)
    wglu = _cast_bf16(s5_w_glu, 256)
    wo = _cast_bf16(w_out, 256)
    wg = _cast_bf16(w_gate, 256)
    wup = _cast_bf16(w_up, 256)
    wd = _cast_bf16(w_down, 256)

    dbl = lambda a: jnp.concatenate([a, a], axis=-1)
    mt, bt, ct, apow = _s5_prep(
        s5_log_step.reshape(dg, 1, 1),
        s5_lam_re.reshape(dg, S5_STATE, 1), s5_lam_im.reshape(dg, S5_STATE, 1),
        dbl(s5_lam_re).reshape(dg, 1, 2 * S5_STATE), dbl(s5_lam_im).reshape(dg, 1, 2 * S5_STATE),
        dbl(s5_c_re).reshape(dg, S5_GROUP, 2 * S5_STATE),
        dbl(s5_c_im).reshape(dg, S5_GROUP, 2 * S5_STATE),
        s5_b_re.reshape(dg, S5_STATE, S5_GROUP), s5_b_im.reshape(dg, S5_STATE, S5_GROUP))

    head_of_col = jnp.arange(SSD_WIDTH) // SSD_HEAD_DIM
    e64 = (jnp.arange(LANES)[:, None] == head_of_col[None, :]).astype(BF16)
    pad_h = lambda a: jnp.pad(a.astype(F32), ((0, 0), (0, LANES - SSD_HEADS))).reshape(
        depth, 1, LANES)
    dtb, alog = pad_h(ssd_dt_bias), pad_h(ssd_a_log)
    dsk = rows(jnp.repeat(ssd_d, SSD_HEAD_DIM, axis=-1))
    g_mix, g_s5, g_ssd, g_ffn = rows(norm_mix), rows(s5_norm), rows(ssd_norm), rows(norm_ffn)
    d5, bglu, cb = rows(s5_d), rows(s5_b_glu), rows(ssd_conv_b)
    cw = ssd_conv_w.astype(F32)
    gfin = norm_final.reshape(1, -1).astype(F32)

    s5_shape = (nc5, D_MODEL // LANES, S5_CHUNK, bsz, LANES)
    for i in range(depth):
        u, z, xbc, dt = _in_proj(x, g_mix, wu, wz, wx, wdt, i, tok)
        ys = _s5_core(u.reshape(s5_shape), mt, bt, ct, apow, i).reshape(u.shape)
        yb = _ssd(z, xbc, dt, cw, cb, dtb, alog, dsk, g_ssd, e64, i, lb)
        x = _post(x, ys, u, yb, d5, wglu, bglu, g_s5, wo, g_ffn, wg, wup, wd, gfin,
                  i, tok, i == depth - 1)
    return x
```

```python
import functools

import jax
import jax.numpy as jnp
from jax import lax
from jax.experimental import pallas as pl
from jax.experimental.pallas import tpu as pltpu

F32 = jnp.float32
BF16 = jnp.bfloat16
EPS = 1e-6

D_MODEL = 1024
S5_GROUP = 16
S5_GROUPS = 64
S5_STATE = 64
S5_CHUNK = 32
S5_FLAT = S5_CHUNK * S5_GROUP
SSD_HEAD_DIM = 64
SSD_HEADS = 16
SSD_GROUPS = 2
SSD_STATE = 128
SSD_CONV = 4
SSD_CHUNK = 128
SSD_WIDTH = SSD_HEADS * SSD_HEAD_DIM
SSD_CONV_DIM = SSD_WIDTH + 2 * SSD_GROUPS * SSD_STATE
FFN_HIDDEN = 2816
LANES = 128
SUBLANES = 8
S5_GPB = LANES // S5_GROUP
VMEM_LIMIT = 56 * 1024 * 1024
IN_U = D_MODEL
IN_Z = IN_U + SSD_WIDTH
IN_X = IN_Z + SSD_CONV_DIM


def _params(*sem):
    return pltpu.CompilerParams(dimension_semantics=sem, vmem_limit_bytes=VMEM_LIMIT)


def _rms(x, gain):
    return x * lax.rsqrt(jnp.mean(x * x, axis=-1, keepdims=True) + EPS) * gain


def _dot(a, b):
    return jnp.dot(a, b, preferred_element_type=F32)


def _split3(v):
    hi = v.astype(BF16)
    r1 = v - hi.astype(F32)
    mid = r1.astype(BF16)
    lo = (r1 - mid.astype(F32)).astype(BF16)
    return hi, mid, lo


def _dot_sel_rhs(v, sel):
    hi, mid, lo = _split3(v)
    return _dot(hi, sel) + _dot(mid, sel) + _dot(lo, sel)


def _dot_sel_lhs(sel, v):
    hi, mid, lo = _split3(v)
    return _dot(sel, hi) + _dot(sel, mid) + _dot(sel, lo)


def _dot_f32(a, b):
    a3 = _split3(a)
    b3 = _split3(b)
    acc = None
    for i in range(3):
        for j in range(3 - i):
            t = _dot(a3[i], b3[j])
            acc = t if acc is None else acc + t
    return acc


def _layer_spec(shape, layer, n_grid, col=0):
    zeros = (0,) * (len(shape) - 2)
    if n_grid == 1:
        return pl.BlockSpec((None,) + shape[1:], lambda i: (layer,) + zeros + (col,))
    return pl.BlockSpec((None,) + shape[1:], lambda i, j: (layer,) + zeros + (col,))


def _cast_kernel(w_ref, o_ref):
    o_ref[...] = w_ref[...].astype(o_ref.dtype)


def _cast_bf16(w, rb):
    depth, rows, cols = w.shape
    spec = pl.BlockSpec((1, rb, cols), lambda i, j: (i, j, 0))
    return pl.pallas_call(
        _cast_kernel, grid=(depth, rows // rb), in_specs=[spec], out_specs=spec,
        out_shape=jax.ShapeDtypeStruct(w.shape, BF16),
        compiler_params=_params("parallel", "parallel"), name="cast",
    )(w)


def _cast_win_kernel(w_ref, wu_ref, wz_ref, wx_ref, wdt_ref):
    wu_ref[0] = w_ref[0, :, 0:IN_U].astype(BF16)
    wz_ref[0] = w_ref[0, :, IN_U:IN_Z].astype(BF16)
    wx_ref[0] = w_ref[0, :, IN_Z:IN_X].astype(BF16)
    wdt_ref[0] = jnp.zeros(wdt_ref.shape[1:], BF16)
    wdt_ref[0, :, 0:SSD_HEADS] = w_ref[0, :, IN_X:IN_X + SSD_HEADS].astype(BF16)


def _cast_win(w_in, rb):
    depth, rows, cols = w_in.shape
    spec = lambda c: pl.BlockSpec((1, rb, c), lambda i, j: (i, j, 0))
    shp = lambda c: jax.ShapeDtypeStruct((depth, rows, c), BF16)
    return pl.pallas_call(
        _cast_win_kernel, grid=(depth, rows // rb), in_specs=[spec(cols)],
        out_specs=[spec(D_MODEL), spec(SSD_WIDTH), spec(SSD_CONV_DIM), spec(LANES)],
        out_shape=[shp(D_MODEL), shp(SSD_WIDTH), shp(SSD_CONV_DIM), shp(LANES)],
        compiler_params=_params("parallel", "parallel"), name="cast_win",
    )(w_in)


def _s5_rows(c, j, b):
    return pl.ds(((c * (D_MODEL // LANES) + j) * S5_CHUNK) * SUBLANES + b, S5_CHUNK,
                 stride=SUBLANES)


def _inproj_kernel(x_ref, g_ref, wu_ref, wz_ref, wx_ref, wdt_ref,
                   u_ref, z_ref, xbc_ref, dt_ref):
    bsz, tok, _ = x_ref.shape
    hb = _rms(x_ref[...], g_ref[...]).astype(BF16).reshape(bsz * tok, D_MODEL)
    u = _dot(hb, wu_ref[...])
    for b in range(bsz):
        for c in range(tok // S5_CHUNK):
            r0 = b * tok + c * S5_CHUNK
            for j in range(D_MODEL // LANES):
                u_ref[_s5_rows(c, j, b), :] = u[r0:r0 + S5_CHUNK, j * LANES:(j + 1) * LANES]
    z_ref[...] = _dot(hb, wz_ref[...]).astype(z_ref.dtype).reshape(z_ref.shape)
    xbc_ref[...] = _dot(hb, wx_ref[...]).astype(xbc_ref.dtype).reshape(xbc_ref.shape)
    dt_ref[...] = _dot(hb, wdt_ref[...]).reshape(dt_ref.shape)


def _in_proj(x3, gain, wu, wz, wx, wdt, layer, tok):
    b, seq, _ = x3.shape
    nat = lambda w: pl.BlockSpec((b, tok, w), lambda i: (0, i, 0))
    ls = lambda a: _layer_spec(a.shape, layer, 1)
    return pl.pallas_call(
        _inproj_kernel,
        grid=(seq // tok,),
        in_specs=[nat(D_MODEL), ls(gain), ls(wu), ls(wz), ls(wx), ls(wdt)],
        out_specs=[pl.BlockSpec((tok * b * D_MODEL // LANES, LANES), lambda i: (i, 0)),
                   nat(SSD_WIDTH), nat(SSD_CONV_DIM), nat(LANES)],
        out_shape=[jax.ShapeDtypeStruct((seq * b * D_MODEL // LANES, LANES), F32),
                   jax.ShapeDtypeStruct((b, seq, SSD_WIDTH), BF16),
                   jax.ShapeDtypeStruct((b, seq, SSD_CONV_DIM), BF16),
                   jax.ShapeDtypeStruct((b, seq, LANES), F32)],
        compiler_params=_params("parallel"),
        name="in_proj",
    )(x3, gain, wu, wz, wx, wdt)


def _s5_prep_kernel(ls_ref, lrc_ref, lic_ref, lrr_ref, lir_ref, cre_ref, cim_ref,
                    bre_ref, bim_ref, mt_ref, bt_ref, ct_ref, apow_ref):
    step = jnp.exp(ls_ref[0])
    lrc = lrc_ref[0]
    lic = lic_ref[0]
    p2 = 2 * S5_STATE

    k_lane = lax.broadcasted_iota(jnp.int32, (1, LANES), 1).astype(F32)
    st = step * k_lane
    mag = jnp.exp(lrc * st)
    pre_c = mag * jnp.cos(lic * st)
    pim_c = mag * jnp.sin(lic * st)
    are = pre_c[:, 1:2]
    aim = pim_c[:, 1:2]
    den = lrc * lrc + lic * lic
    nr = are - 1.0
    cfr = (nr * lrc + aim * lic) / den
    cfi = (aim * lrc - nr * lic) / den
    bre = bre_ref[0]
    bim = bim_ref[0]
    bbr = cfr * bre - cfi * bim
    bbi = cfr * bim + cfi * bre

    lane = lax.broadcasted_iota(jnp.int32, (1, S5_FLAT), 1)
    h_of_lane = lax.broadcasted_iota(jnp.int32, (S5_GROUP, S5_FLAT), 1) % S5_GROUP
    e_h = (h_of_lane == lax.broadcasted_iota(jnp.int32, (S5_GROUP, S5_FLAT), 0)).astype(BF16)
    k_row = lax.broadcasted_iota(jnp.int32, (LANES, S5_FLAT), 0)
    j_lane = lax.broadcasted_iota(jnp.int32, (LANES, S5_FLAT), 1) // S5_GROUP
    e_rev = (k_row == (S5_CHUNK - 1) - j_lane).astype(BF16)
    btr = _dot_sel_rhs(bbr, e_h)
    bti = _dot_sel_rhs(bbi, e_h)
    prr = _dot_sel_rhs(pre_c, e_rev)
    pri = _dot_sel_rhs(pim_c, e_rev)
    wre = prr * btr - pri * bti
    wim = prr * bti + pri * btr
    bt_ref[0] = jnp.concatenate([wre, wim, wim, wre], axis=0).astype(bt_ref.dtype)

    lane2 = lax.broadcasted_iota(jnp.int32, (1, p2), 1)
    first = lane2 < S5_STATE
    c1 = cre_ref[0]
    c2 = cim_ref[0]
    lhs = jnp.where(first, c1, c2)
    rev = _dot_f32(lhs, jnp.concatenate([wre, -wim], axis=0))
    for t in range(S5_CHUNK):
        rows = slice(t * S5_GROUP, (t + 1) * S5_GROUP)
        shift = (S5_FLAT - (S5_CHUNK - 1 - t) * S5_GROUP) % S5_FLAT
        blk = rev if shift == 0 else pltpu.roll(rev, shift, axis=1)
        if t < S5_CHUNK - 1:
            blk = jnp.where(lane < (t + 1) * S5_GROUP, blk, 0.0)
        mt_ref[0, rows, :] = blk.astype(mt_ref.dtype)

    lrr = lrr_ref[0]
    lir = lir_ref[0]
    tau = (lax.broadcasted_iota(jnp.int32, (S5_CHUNK, 1), 0) + 1).astype(F32)
    st_r = step * tau
    mag_r = jnp.exp(lrr * st_r)
    pre_r = mag_r * jnp.cos(lir * st_r)
    pim_r = mag_r * jnp.sin(lir * st_r)
    q1 = jnp.where(first, pre_r, pim_r)
    q2 = jnp.where(first, pim_r, pre_r)
    s1 = jnp.where(first, 1.0, -1.0).astype(F32)
    for t in range(S5_CHUNK):
        rows = slice(t * S5_GROUP, (t + 1) * S5_GROUP)
        ct_ref[0, rows, :] = (s1 * (c1 * q1[t:t + 1, :]) - c2 * q2[t:t + 1, :]).astype(ct_ref.dtype)
    atr = pre_r[S5_CHUNK - 1:S5_CHUNK, :]
    a2 = -s1 * pim_r[S5_CHUNK - 1:S5_CHUNK, :]
    apow_ref[0] = jnp.concatenate(
        [atr, a2, -a2, jnp.zeros((SUBLANES - 3, p2), F32)], axis=0)


def _s5_prep(ls, lrc, lic, lrr, lir, cre, cim, bre, bim):
    dg = ls.shape[0]
    spec = lambda a: pl.BlockSpec((1,) + a.shape[1:], lambda i: (i, 0, 0))
    ins = (ls, lrc, lic, lrr, lir, cre, cim, bre, bim)
    shapes = [(dg, S5_FLAT, S5_FLAT), (dg, 4 * S5_STATE, S5_FLAT),
              (dg, S5_FLAT, 2 * S5_STATE), (dg, SUBLANES, 2 * S5_STATE)]
    dtypes = [BF16, BF16, BF16, F32]
    return pl.pallas_call(
        _s5_prep_kernel,
        grid=(dg,),
        in_specs=[spec(a) for a in ins],
        out_specs=[pl.BlockSpec((1,) + s[1:], lambda i: (i, 0, 0)) for s in shapes],
        out_shape=[jax.ShapeDtypeStruct(s, d) for s, d in zip(shapes, dtypes)],
        compiler_params=_params("parallel"),
        name="s5_prep",
    )(*ins)


def _s5_core_kernel(u_hbm, mt_ref, bt_ref, ct_ref, ap_ref, y_hbm,
                    io_buf, zt_scr, e_scr, sp_scr, yt_scr, sem_in, sem_out, *, batch):
    ng = mt_ref.shape[0]
    nchunk = u_hbm.shape[0]
    cols = nchunk * batch
    half = S5_FLAT // 2
    j = pl.program_id(0)
    last = pl.num_programs(0) - 1
    slot = j % 2
    other = 1 - slot

    def in_copy(step, s, t):
        return pltpu.make_async_copy(u_hbm.at[:, step, t], io_buf.at[s, t], sem_in.at[s, t])

    def out_copy(step, s, t):
        return pltpu.make_async_copy(io_buf.at[s, t], y_hbm.at[:, step, t], sem_out.at[s, t])

    @pl.when(j == 0)
    def _():
        for t in range(S5_CHUNK):
            in_copy(j, slot, t).start()

    for t in range(S5_CHUNK):
        in_copy(j, slot, t).wait()
    for t in range(S5_CHUNK):
        blk_t = io_buf[slot, t].reshape(cols, LANES).T.astype(BF16)
        for g in range(ng):
            zt_scr[g, t * S5_GROUP:(t + 1) * S5_GROUP, :] = blk_t[g * S5_GROUP:(g + 1) * S5_GROUP, :]

    @pl.when(j > 0)
    def _():
        for t in range(S5_CHUNK):
            out_copy(j - 1, other, t).wait()

    @pl.when(j < last)
    def _():
        for t in range(S5_CHUNK):
            in_copy(j + 1, other, t).start()

    for g in range(ng):
        e_scr[g] = _dot(bt_ref[g], zt_scr[g]).T

    def body(c, carry):
        r = pl.multiple_of(c * batch, batch)
        out = []
        for g in range(ng):
            s, sw = carry[g]
            sp_scr[g, pl.ds(r, batch), :] = s
            e = e_scr[g, pl.ds(r, batch), :]
            a1 = ap_ref[g, 0:1, :]
            a2 = ap_ref[g, 1:2, :]
            a2w = ap_ref[g, 2:3, :]
            out.append((a1 * s + a2 * sw + e[:, :LANES], a1 * sw + a2w * s + e[:, LANES:]))
        return tuple(out)

    zero = jnp.zeros((batch, 2 * S5_STATE), F32)
    lax.fori_loop(0, nchunk, body, tuple((zero, zero) for _ in range(ng)))

    for g in range(ng):
        y_off = lax.dot_general(ct_ref[g], sp_scr[g].astype(BF16), (((1,), (1,)), ((), ())),
                                preferred_element_type=F32)
        top = _dot(mt_ref[g, :half, :half], zt_scr[g, :half, :])
        bot = _dot(mt_ref[g, half:, :], zt_scr[g])
        yt_scr[g, :half, :] = (top + y_off[:half]).astype(yt_scr.dtype)
        yt_scr[g, half:, :] = (bot + y_off[half:]).astype(yt_scr.dtype)

    for t in range(S5_CHUNK):
        rows = slice(t * S5_GROUP, (t + 1) * S5_GROUP)
        blk = jnp.concatenate([yt_scr[g, rows, :] for g in range(ng)], axis=0)
        io_buf[slot, t] = blk.astype(F32).T.reshape(nchunk, batch, LANES)
        out_copy(j, slot, t).start()

    @pl.when(j == last)
    def _():
        for t in range(S5_CHUNK):
            out_copy(j, slot, t).wait()


def _s5_core(u5, mt, bt, ct, apow, layer):
    nchunk, nblk, _, batch, _ = u5.shape
    cols = nchunk * batch
    ng = S5_GPB
    op = lambda a: pl.BlockSpec((ng,) + a.shape[1:], lambda i: (layer * nblk + i, 0, 0))
    hbm = pl.BlockSpec(memory_space=pl.ANY)
    return pl.pallas_call(
        functools.partial(_s5_core_kernel, batch=batch),
        grid=(nblk,),
        in_specs=[hbm, op(mt), op(bt), op(ct), op(apow)],
        out_specs=hbm,
        out_shape=jax.ShapeDtypeStruct(u5.shape, F32),
        scratch_shapes=[pltpu.VMEM((2, S5_CHUNK, nchunk, batch, LANES), F32),
                        pltpu.VMEM((ng, S5_FLAT, cols), BF16),
                        pltpu.VMEM((ng, cols, 4 * S5_STATE), F32),
                        pltpu.VMEM((ng, cols, 2 * S5_STATE), F32),
                        pltpu.VMEM((ng, S5_FLAT, cols), BF16),
                        pltpu.SemaphoreType.DMA((2, S5_CHUNK)),
                        pltpu.SemaphoreType.DMA((2, S5_CHUNK))],
        compiler_params=_params("arbitrary"),
        name="s5_core",
    )(u5, mt, bt, ct, apow)


def _ssd_kernel(z_ref, xbc_ref, dt_ref, cw_ref, cb_ref, dtb_ref, alog_ref, dsk_ref,
                gn_ref, e64_ref, y_ref, tail_scr, xc_scr, st_scr):
    lb = z_ref.shape[0]
    j = pl.program_id(1)

    @pl.when(j == 0)
    def _():
        tail_scr[...] = jnp.zeros_like(tail_scr)
        st_scr[...] = jnp.zeros_like(st_scr)

    xb = xbc_ref[...]
    x0 = xb.astype(F32)
    rr = lax.broadcasted_iota(jnp.int32, (lb, lb), 0)
    cc = lax.broadcasted_iota(jnp.int32, (lb, lb), 1)
    row8 = lax.broadcasted_iota(jnp.int32, (SUBLANES, 1), 0)
    tail = tail_scr[...]
    acc = cb_ref[...] + cw_ref[SSD_CONV - 1:SSD_CONV, :] * x0
    head = jnp.zeros((SUBLANES, SSD_CONV_DIM), F32)
    for k in range(1, SSD_CONV):
        wk = cw_ref[SSD_CONV - 1 - k:SSD_CONV - k, :]
        acc = acc + wk * _dot((rr - cc == k).astype(BF16), xb)
        head = head + wk * jnp.where(row8 < k, pltpu.roll(tail, k, axis=0), 0.0)
    tail_scr[...] = x0[lb - SUBLANES:, :]
    xc_scr[...] = acc * jax.nn.sigmoid(acc)
    acc0 = acc[0:SUBLANES, :] + head
    xc_scr[0:SUBLANES, :] = acc0 * jax.nn.sigmoid(acc0)

    dt_t = (dt_ref[...] + dtb_ref[...]).T[0:SSD_HEADS, :]
    dtp_t = jnp.maximum(dt_t, 0.0) + jnp.log(1.0 + jnp.exp(-jnp.abs(dt_t)))
    dtp_all = jnp.concatenate(
        [dtp_t, jnp.zeros((LANES - SSD_HEADS, lb), F32)], axis=0).T

    ri = lax.broadcasted_iota(jnp.int32, (SSD_CHUNK, SSD_CHUNK), 0)
    ci = lax.broadcasted_iota(jnp.int32, (SSD_CHUNK, SSD_CHUNK), 1)
    causal = ri >= ci
    ltri = causal.astype(BF16)
    lo_half = ci < SSD_HEAD_DIM
    a_neg = -jnp.exp(alog_ref[...])
    e64 = e64_ref[...]
    b0 = SSD_WIDTH
    c0 = SSD_WIDTH + SSD_GROUPS * SSD_STATE
    hpg = SSD_HEADS // SSD_GROUPS

    for c in range(lb // SSD_CHUNK):
        r0 = c * SSD_CHUNK
        xs = xc_scr[r0:r0 + SSD_CHUNK, 0:SSD_WIDTH]
        dtp = dtp_all[r0:r0 + SSD_CHUNK, :]
        dta = dtp * a_neg
        acum = _dot_sel_lhs(ltri, dta)
        acum_t = acum.T
        atot = acum[SSD_CHUNK - 1:SSD_CHUNK, :]
        fac = jnp.concatenate([dtp, dtp * jnp.exp(atot - acum), jnp.exp(acum)], axis=0)
        fac64 = _dot(fac.astype(BF16), e64)
        xdt = xs * fac64[0:SSD_CHUNK]
        xw = (xs * fac64[SSD_CHUNK:2 * SSD_CHUNK]).astype(BF16)
        ea64 = fac64[2 * SSD_CHUNK:]
        tail = SSD_CHUNK - SUBLANES
        etot64 = jnp.exp(_dot_sel_rhs(acum[tail:, :], e64)[SUBLANES - 1:SUBLANES, :])
        gmats = []
        for g in range(SSD_GROUPS):
            bm = xc_scr[r0:r0 + SSD_CHUNK, b0 + g * SSD_STATE:b0 + (g + 1) * SSD_STATE]
            cm = xc_scr[r0:r0 + SSD_CHUNK, c0 + g * SSD_STATE:c0 + (g + 1) * SSD_STATE].astype(BF16)
            bm_t = bm.T.astype(BF16)
            gmats.append((bm_t, cm, _dot(cm, bm_t)))
        ys = []
        for p in range(SSD_HEADS // 2):
            bm_t, cm, gm = gmats[(2 * p) // hpg]
            cols = slice(p * LANES, (p + 1) * LANES)
            xp = xdt[:, cols]
            y = None
            for hh in range(2):
                h = 2 * p + hh
                col = jnp.broadcast_to(acum[:, h:h + 1], (SSD_CHUNK, SSD_CHUNK))
                rowv = acum_t[h:h + 1, :]
                w = jnp.where(causal, gm * jnp.exp(col - rowv), 0.0).astype(BF16)
                xh = jnp.where(lo_half if hh == 0 else ~lo_half, xp, 0.0).astype(BF16)
                t = _dot(w, xh)
                y = t if y is None else y + t
            st = st_scr[p]
            y = y + _dot(cm, st.astype(BF16)) * ea64[:, cols]
            st_scr[p] = st * etot64[:, cols] + _dot(bm_t, xw[:, cols])
            ys.append(y + dsk_ref[:, cols] * xs[:, cols])
        yc = jnp.concatenate(ys, axis=1)
        zc = z_ref[r0:r0 + SSD_CHUNK, :].astype(F32)
        yc = yc * (zc * jax.nn.sigmoid(zc))
        y_ref[r0:r0 + SSD_CHUNK, :] = _rms(yc, gn_ref[...]).astype(y_ref.dtype)


def _ssd(z3, xbc3, dt3, cw, cb, dtb, alog, dsk, gn, e64, layer, lb):
    b, l, _ = z3.shape
    blk = lambda w: pl.BlockSpec((None, lb, w), lambda i, j: (i, j, 0))
    ls = lambda a: _layer_spec(a.shape, layer, 2)
    return pl.pallas_call(
        _ssd_kernel,
        grid=(b, l // lb),
        in_specs=[blk(SSD_WIDTH), blk(SSD_CONV_DIM), blk(LANES), ls(cw), ls(cb),
                  ls(dtb), ls(alog), ls(dsk), ls(gn),
                  pl.BlockSpec(e64.shape, lambda i, j: (0, 0))],
        out_specs=blk(SSD_WIDTH),
        out_shape=jax.ShapeDtypeStruct((b, l, SSD_WIDTH), BF16),
        scratch_shapes=[pltpu.VMEM((SUBLANES, SSD_CONV_DIM), F32),
                        pltpu.VMEM((lb, SSD_CONV_DIM), F32),
                        pltpu.VMEM((SSD_HEADS // 2, SSD_STATE, LANES), F32)],
        compiler_params=_params("parallel", "arbitrary"),
        name="ssd",
    )(z3, xbc3, dt3, cw, cb, dtb, alog, dsk, gn, e64)


FFN_CHUNK = 256


def _gelu_tanh(v):
    return 0.5 * v * (1.0 + jnp.tanh(0.7978845608028654 * (v + 0.044715 * (v * v * v))))


def _post_kernel(x_ref, ys_ref, u_ref, yb_ref, d5_ref, wglu_ref, bglu_ref, g5_ref,
                 woa_ref, wob_ref, gf_ref, wg_ref, wu_ref, wd_ref, gfin_ref, o_ref, y_scr,
                 *, final):
    bsz, tok, _ = x_ref.shape
    for b in range(bsz):
        for c in range(tok // S5_CHUNK):
            r0 = b * tok + c * S5_CHUNK
            for j in range(D_MODEL // LANES):
                cols = slice(j * LANES, (j + 1) * LANES)
                rows = _s5_rows(c, j, b)
                y_scr[r0:r0 + S5_CHUNK, cols] = ys_ref[rows, :] + d5_ref[:, cols] * u_ref[rows, :]
    g = _gelu_tanh(y_scr[...])
    ya = g * jax.nn.sigmoid(_dot(g.astype(BF16), wglu_ref[...]) + bglu_ref[...])
    ya = _rms(ya, g5_ref[...]).astype(BF16)
    yb = yb_ref[...].reshape(bsz * tok, SSD_WIDTH)
    x = x_ref[...].reshape(bsz * tok, D_MODEL) + _dot(ya, woa_ref[...]) + _dot(yb, wob_ref[...])
    hb = _rms(x, gf_ref[...]).astype(BF16)
    y_scr[...] = x

    def ffn_chunk(k, carry):
        sl = pl.ds(pl.multiple_of(k * FFN_CHUNK, FFN_CHUNK), FFN_CHUNK)
        gate = _dot(hb, wg_ref[:, sl])
        act = (gate * jax.nn.sigmoid(gate)) * _dot(hb, wu_ref[:, sl])
        y_scr[...] += _dot(act.astype(BF16), wd_ref[sl, :])
        return carry

    lax.fori_loop(0, FFN_HIDDEN // FFN_CHUNK, ffn_chunk, 0)
    out = y_scr[...]
    if final:
        out = _rms(out, gfin_ref[...])
    o_ref[...] = out.reshape(o_ref.shape)


def _post(x3, ys, u, yb3, d5, wglu, bglu, g5, wo, gf, wg, wu, wd, gfin, layer, tok, final):
    b, seq, _ = x3.shape
    nat = pl.BlockSpec((b, tok, D_MODEL), lambda i: (0, i, 0))
    s5 = pl.BlockSpec((tok * b * D_MODEL // LANES, LANES), lambda i: (i, 0))
    one = pl.Buffered(1)
    ls = lambda a: pl.BlockSpec((None,) + a.shape[1:], lambda i: (layer, 0, 0), pipeline_mode=one)
    wo_half = lambda k: pl.BlockSpec((None, D_MODEL, D_MODEL), lambda i: (layer, k, 0),
                                     pipeline_mode=one)
    gfin_spec = pl.BlockSpec(gfin.shape, lambda i: (0, 0), pipeline_mode=one)
    return pl.pallas_call(
        functools.partial(_post_kernel, final=final),
        grid=(seq // tok,),
        in_specs=[nat, s5, s5, nat, ls(d5), ls(wglu), ls(bglu), ls(g5), wo_half(0), wo_half(1),
                  ls(gf), ls(wg), ls(wu), ls(wd), gfin_spec],
        out_specs=nat,
        out_shape=jax.ShapeDtypeStruct(x3.shape, F32),
        scratch_shapes=[pltpu.VMEM((b * tok, D_MODEL), F32)],
        compiler_params=_params("parallel"),
        name="post",
    )(x3, ys, u, yb3, d5, wglu, bglu, g5, wo, wo, gf, wg, wu, wd, gfin)


def kernel(x, norm_mix, w_in, s5_lam_re, s5_lam_im, s5_log_step, s5_b_re, s5_b_im, s5_c_re, s5_c_im, s5_d, s5_w_glu, s5_b_glu, s5_norm, ssd_conv_w, ssd_conv_b, ssd_dt_bias, ssd_a_log, ssd_d, ssd_norm, w_out, norm_ffn, w_gate, w_up, w_down, norm_final):
    bsz, seq, _ = x.shape
    depth = w_in.shape[0]
    assert bsz == SUBLANES and seq % S5_CHUNK == 0 and seq % SSD_CHUNK == 0
    nc5 = seq // S5_CHUNK
    tok = 2 * S5_CHUNK
    lb = min(256, seq)
    dg = depth * S5_GROUPS
    rows = lambda a: a.reshape(depth, 1, -1).astype(F32)

    wu, wz, wx, wdt = _cast_win(w_in, 256)
    wglu = _cast_bf16(s5_w_glu, 256)
    wo = _cast_bf16(w_out, 256)
    wg = _cast_bf16(w_gate, 256)
    wup = _cast_bf16(w_up, 256)
    wd = _cast_bf16(w_down, 256)

    dbl = lambda a: jnp.concatenate([a, a], axis=-1)
    mt, bt, ct, apow = _s5_prep(
        s5_log_step.reshape(dg, 1, 1),
        s5_lam_re.reshape(dg, S5_STATE, 1), s5_lam_im.reshape(dg, S5_STATE, 1),
        dbl(s5_lam_re).reshape(dg, 1, 2 * S5_STATE), dbl(s5_lam_im).reshape(dg, 1, 2 * S5_STATE),
        dbl(s5_c_re).reshape(dg, S5_GROUP, 2 * S5_STATE),
        dbl(s5_c_im).reshape(dg, S5_GROUP, 2 * S5_STATE),
        s5_b_re.reshape(dg, S5_STATE, S5_GROUP), s5_b_im.reshape(dg, S5_STATE, S5_GROUP))

    head_of_col = jnp.arange(SSD_WIDTH) // SSD_HEAD_DIM
    e64 = (jnp.arange(LANES)[:, None] == head_of_col[None, :]).astype(BF16)
    pad_h = lambda a: jnp.pad(a.astype(F32), ((0, 0), (0, LANES - SSD_HEADS))).reshape(
        depth, 1, LANES)
    dtb, alog = pad_h(ssd_dt_bias), pad_h(ssd_a_log)
    dsk = rows(jnp.repeat(ssd_d, SSD_HEAD_DIM, axis=-1))
    g_mix, g_s5, g_ssd, g_ffn = rows(norm_mix), rows(s5_norm), rows(ssd_norm), rows(norm_ffn)
    d5, bglu, cb = rows(s5_d), rows(s5_b_glu), rows(ssd_conv_b)
    cw = ssd_conv_w.astype(F32)
    gfin = norm_final.reshape(1, -1).astype(F32)

    s5_shape = (nc5, D_MODEL // LANES, S5_CHUNK, bsz, LANES)
    for i in range(depth):
        u, z, xbc, dt = _in_proj(x, g_mix, wu, wz, wx, wdt, i, tok)
        ys = _s5_core(u.reshape(s5_shape), mt, bt, ct, apow, i).reshape(u.shape)
        yb = _ssd(z, xbc, dt, cw, cb, dtb, alog, dsk, g_ssd, e64, i, lb)
        x = _post(x, ys, u, yb, d5, wglu, bglu, g_s5, wo, g_ffn, wg, wup, wd, gfin,
                  i, tok, i == depth - 1)
    return x
```

```python
import functools

import jax
import jax.numpy as jnp
from jax import lax
from jax.experimental import pallas as pl
from jax.experimental.pallas import tpu as pltpu

F32 = jnp.float32
BF16 = jnp.bfloat16
EPS = 1e-6

D_MODEL = 1024
S5_GROUP = 16
S5_GROUPS = 64
S5_STATE = 64
S5_CHUNK = 32
S5_FLAT = S5_CHUNK * S5_GROUP
SSD_HEAD_DIM = 64
SSD_HEADS = 16
SSD_GROUPS = 2
SSD_STATE = 128
SSD_CONV = 4
SSD_CHUNK = 128
SSD_WIDTH = SSD_HEADS * SSD_HEAD_DIM
SSD_CONV_DIM = SSD_WIDTH + 2 * SSD_GROUPS * SSD_STATE
FFN_HIDDEN = 2816
LANES = 128
SUBLANES = 8
MXU_WIDTH = 256
S5_GPB = LANES // S5_GROUP
VMEM_LIMIT = 56 * 1024 * 1024
IN_U = D_MODEL
IN_Z = IN_U + SSD_WIDTH
IN_X = IN_Z + SSD_CONV_DIM


def _params(*sem):
    return pltpu.CompilerParams(dimension_semantics=sem, vmem_limit_bytes=VMEM_LIMIT)


def _rms(x, gain):
    return x * lax.rsqrt(jnp.mean(x * x, axis=-1, keepdims=True) + EPS) * gain


def _dot(a, b):
    return jnp.dot(a, b, preferred_element_type=F32)


def _split3(v):
    hi = v.astype(BF16)
    r1 = v - hi.astype(F32)
    mid = r1.astype(BF16)
    lo = (r1 - mid.astype(F32)).astype(BF16)
    return hi, mid, lo


def _dot_sel_rhs(v, sel):
    hi, mid, lo = _split3(v)
    return _dot(hi, sel) + _dot(mid, sel) + _dot(lo, sel)


def _dot_sel_lhs(sel, v):
    hi, mid, lo = _split3(v)
    return _dot(sel, hi) + _dot(sel, mid) + _dot(sel, lo)


def _dot_f32(a, b):
    a3 = _split3(a)
    b3 = _split3(b)
    acc = None
    for i in range(3):
        for j in range(3 - i):
            t = _dot(a3[i], b3[j])
            acc = t if acc is None else acc + t
    return acc


def _layer_spec(shape, layer, n_grid, col=0):
    zeros = (0,) * (len(shape) - 2)
    if n_grid == 1:
        return pl.BlockSpec((None,) + shape[1:], lambda i: (layer,) + zeros + (col,))
    return pl.BlockSpec((None,) + shape[1:], lambda i, j: (layer,) + zeros + (col,))


def _cast_kernel(w_ref, o_ref):
    o_ref[...] = w_ref[...].astype(o_ref.dtype)


def _cast_bf16(w, rb):
    depth, rows, cols = w.shape
    spec = pl.BlockSpec((1, rb, cols), lambda i, j: (i, j, 0))
    return pl.pallas_call(
        _cast_kernel, grid=(depth, rows // rb), in_specs=[spec], out_specs=spec,
        out_shape=jax.ShapeDtypeStruct(w.shape, BF16),
        compiler_params=_params("parallel", "parallel"), name="cast",
    )(w)


def _cast_win_kernel(w_ref, o_ref, *, n_out):
    rb = w_ref.shape[1]
    valid = n_out - pl.program_id(1) * rb
    row = lax.broadcasted_iota(jnp.int32, (rb, 1), 0)
    o_ref[0] = jnp.where(row < valid, w_ref[0], 0.0).T.astype(o_ref.dtype)


def _cast_win(w_in_t, rb):
    depth, n_out, d = w_in_t.shape
    nblk = pl.cdiv(n_out, rb)
    return pl.pallas_call(
        functools.partial(_cast_win_kernel, n_out=n_out), grid=(depth, nblk),
        in_specs=[pl.BlockSpec((1, rb, d), lambda i, j: (i, j, 0))],
        out_specs=pl.BlockSpec((1, d, rb), lambda i, j: (i, 0, j)),
        out_shape=jax.ShapeDtypeStruct((depth, d, nblk * rb), BF16),
        compiler_params=_params("parallel", "parallel"), name="cast_win",
    )(w_in_t)


def _s5_rows(c, j, b):
    return pl.ds(((c * (D_MODEL // LANES) + j) * S5_CHUNK) * SUBLANES + b, S5_CHUNK,
                 stride=SUBLANES)


def _inproj_kernel(x_ref, g_ref, wu_ref, wz_ref, wx0_ref, wx1_ref, wx2_ref, wdt_ref,
                   u_ref, z_ref, xbc_ref, dt_ref):
    bsz, tok, _ = x_ref.shape
    hb = _rms(x_ref[...], g_ref[...]).astype(BF16).reshape(bsz * tok, D_MODEL)
    u = _dot(hb, wu_ref[...])
    for b in range(bsz):
        for c in range(tok // S5_CHUNK):
            r0 = b * tok + c * S5_CHUNK
            for j in range(D_MODEL // LANES):
                u_ref[_s5_rows(c, j, b), :] = u[r0:r0 + S5_CHUNK, j * LANES:(j + 1) * LANES]
    z_ref[...] = _dot(hb, wz_ref[...]).astype(z_ref.dtype).reshape(z_ref.shape)
    xbc = jnp.concatenate([_dot(hb, w[...]).astype(xbc_ref.dtype)
                           for w in (wx0_ref, wx1_ref, wx2_ref)], axis=1)
    xbc_ref[...] = xbc.reshape(xbc_ref.shape)
    dt_ref[...] = _dot(hb, wdt_ref[...]).reshape(dt_ref.shape)


def _in_proj(x3, gain, w_all, layer, tok):
    b, seq, _ = x3.shape
    nat = lambda w: pl.BlockSpec((b, tok, w), lambda i: (0, i, 0))
    win = lambda width, start: pl.BlockSpec((None, D_MODEL, width),
                                            lambda i: (layer, 0, start // width))
    xw = SSD_CONV_DIM // 3
    return pl.pallas_call(
        _inproj_kernel,
        grid=(seq // tok,),
        in_specs=[nat(D_MODEL), _layer_spec(gain.shape, layer, 1), win(D_MODEL, 0),
                  win(SSD_WIDTH, IN_U), win(xw, IN_Z), win(xw, IN_Z + xw), win(xw, IN_Z + 2 * xw),
                  win(LANES, IN_X)],
        out_specs=[pl.BlockSpec((tok * b * D_MODEL // LANES, LANES), lambda i: (i, 0)),
                   nat(SSD_WIDTH), nat(SSD_CONV_DIM), nat(LANES)],
        out_shape=[jax.ShapeDtypeStruct((seq * b * D_MODEL // LANES, LANES), F32),
                   jax.ShapeDtypeStruct((b, seq, SSD_WIDTH), BF16),
                   jax.ShapeDtypeStruct((b, seq, SSD_CONV_DIM), BF16),
                   jax.ShapeDtypeStruct((b, seq, LANES), F32)],
        compiler_params=_params("parallel"),
        name="in_proj",
    )(x3, gain, *([w_all] * 6))


def _s5_prep_group(q, ls_ref, lrc_ref, lic_ref, lrr_ref, lir_ref, cre_ref, cim_ref,
                   bre_ref, bim_ref, mt_ref, bt_ref, ct_ref, apow_ref):
    step = jnp.exp(ls_ref[q])
    lrc = lrc_ref[q]
    lic = lic_ref[q]
    p2 = 2 * S5_STATE

    k_lane = lax.broadcasted_iota(jnp.int32, (1, LANES), 1).astype(F32)
    st = step * k_lane
    mag = jnp.exp(lrc * st)
    pre_c = mag * jnp.cos(lic * st)
    pim_c = mag * jnp.sin(lic * st)
    are = pre_c[:, 1:2]
    aim = pim_c[:, 1:2]
    den = lrc * lrc + lic * lic
    nr = are - 1.0
    cfr = (nr * lrc + aim * lic) / den
    cfi = (aim * lrc - nr * lic) / den
    bre = bre_ref[q]
    bim = bim_ref[q]
    bbr = cfr * bre - cfi * bim
    bbi = cfr * bim + cfi * bre

    lane = lax.broadcasted_iota(jnp.int32, (1, S5_FLAT), 1)
    h_of_lane = lax.broadcasted_iota(jnp.int32, (S5_GROUP, S5_FLAT), 1) % S5_GROUP
    e_h = (h_of_lane == lax.broadcasted_iota(jnp.int32, (S5_GROUP, S5_FLAT), 0)).astype(BF16)
    k_row = lax.broadcasted_iota(jnp.int32, (LANES, S5_FLAT), 0)
    j_lane = lax.broadcasted_iota(jnp.int32, (LANES, S5_FLAT), 1) // S5_GROUP
    e_rev = (k_row == (S5_CHUNK - 1) - j_lane).astype(BF16)
    btr = _dot_sel_rhs(bbr, e_h)
    bti = _dot_sel_rhs(bbi, e_h)
    prr = _dot_sel_rhs(pre_c, e_rev)
    pri = _dot_sel_rhs(pim_c, e_rev)
    wre = prr * btr - pri * bti
    wim = prr * bti + pri * btr
    bt_ref[q] = jnp.concatenate([wre, wim, wim, wre], axis=0).astype(bt_ref.dtype)

    lane2 = lax.broadcasted_iota(jnp.int32, (1, p2), 1)
    first = lane2 < S5_STATE
    c1 = cre_ref[q]
    c2 = cim_ref[q]
    lhs = jnp.where(first, c1, c2)
    rev = _dot_f32(lhs, jnp.concatenate([wre, -wim], axis=0))
    for t in range(S5_CHUNK):
        rows = slice(t * S5_GROUP, (t + 1) * S5_GROUP)
        shift = (S5_FLAT - (S5_CHUNK - 1 - t) * S5_GROUP) % S5_FLAT
        blk = rev if shift == 0 else pltpu.roll(rev, shift, axis=1)
        if t < S5_CHUNK - 1:
            blk = jnp.where(lane < (t + 1) * S5_GROUP, blk, 0.0)
        mt_ref[q, rows, :] = blk.astype(mt_ref.dtype)

    lrr = lrr_ref[q]
    lir = lir_ref[q]
    tau = (lax.broadcasted_iota(jnp.int32, (S5_CHUNK, 1), 0) + 1).astype(F32)
    st_r = step * tau
    mag_r = jnp.exp(lrr * st_r)
    pre_r = mag_r * jnp.cos(lir * st_r)
    pim_r = mag_r * jnp.sin(lir * st_r)
    q1 = jnp.where(first, pre_r, pim_r)
    q2 = jnp.where(first, pim_r, pre_r)
    s1 = jnp.where(first, 1.0, -1.0).astype(F32)
    for t in range(S5_CHUNK):
        rows = slice(t * S5_GROUP, (t + 1) * S5_GROUP)
        ct_ref[q, rows, :] = (s1 * (c1 * q1[t:t + 1, :]) - c2 * q2[t:t + 1, :]).astype(ct_ref.dtype)
    atr = pre_r[S5_CHUNK - 1:S5_CHUNK, :]
    a2 = -s1 * pim_r[S5_CHUNK - 1:S5_CHUNK, :]
    apow_ref[q] = jnp.concatenate(
        [atr, a2, -a2, jnp.zeros((SUBLANES - 3, p2), F32)], axis=0)


def _s5_prep_kernel(*refs):
    for q in range(refs[0].shape[0]):
        _s5_prep_group(q, *refs)


def _s5_prep(ls, lrc, lic, lrr, lir, cre, cim, bre, bim):
    dg = ls.shape[0]
    gps = 4
    spec = lambda a: pl.BlockSpec((gps,) + a.shape[1:], lambda i: (i, 0, 0))
    ins = (ls, lrc, lic, lrr, lir, cre, cim, bre, bim)
    shapes = [(dg, S5_FLAT, S5_FLAT), (dg, 4 * S5_STATE, S5_FLAT),
              (dg, S5_FLAT, 2 * S5_STATE), (dg, SUBLANES, 2 * S5_STATE)]
    dtypes = [BF16, BF16, BF16, F32]
    return pl.pallas_call(
        _s5_prep_kernel,
        grid=(dg // gps,),
        in_specs=[spec(a) for a in ins],
        out_specs=[pl.BlockSpec((gps,) + s[1:], lambda i: (i, 0, 0)) for s in shapes],
        out_shape=[jax.ShapeDtypeStruct(s, d) for s, d in zip(shapes, dtypes)],
        compiler_params=_params("parallel"),
        name="s5_prep",
    )(*ins)


def _s5_core_kernel(u_hbm, mt_ref, bt_ref, ct_ref, ap_ref, y_hbm,
                    io_buf, zt_scr, e_scr, sp_scr, yt_scr, sem_in, sem_out, *, batch):
    ng = mt_ref.shape[0]
    nchunk = u_hbm.shape[0]
    cols = nchunk * batch
    half = S5_FLAT // 2
    j = pl.program_id(0)
    last = pl.num_programs(0) - 1
    slot = j % 2
    other = 1 - slot

    def in_copy(step, s, t):
        return pltpu.make_async_copy(u_hbm.at[:, step, t], io_buf.at[s, t], sem_in.at[s, t])

    def out_copy(step, s, t):
        return pltpu.make_async_copy(io_buf.at[s, t], y_hbm.at[:, step, t], sem_out.at[s, t])

    @pl.when(j == 0)
    def _():
        for t in range(S5_CHUNK):
            in_copy(j, slot, t).start()

    for t in range(S5_CHUNK):
        in_copy(j, slot, t).wait()
    for t in range(S5_CHUNK):
        blk_t = io_buf[slot, t].reshape(cols, LANES).T.astype(BF16)
        for g in range(ng):
            zt_scr[g, t * S5_GROUP:(t + 1) * S5_GROUP, :] = blk_t[g * S5_GROUP:(g + 1) * S5_GROUP, :]

    @pl.when(j > 0)
    def _():
        for t in range(S5_CHUNK):
            out_copy(j - 1, other, t).wait()

    @pl.when(j < last)
    def _():
        for t in range(S5_CHUNK):
            in_copy(j + 1, other, t).start()

    for g in range(ng):
        e_scr[g] = _dot(bt_ref[g], zt_scr[g]).T

    def body(c, carry):
        r = pl.multiple_of(c * batch, batch)
        out = []
        for g in range(ng):
            s, sw = carry[g]
            sp_scr[g, pl.ds(r, batch), :] = s
            e = e_scr[g, pl.ds(r, batch), :]
            a1 = ap_ref[g, 0:1, :]
            a2 = ap_ref[g, 1:2, :]
            a2w = ap_ref[g, 2:3, :]
            out.append((a1 * s + a2 * sw + e[:, :LANES], a1 * sw + a2w * s + e[:, LANES:]))
        return tuple(out)

    zero = jnp.zeros((batch, 2 * S5_STATE), F32)
    lax.fori_loop(0, nchunk, body, tuple((zero, zero) for _ in range(ng)))

    for g in range(ng):
        y_off = lax.dot_general(ct_ref[g], sp_scr[g].astype(BF16), (((1,), (1,)), ((), ())),
                                preferred_element_type=F32)
        top = _dot(mt_ref[g, :half, :half], zt_scr[g, :half, :])
        bot = _dot(mt_ref[g, half:, :], zt_scr[g])
        yt_scr[g, :half, :] = (top + y_off[:half]).astype(yt_scr.dtype)
        yt_scr[g, half:, :] = (bot + y_off[half:]).astype(yt_scr.dtype)

    for t in range(S5_CHUNK):
        rows = slice(t * S5_GROUP, (t + 1) * S5_GROUP)
        blk = jnp.concatenate([yt_scr[g, rows, :] for g in range(ng)], axis=0)
        io_buf[slot, t] = blk.astype(F32).T.reshape(nchunk, batch, LANES)
        out_copy(j, slot, t).start()

    @pl.when(j == last)
    def _():
        for t in range(S5_CHUNK):
            out_copy(j, slot, t).wait()


def _s5_core(u5, mt, bt, ct, apow, layer):
    nchunk, nblk, _, batch, _ = u5.shape
    cols = nchunk * batch
    ng = S5_GPB
    op = lambda a: pl.BlockSpec((ng,) + a.shape[1:], lambda i: (layer * nblk + i, 0, 0))
    hbm = pl.BlockSpec(memory_space=pl.ANY)
    return pl.pallas_call(
        functools.partial(_s5_core_kernel, batch=batch),
        grid=(nblk,),
        in_specs=[hbm, op(mt), op(bt), op(ct), op(apow)],
        out_specs=hbm,
        out_shape=jax.ShapeDtypeStruct(u5.shape, F32),
        scratch_shapes=[pltpu.VMEM((2, S5_CHUNK, nchunk, batch, LANES), F32),
                        pltpu.VMEM((ng, S5_FLAT, cols), BF16),
                        pltpu.VMEM((ng, cols, 4 * S5_STATE), F32),
                        pltpu.VMEM((ng, cols, 2 * S5_STATE), F32),
                        pltpu.VMEM((ng, S5_FLAT, cols), BF16),
                        pltpu.SemaphoreType.DMA((2, S5_CHUNK)),
                        pltpu.SemaphoreType.DMA((2, S5_CHUNK))],
        compiler_params=_params("arbitrary"),
        name="s5_core",
    )(u5, mt, bt, ct, apow)


def _ssd_kernel(z_ref, xbc_ref, dt_ref, cw_ref, cb_ref, dtb_ref, alog_ref, dsk_ref,
                gn_ref, e64_ref, y_ref, tail_scr, xc_scr, st_scr):
    lb = z_ref.shape[0]
    j = pl.program_id(1)

    @pl.when(j == 0)
    def _():
        tail_scr[...] = jnp.zeros_like(tail_scr)
        st_scr[...] = jnp.zeros_like(st_scr)

    xb = xbc_ref[...]
    x0 = xb.astype(F32)
    rr = lax.broadcasted_iota(jnp.int32, (lb, lb), 0)
    cc = lax.broadcasted_iota(jnp.int32, (lb, lb), 1)
    row8 = lax.broadcasted_iota(jnp.int32, (SUBLANES, 1), 0)
    tail = tail_scr[...]
    acc = cb_ref[...] + cw_ref[SSD_CONV - 1:SSD_CONV, :] * x0
    head = jnp.zeros((SUBLANES, SSD_CONV_DIM), F32)
    for k in range(1, SSD_CONV):
        wk = cw_ref[SSD_CONV - 1 - k:SSD_CONV - k, :]
        acc = acc + wk * _dot((rr - cc == k).astype(BF16), xb)
        head = head + wk * jnp.where(row8 < k, pltpu.roll(tail, k, axis=0), 0.0)
    tail_scr[...] = x0[lb - SUBLANES:, :]
    xc_scr[...] = acc * jax.nn.sigmoid(acc)
    acc0 = acc[0:SUBLANES, :] + head
    xc_scr[0:SUBLANES, :] = acc0 * jax.nn.sigmoid(acc0)

    dt_t = (dt_ref[...] + dtb_ref[...]).T[0:SSD_HEADS, :]
    dtp_t = jnp.maximum(dt_t, 0.0) + jnp.log(1.0 + jnp.exp(-jnp.abs(dt_t)))
    dtp_all = jnp.concatenate(
        [dtp_t, jnp.zeros((LANES - SSD_HEADS, lb), F32)], axis=0).T

    ri = lax.broadcasted_iota(jnp.int32, (SSD_CHUNK, SSD_CHUNK), 0)
    ci = lax.broadcasted_iota(jnp.int32, (SSD_CHUNK, SSD_CHUNK), 1)
    causal = ri >= ci
    ltri = causal.astype(BF16)
    lo_half = ci < SSD_HEAD_DIM
    a_neg = -jnp.exp(alog_ref[...])
    e64 = e64_ref[...]
    b0 = SSD_WIDTH
    c0 = SSD_WIDTH + SSD_GROUPS * SSD_STATE
    hpg = SSD_HEADS // SSD_GROUPS

    for c in range(lb // SSD_CHUNK):
        r0 = c * SSD_CHUNK
        xs = xc_scr[r0:r0 + SSD_CHUNK, 0:SSD_WIDTH]
        dtp = dtp_all[r0:r0 + SSD_CHUNK, :]
        dta = dtp * a_neg
        acum = _dot_sel_lhs(ltri, dta)
        acum_t = acum.T
        atot = acum[SSD_CHUNK - 1:SSD_CHUNK, :]
        fac = jnp.concatenate([dtp, dtp * jnp.exp(atot - acum), jnp.exp(acum)], axis=0)
        fac64 = _dot(fac.astype(BF16), e64)
        xdt = xs * fac64[0:SSD_CHUNK]
        xw = (xs * fac64[SSD_CHUNK:2 * SSD_CHUNK]).astype(BF16)
        ea64 = fac64[2 * SSD_CHUNK:]
        tail = SSD_CHUNK - SUBLANES
        etot64 = jnp.exp(_dot_sel_rhs(acum[tail:, :], e64)[SUBLANES - 1:SUBLANES, :])
        gmats = []
        for g in range(SSD_GROUPS):
            bm = xc_scr[r0:r0 + SSD_CHUNK, b0 + g * SSD_STATE:b0 + (g + 1) * SSD_STATE]
            cm = xc_scr[r0:r0 + SSD_CHUNK, c0 + g * SSD_STATE:c0 + (g + 1) * SSD_STATE].astype(BF16)
            bm_t = bm.T.astype(BF16)
            gmats.append((bm_t, cm, _dot(cm, bm_t)))
        ys = []
        for p in range(SSD_HEADS // 2):
            bm_t, cm, gm = gmats[(2 * p) // hpg]
            cols = slice(p * LANES, (p + 1) * LANES)
            xp = xdt[:, cols]
            y = None
            for hh in range(2):
                h = 2 * p + hh
                col = jnp.broadcast_to(acum[:, h:h + 1], (SSD_CHUNK, SSD_CHUNK))
                rowv = acum_t[h:h + 1, :]
                w = jnp.where(causal, gm * jnp.exp(col - rowv), 0.0).astype(BF16)
                xh = jnp.where(lo_half if hh == 0 else ~lo_half, xp, 0.0).astype(BF16)
                t = _dot(w, xh)
                y = t if y is None else y + t
            st = st_scr[p]
            y = y + _dot(cm, st.astype(BF16)) * ea64[:, cols]
            st_scr[p] = st * etot64[:, cols] + _dot(bm_t, xw[:, cols])
            ys.append(y + dsk_ref[:, cols] * xs[:, cols])
        yc = jnp.concatenate(ys, axis=1)
        zc = z_ref[r0:r0 + SSD_CHUNK, :].astype(F32)
        yc = yc * (zc * jax.nn.sigmoid(zc))
        y_ref[r0:r0 + SSD_CHUNK, :] = _rms(yc, gn_ref[...]).astype(y_ref.dtype)


def _ssd(z3, xbc3, dt3, cw, cb, dtb, alog, dsk, gn, e64, layer, lb):
    b, l, _ = z3.shape
    blk = lambda w: pl.BlockSpec((None, lb, w), lambda i, j: (i, j, 0))
    ls = lambda a: _layer_spec(a.shape, layer, 2)
    return pl.pallas_call(
        _ssd_kernel,
        grid=(b, l // lb),
        in_specs=[blk(SSD_WIDTH), blk(SSD_CONV_DIM), blk(LANES), ls(cw), ls(cb),
                  ls(dtb), ls(alog), ls(dsk), ls(gn),
                  pl.BlockSpec(e64.shape, lambda i, j: (0, 0))],
        out_specs=blk(SSD_WIDTH),
        out_shape=jax.ShapeDtypeStruct((b, l, SSD_WIDTH), BF16),
        scratch_shapes=[pltpu.VMEM((SUBLANES, SSD_CONV_DIM), F32),
                        pltpu.VMEM((lb, SSD_CONV_DIM), F32),
                        pltpu.VMEM((SSD_HEADS // 2, SSD_STATE, LANES), F32)],
        compiler_params=_params("parallel", "arbitrary"),
        name="ssd",
    )(z3, xbc3, dt3, cw, cb, dtb, alog, dsk, gn, e64)


FFN_CHUNK = MXU_WIDTH


def _gelu_tanh(v):
    return 0.5 * v * (1.0 + jnp.tanh(0.7978845608028654 * (v + 0.044715 * (v * v * v))))


def _post_kernel(x_ref, ys_ref, u_ref, yb_ref, d5_ref, wglu_ref, bglu_ref, g5_ref,
                 woa_ref, wob_ref, gf_ref, wg_ref, wu_ref, wd_ref, gfin_ref, o_ref, y_scr,
                 *, final):
    bsz, tok, _ = x_ref.shape
    for b in range(bsz):
        for c in range(tok // S5_CHUNK):
            r0 = b * tok + c * S5_CHUNK
            for j in range(D_MODEL // LANES):
                cols = slice(j * LANES, (j + 1) * LANES)
                rows = _s5_rows(c, j, b)
                y_scr[r0:r0 + S5_CHUNK, cols] = ys_ref[rows, :] + d5_ref[:, cols] * u_ref[rows, :]
    g = _gelu_tanh(y_scr[...])
    ya = g * jax.nn.sigmoid(_dot(g.astype(BF16), wglu_ref[...]) + bglu_ref[...])
    ya = _rms(ya, g5_ref[...]).astype(BF16)
    yb = yb_ref[...].reshape(bsz * tok, SSD_WIDTH)
    x = x_ref[...].reshape(bsz * tok, D_MODEL) + _dot(ya, woa_ref[...]) + _dot(yb, wob_ref[...])
    hb = _rms(x, gf_ref[...]).astype(BF16)
    y_scr[...] = x

    def ffn_chunk(k, carry):
        sl = pl.ds(pl.multiple_of(k * FFN_CHUNK, FFN_CHUNK), FFN_CHUNK)
        gate = _dot(hb, wg_ref[:, sl])
        act = (gate * jax.nn.sigmoid(gate)) * _dot(hb, wu_ref[:, sl])
        y_scr[...] += _dot(act.astype(BF16), wd_ref[sl, :])
        return carry

    lax.fori_loop(0, FFN_HIDDEN // FFN_CHUNK, ffn_chunk, 0)
    out = y_scr[...]
    if final:
        out = _rms(out, gfin_ref[...])
    o_ref[...] = out.reshape(o_ref.shape)


def _post(x3, ys, u, yb3, d5, wglu, bglu, g5, wo, gf, wg, wu, wd, gfin, layer, tok, final):
    b, seq, _ = x3.shape
    nat = pl.BlockSpec((b, tok, D_MODEL), lambda i: (0, i, 0))
    s5 = pl.BlockSpec((tok * b * D_MODEL // LANES, LANES), lambda i: (i, 0))
    one = pl.Buffered(1)
    ls = lambda a: pl.BlockSpec((None,) + a.shape[1:], lambda i: (layer, 0, 0), pipeline_mode=one)
    wo_half = lambda k: pl.BlockSpec((None, D_MODEL, D_MODEL), lambda i: (layer, k, 0),
                                     pipeline_mode=one)
    gfin_spec = pl.BlockSpec(gfin.shape, lambda i: (0, 0), pipeline_mode=one)
    return pl.pallas_call(
        functools.partial(_post_kernel, final=final),
        grid=(seq // tok,),
        in_specs=[nat, s5, s5, nat, ls(d5), ls(wglu), ls(bglu), ls(g5), wo_half(0), wo_half(1),
                  ls(gf), ls(wg), ls(wu), ls(wd), gfin_spec],
        out_specs=nat,
        out_shape=jax.ShapeDtypeStruct(x3.shape, F32),
        scratch_shapes=[pltpu.VMEM((b * tok, D_MODEL), F32)],
        compiler_params=_params("parallel"),
        name="post",
    )(x3, ys, u, yb3, d5, wglu, bglu, g5, wo, wo, gf, wg, wu, wd, gfin)


def kernel(x, norm_mix, w_in, s5_lam_re, s5_lam_im, s5_log_step, s5_b_re, s5_b_im, s5_c_re, s5_c_im, s5_d, s5_w_glu, s5_b_glu, s5_norm, ssd_conv_w, ssd_conv_b, ssd_dt_bias, ssd_a_log, ssd_d, ssd_norm, w_out, norm_ffn, w_gate, w_up, w_down, norm_final):
    bsz, seq, _ = x.shape
    depth = w_in.shape[0]
    assert bsz == SUBLANES and seq % S5_CHUNK == 0 and seq % SSD_CHUNK == 0
    nc5 = seq // S5_CHUNK
    tok = 2 * S5_CHUNK
    lb = min(256, seq)
    dg = depth * S5_GROUPS
    rows = lambda a: a.reshape(depth, 1, -1).astype(F32)

    w_all = _cast_win(jnp.swapaxes(w_in, 1, 2), 256)
    wglu = _cast_bf16(s5_w_glu, 256)
    wo = _cast_bf16(w_out, 256)
    wg = _cast_bf16(w_gate, 256)
    wup = _cast_bf16(w_up, 256)
    wd = _cast_bf16(w_down, 256)

    dbl = lambda a: jnp.concatenate([a, a], axis=-1)
    mt, bt, ct, apow = _s5_prep(
        s5_log_step.reshape(dg, 1, 1),
        s5_lam_re.reshape(dg, S5_STATE, 1), s5_lam_im.reshape(dg, S5_STATE, 1),
        dbl(s5_lam_re).reshape(dg, 1, 2 * S5_STATE), dbl(s5_lam_im).reshape(dg, 1, 2 * S5_STATE),
        dbl(s5_c_re).reshape(dg, S5_GROUP, 2 * S5_STATE),
        dbl(s5_c_im).reshape(dg, S5_GROUP, 2 * S5_STATE),
        s5_b_re.reshape(dg, S5_STATE, S5_GROUP), s5_b_im.reshape(dg, S5_STATE, S5_GROUP))

    head_of_col = jnp.arange(SSD_WIDTH) // SSD_HEAD_DIM
    e64 = (jnp.arange(LANES)[:, None] == head_of_col[None, :]).astype(BF16)
    pad_h = lambda a: jnp.pad(a.astype(F32), ((0, 0), (0, LANES - SSD_HEADS))).reshape(
        depth, 1, LANES)
    dtb, alog = pad_h(ssd_dt_bias), pad_h(ssd_a_log)
    dsk = rows(jnp.repeat(ssd_d, SSD_HEAD_DIM, axis=-1))
    g_mix, g_s5, g_ssd, g_ffn = rows(norm_mix), rows(s5_norm), rows(ssd_norm), rows(norm_ffn)
    d5, bglu, cb = rows(s5_d), rows(s5_b_glu), rows(ssd_conv_b)
    cw = ssd_conv_w.astype(F32)
    gfin = norm_final.reshape(1, -1).astype(F32)

    s5_shape = (nc5, D_MODEL // LANES, S5_CHUNK, bsz, LANES)
    for i in range(depth):
        u, z, xbc, dt = _in_proj(x, g_mix, w_all, i, tok)
        ys = _s5_core(u.reshape(s5_shape), mt, bt, ct, apow, i).reshape(u.shape)
        yb = _ssd(z, xbc, dt, cw, cb, dtb, alog, dsk, g_ssd, e64, i, lb)
        x = _post(x, ys, u, yb, d5, wglu, bglu, g_s5, wo, g_ffn, wg, wup, wd, gfin,
                  i, tok, i == depth - 1)
    return x
```

```python
import functools

import jax
import jax.numpy as jnp
from jax import lax
from jax.experimental import pallas as pl
from jax.experimental.pallas import tpu as pltpu

F32 = jnp.float32
BF16 = jnp.bfloat16
EPS = 1e-6

D_MODEL = 1024
S5_GROUP = 16
S5_GROUPS = 64
S5_STATE = 64
S5_CHUNK = 32
S5_FLAT = S5_CHUNK * S5_GROUP
SSD_HEAD_DIM = 64
SSD_HEADS = 16
SSD_GROUPS = 2
SSD_STATE = 128
SSD_CONV = 4
SSD_CHUNK = 128
SSD_WIDTH = SSD_HEADS * SSD_HEAD_DIM
SSD_CONV_DIM = SSD_WIDTH + 2 * SSD_GROUPS * SSD_STATE
FFN_HIDDEN = 2816
LANES = 128
SUBLANES = 8
MXU_WIDTH = 256
S5_GPB = LANES // S5_GROUP
VMEM_LIMIT = 56 * 1024 * 1024
IN_U = D_MODEL
IN_Z = IN_U + SSD_WIDTH
IN_X = IN_Z + SSD_CONV_DIM


def _params(*sem):
    return pltpu.CompilerParams(dimension_semantics=sem, vmem_limit_bytes=VMEM_LIMIT)


def _rms(x, gain):
    return x * lax.rsqrt(jnp.mean(x * x, axis=-1, keepdims=True) + EPS) * gain


def _dot(a, b):
    return jnp.dot(a, b, preferred_element_type=F32)


def _split3(v):
    hi = v.astype(BF16)
    r1 = v - hi.astype(F32)
    mid = r1.astype(BF16)
    lo = (r1 - mid.astype(F32)).astype(BF16)
    return hi, mid, lo


def _dot_sel_rhs(v, sel):
    hi, mid, lo = _split3(v)
    return _dot(hi, sel) + _dot(mid, sel) + _dot(lo, sel)


def _dot_sel_lhs(sel, v):
    hi, mid, lo = _split3(v)
    return _dot(sel, hi) + _dot(sel, mid) + _dot(sel, lo)


def _dot_f32(a, b):
    a3 = _split3(a)
    b3 = _split3(b)
    acc = None
    for i in range(3):
        for j in range(3 - i):
            t = _dot(a3[i], b3[j])
            acc = t if acc is None else acc + t
    return acc


def _layer_spec(shape, layer, n_grid, col=0):
    zeros = (0,) * (len(shape) - 2)
    if n_grid == 1:
        return pl.BlockSpec((None,) + shape[1:], lambda i: (layer,) + zeros + (col,))
    return pl.BlockSpec((None,) + shape[1:], lambda i, j: (layer,) + zeros + (col,))


def _cast_kernel(w_ref, o_ref):
    o_ref[...] = w_ref[...].astype(o_ref.dtype)


def _cast_bf16(w, rb):
    depth, rows, cols = w.shape
    spec = pl.BlockSpec((1, rb, cols), lambda i, j: (i, j, 0))
    return pl.pallas_call(
        _cast_kernel, grid=(depth, rows // rb), in_specs=[spec], out_specs=spec,
        out_shape=jax.ShapeDtypeStruct(w.shape, BF16),
        compiler_params=_params("parallel", "parallel"), name="cast",
    )(w)


def _cast_win_kernel(w_ref, o_ref, *, n_out):
    rb = w_ref.shape[1]
    valid = n_out - pl.program_id(1) * rb
    row = lax.broadcasted_iota(jnp.int32, (rb, 1), 0)
    o_ref[0] = jnp.where(row < valid, w_ref[0], 0.0).T.astype(o_ref.dtype)


def _cast_win(w_in_t, rb):
    depth, n_out, d = w_in_t.shape
    nblk = pl.cdiv(n_out, rb)
    return pl.pallas_call(
        functools.partial(_cast_win_kernel, n_out=n_out), grid=(depth, nblk),
        in_specs=[pl.BlockSpec((1, rb, d), lambda i, j: (i, j, 0))],
        out_specs=pl.BlockSpec((1, d, rb), lambda i, j: (i, 0, j)),
        out_shape=jax.ShapeDtypeStruct((depth, d, nblk * rb), BF16),
        compiler_params=_params("parallel", "parallel"), name="cast_win",
    )(w_in_t)


def _s5_rows(c, j, b):
    return pl.ds(((c * (D_MODEL // LANES) + j) * S5_CHUNK) * SUBLANES + b, S5_CHUNK,
                 stride=SUBLANES)


def _inproj_kernel(x_ref, g_ref, wu_ref, wz_ref, wx0_ref, wx1_ref, wx2_ref, wdt_ref,
                   u_ref, z_ref, xbc_ref, dt_ref):
    bsz, tok, _ = x_ref.shape
    hb = _rms(x_ref[...], g_ref[...]).astype(BF16).reshape(bsz * tok, D_MODEL)
    u = _dot(hb, wu_ref[...])
    for b in range(bsz):
        for c in range(tok // S5_CHUNK):
            r0 = b * tok + c * S5_CHUNK
            for j in range(D_MODEL // LANES):
                u_ref[_s5_rows(c, j, b), :] = u[r0:r0 + S5_CHUNK, j * LANES:(j + 1) * LANES]
    z_ref[...] = _dot(hb, wz_ref[...]).astype(z_ref.dtype).reshape(z_ref.shape)
    xbc = jnp.concatenate([_dot(hb, w[...]).astype(xbc_ref.dtype)
                           for w in (wx0_ref, wx1_ref, wx2_ref)], axis=1)
    xbc_ref[...] = xbc.reshape(xbc_ref.shape)
    dt_ref[...] = _dot(hb, wdt_ref[...]).reshape(dt_ref.shape)


def _in_proj(x3, gain, w_all, layer, tok):
    b, seq, _ = x3.shape
    nat = lambda w: pl.BlockSpec((b, tok, w), lambda i: (0, i, 0))
    win = lambda width, start: pl.BlockSpec((None, D_MODEL, width),
                                            lambda i: (layer, 0, start // width))
    xw = SSD_CONV_DIM // 3
    return pl.pallas_call(
        _inproj_kernel,
        grid=(seq // tok,),
        in_specs=[nat(D_MODEL), _layer_spec(gain.shape, layer, 1), win(D_MODEL, 0),
                  win(SSD_WIDTH, IN_U), win(xw, IN_Z), win(xw, IN_Z + xw), win(xw, IN_Z + 2 * xw),
                  win(LANES, IN_X)],
        out_specs=[pl.BlockSpec((tok * b * D_MODEL // LANES, LANES), lambda i: (i, 0)),
                   nat(SSD_WIDTH), nat(SSD_CONV_DIM), nat(LANES)],
        out_shape=[jax.ShapeDtypeStruct((seq * b * D_MODEL // LANES, LANES), F32),
                   jax.ShapeDtypeStruct((b, seq, SSD_WIDTH), BF16),
                   jax.ShapeDtypeStruct((b, seq, SSD_CONV_DIM), BF16),
                   jax.ShapeDtypeStruct((b, seq, LANES), F32)],
        compiler_params=_params("parallel"),
        name="in_proj",
    )(x3, gain, *([w_all] * 6))


def _s5_prep_group(q, ls_ref, lrc_ref, lic_ref, lrr_ref, lir_ref, cre_ref, cim_ref,
                   bre_ref, bim_ref, mt_ref, bt_ref, ct_ref, apow_ref):
    step = jnp.exp(ls_ref[q])
    lrc = lrc_ref[q]
    lic = lic_ref[q]
    p2 = 2 * S5_STATE

    k_lane = lax.broadcasted_iota(jnp.int32, (1, LANES), 1).astype(F32)
    st = step * k_lane
    mag = jnp.exp(lrc * st)
    pre_c = mag * jnp.cos(lic * st)
    pim_c = mag * jnp.sin(lic * st)
    are = pre_c[:, 1:2]
    aim = pim_c[:, 1:2]
    den = lrc * lrc + lic * lic
    nr = are - 1.0
    cfr = (nr * lrc + aim * lic) / den
    cfi = (aim * lrc - nr * lic) / den
    bre = bre_ref[q]
    bim = bim_ref[q]
    bbr = cfr * bre - cfi * bim
    bbi = cfr * bim + cfi * bre

    lane = lax.broadcasted_iota(jnp.int32, (1, S5_FLAT), 1)
    h_of_lane = lax.broadcasted_iota(jnp.int32, (S5_GROUP, S5_FLAT), 1) % S5_GROUP
    e_h = (h_of_lane == lax.broadcasted_iota(jnp.int32, (S5_GROUP, S5_FLAT), 0)).astype(BF16)
    k_row = lax.broadcasted_iota(jnp.int32, (LANES, S5_FLAT), 0)
    j_lane = lax.broadcasted_iota(jnp.int32, (LANES, S5_FLAT), 1) // S5_GROUP
    e_rev = (k_row == (S5_CHUNK - 1) - j_lane).astype(BF16)
    btr = _dot_sel_rhs(bbr, e_h)
    bti = _dot_sel_rhs(bbi, e_h)
    prr = _dot_sel_rhs(pre_c, e_rev)
    pri = _dot_sel_rhs(pim_c, e_rev)
    wre = prr * btr - pri * bti
    wim = prr * bti + pri * btr
    bt_ref[q] = jnp.concatenate([wre, wim, wim, wre], axis=0).astype(bt_ref.dtype)

    lane2 = lax.broadcasted_iota(jnp.int32, (1, p2), 1)
    first = lane2 < S5_STATE
    c1 = cre_ref[q]
    c2 = cim_ref[q]
    lhs = jnp.where(first, c1, c2)
    rev = _dot_f32(lhs, jnp.concatenate([wre, -wim], axis=0))
    for t in range(S5_CHUNK):
        rows = slice(t * S5_GROUP, (t + 1) * S5_GROUP)
        shift = (S5_FLAT - (S5_CHUNK - 1 - t) * S5_GROUP) % S5_FLAT
        blk = rev if shift == 0 else pltpu.roll(rev, shift, axis=1)
        if t < S5_CHUNK - 1:
            blk = jnp.where(lane < (t + 1) * S5_GROUP, blk, 0.0)
        mt_ref[q, rows, :] = blk.astype(mt_ref.dtype)

    lrr = lrr_ref[q]
    lir = lir_ref[q]
    tau = (lax.broadcasted_iota(jnp.int32, (S5_CHUNK, 1), 0) + 1).astype(F32)
    st_r = step * tau
    mag_r = jnp.exp(lrr * st_r)
    pre_r = mag_r * jnp.cos(lir * st_r)
    pim_r = mag_r * jnp.sin(lir * st_r)
    q1 = jnp.where(first, pre_r, pim_r)
    q2 = jnp.where(first, pim_r, pre_r)
    s1 = jnp.where(first, 1.0, -1.0).astype(F32)
    for t in range(S5_CHUNK):
        rows = slice(t * S5_GROUP, (t + 1) * S5_GROUP)
        ct_ref[q, rows, :] = (s1 * (c1 * q1[t:t + 1, :]) - c2 * q2[t:t + 1, :]).astype(ct_ref.dtype)
    atr = pre_r[S5_CHUNK - 1:S5_CHUNK, :]
    a2 = -s1 * pim_r[S5_CHUNK - 1:S5_CHUNK, :]
    apow_ref[q] = jnp.concatenate(
        [atr, a2, -a2, jnp.zeros((SUBLANES - 3, p2), F32)], axis=0)


def _s5_prep_kernel(*refs):
    for q in range(refs[0].shape[0]):
        _s5_prep_group(q, *refs)


def _s5_prep(ls, lrc, lic, lrr, lir, cre, cim, bre, bim):
    dg = ls.shape[0]
    gps = 8
    spec = lambda a: pl.BlockSpec((gps,) + a.shape[1:], lambda i: (i, 0, 0))
    ins = (ls, lrc, lic, lrr, lir, cre, cim, bre, bim)
    shapes = [(dg, S5_FLAT, S5_FLAT), (dg, 4 * S5_STATE, S5_FLAT),
              (dg, S5_FLAT, 2 * S5_STATE), (dg, SUBLANES, 2 * S5_STATE)]
    dtypes = [BF16, BF16, BF16, F32]
    return pl.pallas_call(
        _s5_prep_kernel,
        grid=(dg // gps,),
        in_specs=[spec(a) for a in ins],
        out_specs=[pl.BlockSpec((gps,) + s[1:], lambda i: (i, 0, 0)) for s in shapes],
        out_shape=[jax.ShapeDtypeStruct(s, d) for s, d in zip(shapes, dtypes)],
        compiler_params=_params("parallel"),
        name="s5_prep",
    )(*ins)


def _s5_core_kernel(u_hbm, mt_ref, bt_ref, ct_ref, ap_ref, y_hbm,
                    io_buf, zt_scr, e_scr, sp_scr, yt_scr, sem_in, sem_out, *, batch):
    ng = mt_ref.shape[0]
    nchunk = u_hbm.shape[0]
    cols = nchunk * batch
    half = S5_FLAT // 2
    j = pl.program_id(0)
    last = pl.num_programs(0) - 1
    slot = j % 2
    other = 1 - slot

    def in_copy(step, s, t):
        return pltpu.make_async_copy(u_hbm.at[:, step, t], io_buf.at[s, t], sem_in.at[s, t])

    def out_copy(step, s, t):
        return pltpu.make_async_copy(io_buf.at[s, t], y_hbm.at[:, step, t], sem_out.at[s, t])

    @pl.when(j == 0)
    def _():
        for t in range(S5_CHUNK):
            in_copy(j, slot, t).start()

    for t in range(S5_CHUNK):
        in_copy(j, slot, t).wait()
    for t in range(S5_CHUNK):
        blk_t = io_buf[slot, t].reshape(cols, LANES).T.astype(BF16)
        for g in range(ng):
            zt_scr[g, t * S5_GROUP:(t + 1) * S5_GROUP, :] = blk_t[g * S5_GROUP:(g + 1) * S5_GROUP, :]

    @pl.when(j > 0)
    def _():
        for t in range(S5_CHUNK):
            out_copy(j - 1, other, t).wait()

    @pl.when(j < last)
    def _():
        for t in range(S5_CHUNK):
            in_copy(j + 1, other, t).start()

    for g in range(ng):
        e_scr[g] = _dot(bt_ref[g], zt_scr[g]).T

    def body(c, carry):
        r = pl.multiple_of(c * batch, batch)
        out = []
        for g in range(ng):
            s, sw = carry[g]
            sp_scr[g, pl.ds(r, batch), :] = s
            e = e_scr[g, pl.ds(r, batch), :]
            a1 = ap_ref[g, 0:1, :]
            a2 = ap_ref[g, 1:2, :]
            a2w = ap_ref[g, 2:3, :]
            out.append((a1 * s + a2 * sw + e[:, :LANES], a1 * sw + a2w * s + e[:, LANES:]))
        return tuple(out)

    zero = jnp.zeros((batch, 2 * S5_STATE), F32)
    lax.fori_loop(0, nchunk, body, tuple((zero, zero) for _ in range(ng)))

    for g in range(ng):
        y_off = lax.dot_general(ct_ref[g], sp_scr[g].astype(BF16), (((1,), (1,)), ((), ())),
                                preferred_element_type=F32)
        top = _dot(mt_ref[g, :half, :half], zt_scr[g, :half, :])
        bot = _dot(mt_ref[g, half:, :], zt_scr[g])
        yt_scr[g, :half, :] = (top + y_off[:half]).astype(yt_scr.dtype)
        yt_scr[g, half:, :] = (bot + y_off[half:]).astype(yt_scr.dtype)

    for t in range(S5_CHUNK):
        rows = slice(t * S5_GROUP, (t + 1) * S5_GROUP)
        blk = jnp.concatenate([yt_scr[g, rows, :] for g in range(ng)], axis=0)
        io_buf[slot, t] = blk.astype(F32).T.reshape(nchunk, batch, LANES)
        out_copy(j, slot, t).start()

    @pl.when(j == last)
    def _():
        for t in range(S5_CHUNK):
            out_copy(j, slot, t).wait()


def _s5_core(u5, mt, bt, ct, apow, layer):
    nchunk, nblk, _, batch, _ = u5.shape
    cols = nchunk * batch
    ng = S5_GPB
    op = lambda a: pl.BlockSpec((ng,) + a.shape[1:], lambda i: (layer * nblk + i, 0, 0))
    hbm = pl.BlockSpec(memory_space=pl.ANY)
    return pl.pallas_call(
        functools.partial(_s5_core_kernel, batch=batch),
        grid=(nblk,),
        in_specs=[hbm, op(mt), op(bt), op(ct), op(apow)],
        out_specs=hbm,
        out_shape=jax.ShapeDtypeStruct(u5.shape, F32),
        scratch_shapes=[pltpu.VMEM((2, S5_CHUNK, nchunk, batch, LANES), F32),
                        pltpu.VMEM((ng, S5_FLAT, cols), BF16),
                        pltpu.VMEM((ng, cols, 4 * S5_STATE), F32),
                        pltpu.VMEM((ng, cols, 2 * S5_STATE), F32),
                        pltpu.VMEM((ng, S5_FLAT, cols), BF16),
                        pltpu.SemaphoreType.DMA((2, S5_CHUNK)),
                        pltpu.SemaphoreType.DMA((2, S5_CHUNK))],
        compiler_params=_params("arbitrary"),
        name="s5_core",
    )(u5, mt, bt, ct, apow)


def _ssd_kernel(z_ref, xbc_ref, dt_ref, cw_ref, cb_ref, dtb_ref, alog_ref, dsk_ref,
                gn_ref, e64_ref, y_ref, tail_scr, xc_scr, st_scr):
    for i in range(z_ref.shape[0]):
        _ssd_batch(z_ref.at[i], xbc_ref.at[i], dt_ref.at[i], cw_ref, cb_ref, dtb_ref, alog_ref,
                   dsk_ref, gn_ref, e64_ref, y_ref.at[i], tail_scr.at[i], xc_scr.at[i],
                   st_scr.at[i])


def _ssd_batch(z_ref, xbc_ref, dt_ref, cw_ref, cb_ref, dtb_ref, alog_ref, dsk_ref,
               gn_ref, e64_ref, y_ref, tail_scr, xc_scr, st_scr):
    lb = z_ref.shape[0]
    j = pl.program_id(1)

    @pl.when(j == 0)
    def _():
        tail_scr[...] = jnp.zeros_like(tail_scr)
        st_scr[...] = jnp.zeros_like(st_scr)

    xb = xbc_ref[...]
    x0 = xb.astype(F32)
    rr = lax.broadcasted_iota(jnp.int32, (lb, lb), 0)
    cc = lax.broadcasted_iota(jnp.int32, (lb, lb), 1)
    row8 = lax.broadcasted_iota(jnp.int32, (SUBLANES, 1), 0)
    tail = tail_scr[...]
    acc = cb_ref[...] + cw_ref[SSD_CONV - 1:SSD_CONV, :] * x0
    head = jnp.zeros((SUBLANES, SSD_CONV_DIM), F32)
    for k in range(1, SSD_CONV):
        wk = cw_ref[SSD_CONV - 1 - k:SSD_CONV - k, :]
        acc = acc + wk * _dot((rr - cc == k).astype(BF16), xb)
        head = head + wk * jnp.where(row8 < k, pltpu.roll(tail, k, axis=0), 0.0)
    tail_scr[...] = x0[lb - SUBLANES:, :]
    xc_scr[...] = acc * jax.nn.sigmoid(acc)
    acc0 = acc[0:SUBLANES, :] + head
    xc_scr[0:SUBLANES, :] = acc0 * jax.nn.sigmoid(acc0)

    dt_t = (dt_ref[...] + dtb_ref[...]).T[0:SSD_HEADS, :]
    dtp_t = jnp.maximum(dt_t, 0.0) + jnp.log(1.0 + jnp.exp(-jnp.abs(dt_t)))
    dtp_all = jnp.concatenate(
        [dtp_t, jnp.zeros((LANES - SSD_HEADS, lb), F32)], axis=0).T

    ri = lax.broadcasted_iota(jnp.int32, (SSD_CHUNK, SSD_CHUNK), 0)
    ci = lax.broadcasted_iota(jnp.int32, (SSD_CHUNK, SSD_CHUNK), 1)
    causal = ri >= ci
    ltri = causal.astype(BF16)
    lo_half = ci < SSD_HEAD_DIM
    a_neg = -jnp.exp(alog_ref[...])
    e64 = e64_ref[...]
    b0 = SSD_WIDTH
    c0 = SSD_WIDTH + SSD_GROUPS * SSD_STATE
    hpg = SSD_HEADS // SSD_GROUPS

    for c in range(lb // SSD_CHUNK):
        r0 = c * SSD_CHUNK
        xs = xc_scr[r0:r0 + SSD_CHUNK, 0:SSD_WIDTH]
        dtp = dtp_all[r0:r0 + SSD_CHUNK, :]
        dta = dtp * a_neg
        acum = _dot_sel_lhs(ltri, dta)
        acum_t = acum.T
        atot = acum[SSD_CHUNK - 1:SSD_CHUNK, :]
        fac = jnp.concatenate([dtp, dtp * jnp.exp(atot - acum), jnp.exp(acum)], axis=0)
        fac64 = _dot(fac.astype(BF16), e64)
        xdt = xs * fac64[0:SSD_CHUNK]
        xw = (xs * fac64[SSD_CHUNK:2 * SSD_CHUNK]).astype(BF16)
        ea64 = fac64[2 * SSD_CHUNK:]
        tail = SSD_CHUNK - SUBLANES
        etot64 = jnp.exp(_dot_sel_rhs(acum[tail:, :], e64)[SUBLANES - 1:SUBLANES, :])
        gmats = []
        for g in range(SSD_GROUPS):
            bm = xc_scr[r0:r0 + SSD_CHUNK, b0 + g * SSD_STATE:b0 + (g + 1) * SSD_STATE]
            cm = xc_scr[r0:r0 + SSD_CHUNK, c0 + g * SSD_STATE:c0 + (g + 1) * SSD_STATE].astype(BF16)
            bm_t = bm.T.astype(BF16)
            gmats.append((bm_t, cm, _dot(cm, bm_t)))
        ys = []
        for p in range(SSD_HEADS // 2):
            bm_t, cm, gm = gmats[(2 * p) // hpg]
            cols = slice(p * LANES, (p + 1) * LANES)
            xp = xdt[:, cols]
            y = None
            for hh in range(2):
                h = 2 * p + hh
                col = jnp.broadcast_to(acum[:, h:h + 1], (SSD_CHUNK, SSD_CHUNK))
                rowv = acum_t[h:h + 1, :]
                w = jnp.where(causal, gm * jnp.exp(col - rowv), 0.0).astype(BF16)
                xh = jnp.where(lo_half if hh == 0 else ~lo_half, xp, 0.0).astype(BF16)
                t = _dot(w, xh)
                y = t if y is None else y + t
            st = st_scr[p]
            y = y + _dot(cm, st.astype(BF16)) * ea64[:, cols]
            st_scr[p] = st * etot64[:, cols] + _dot(bm_t, xw[:, cols])
            ys.append(y + dsk_ref[:, cols] * xs[:, cols])
        yc = jnp.concatenate(ys, axis=1)
        zc = z_ref[r0:r0 + SSD_CHUNK, :].astype(F32)
        yc = yc * (zc * jax.nn.sigmoid(zc))
        y_ref[r0:r0 + SSD_CHUNK, :] = _rms(yc, gn_ref[...]).astype(y_ref.dtype)


def _ssd(z3, xbc3, dt3, cw, cb, dtb, alog, dsk, gn, e64, layer, lb, nb):
    b, l, _ = z3.shape
    blk = lambda w: pl.BlockSpec((nb, lb, w), lambda i, j: (i, j, 0))
    ls = lambda a: _layer_spec(a.shape, layer, 2)
    return pl.pallas_call(
        _ssd_kernel,
        grid=(b // nb, l // lb),
        in_specs=[blk(SSD_WIDTH), blk(SSD_CONV_DIM), blk(LANES), ls(cw), ls(cb),
                  ls(dtb), ls(alog), ls(dsk), ls(gn),
                  pl.BlockSpec(e64.shape, lambda i, j: (0, 0))],
        out_specs=blk(SSD_WIDTH),
        out_shape=jax.ShapeDtypeStruct((b, l, SSD_WIDTH), BF16),
        scratch_shapes=[pltpu.VMEM((nb, SUBLANES, SSD_CONV_DIM), F32),
                        pltpu.VMEM((nb, lb, SSD_CONV_DIM), F32),
                        pltpu.VMEM((nb, SSD_HEADS // 2, SSD_STATE, LANES), F32)],
        compiler_params=_params("parallel", "arbitrary"),
        name="ssd",
    )(z3, xbc3, dt3, cw, cb, dtb, alog, dsk, gn, e64)


FFN_CHUNK = MXU_WIDTH


def _gelu_tanh(v):
    return 0.5 * v * (1.0 + jnp.tanh(0.7978845608028654 * (v + 0.044715 * (v * v * v))))


def _post_kernel(x_ref, ys_ref, u_ref, yb_ref, d5_ref, wglu_ref, bglu_ref, g5_ref,
                 woa_ref, wob_ref, gf_ref, wg_ref, wu_ref, wd_ref, gfin_ref, o_ref, y_scr,
                 *, final):
    bsz, tok, _ = x_ref.shape
    for b in range(bsz):
        for c in range(tok // S5_CHUNK):
            r0 = b * tok + c * S5_CHUNK
            for j in range(D_MODEL // LANES):
                cols = slice(j * LANES, (j + 1) * LANES)
                rows = _s5_rows(c, j, b)
                y_scr[r0:r0 + S5_CHUNK, cols] = ys_ref[rows, :] + d5_ref[:, cols] * u_ref[rows, :]
    g = _gelu_tanh(y_scr[...])
    ya = g * jax.nn.sigmoid(_dot(g.astype(BF16), wglu_ref[...]) + bglu_ref[...])
    ya = _rms(ya, g5_ref[...]).astype(BF16)
    yb = yb_ref[...].reshape(bsz * tok, SSD_WIDTH)
    x = x_ref[...].reshape(bsz * tok, D_MODEL) + _dot(ya, woa_ref[...]) + _dot(yb, wob_ref[...])
    hb = _rms(x, gf_ref[...]).astype(BF16)
    y_scr[...] = x

    def ffn_chunk(k, carry):
        sl = pl.ds(pl.multiple_of(k * FFN_CHUNK, FFN_CHUNK), FFN_CHUNK)
        gate = _dot(hb, wg_ref[:, sl])
        act = (gate * jax.nn.sigmoid(gate)) * _dot(hb, wu_ref[:, sl])
        y_scr[...] += _dot(act.astype(BF16), wd_ref[sl, :])
        return carry

    lax.fori_loop(0, FFN_HIDDEN // FFN_CHUNK, ffn_chunk, 0, unroll=4)
    out = y_scr[...]
    if final:
        out = _rms(out, gfin_ref[...])
    o_ref[...] = out.reshape(o_ref.shape)


def _post(x3, ys, u, yb3, d5, wglu, bglu, g5, wo, gf, wg, wu, wd, gfin, layer, tok, final):
    b, seq, _ = x3.shape
    nat = pl.BlockSpec((b, tok, D_MODEL), lambda i: (0, i, 0))
    s5 = pl.BlockSpec((tok * b * D_MODEL // LANES, LANES), lambda i: (i, 0))
    one = pl.Buffered(1)
    ls = lambda a: pl.BlockSpec((None,) + a.shape[1:], lambda i: (layer, 0, 0), pipeline_mode=one)
    wo_half = lambda k: pl.BlockSpec((None, D_MODEL, D_MODEL), lambda i: (layer, k, 0),
                                     pipeline_mode=one)
    gfin_spec = pl.BlockSpec(gfin.shape, lambda i: (0, 0), pipeline_mode=one)
    return pl.pallas_call(
        functools.partial(_post_kernel, final=final),
        grid=(seq // tok,),
        in_specs=[nat, s5, s5, nat, ls(d5), ls(wglu), ls(bglu), ls(g5), wo_half(0), wo_half(1),
                  ls(gf), ls(wg), ls(wu), ls(wd), gfin_spec],
        out_specs=nat,
        out_shape=jax.ShapeDtypeStruct(x3.shape, F32),
        scratch_shapes=[pltpu.VMEM((b * tok, D_MODEL), F32)],
        compiler_params=_params("parallel"),
        name="post",
    )(x3, ys, u, yb3, d5, wglu, bglu, g5, wo, wo, gf, wg, wu, wd, gfin)


def kernel(x, norm_mix, w_in, s5_lam_re, s5_lam_im, s5_log_step, s5_b_re, s5_b_im, s5_c_re, s5_c_im, s5_d, s5_w_glu, s5_b_glu, s5_norm, ssd_conv_w, ssd_conv_b, ssd_dt_bias, ssd_a_log, ssd_d, ssd_norm, w_out, norm_ffn, w_gate, w_up, w_down, norm_final):
    bsz, seq, _ = x.shape
    depth = w_in.shape[0]
    assert bsz == SUBLANES and seq % S5_CHUNK == 0 and seq % SSD_CHUNK == 0
    nc5 = seq // S5_CHUNK
    tok = 2 * S5_CHUNK
    lb = min(256, seq)
    dg = depth * S5_GROUPS
    rows = lambda a: a.reshape(depth, 1, -1).astype(F32)

    w_all = _cast_win(jnp.swapaxes(w_in, 1, 2), 256)
    wglu = _cast_bf16(s5_w_glu, 256)
    wo = _cast_bf16(w_out, 256)
    wg = _cast_bf16(w_gate, 256)
    wup = _cast_bf16(w_up, 256)
    wd = _cast_bf16(w_down, 256)

    dbl = lambda a: jnp.concatenate([a, a], axis=-1)
    mt, bt, ct, apow = _s5_prep(
        s5_log_step.reshape(dg, 1, 1),
        s5_lam_re.reshape(dg, S5_STATE, 1), s5_lam_im.reshape(dg, S5_STATE, 1),
        dbl(s5_lam_re).reshape(dg, 1, 2 * S5_STATE), dbl(s5_lam_im).reshape(dg, 1, 2 * S5_STATE),
        dbl(s5_c_re).reshape(dg, S5_GROUP, 2 * S5_STATE),
        dbl(s5_c_im).reshape(dg, S5_GROUP, 2 * S5_STATE),
        s5_b_re.reshape(dg, S5_STATE, S5_GROUP), s5_b_im.reshape(dg, S5_STATE, S5_GROUP))

    head_of_col = jnp.arange(SSD_WIDTH) // SSD_HEAD_DIM
    e64 = (jnp.arange(LANES)[:, None] == head_of_col[None, :]).astype(BF16)
    pad_h = lambda a: jnp.pad(a.astype(F32), ((0, 0), (0, LANES - SSD_HEADS))).reshape(
        depth, 1, LANES)
    dtb, alog = pad_h(ssd_dt_bias), pad_h(ssd_a_log)
    dsk = rows(jnp.repeat(ssd_d, SSD_HEAD_DIM, axis=-1))
    g_mix, g_s5, g_ssd, g_ffn = rows(norm_mix), rows(s5_norm), rows(ssd_norm), rows(norm_ffn)
    d5, bglu, cb = rows(s5_d), rows(s5_b_glu), rows(ssd_conv_b)
    cw = ssd_conv_w.astype(F32)
    gfin = norm_final.reshape(1, -1).astype(F32)

    s5_shape = (nc5, D_MODEL // LANES, S5_CHUNK, bsz, LANES)
    for i in range(depth):
        u, z, xbc, dt = _in_proj(x, g_mix, w_all, i, tok)
        ys = _s5_core(u.reshape(s5_shape), mt, bt, ct, apow, i).reshape(u.shape)
        yb = _ssd(z, xbc, dt, cw, cb, dtb, alog, dsk, g_ssd, e64, i, lb, 2)
        x = _post(x, ys, u, yb, d5, wglu, bglu, g_s5, wo, g_ffn, wg, wup, wd, gfin,
                  i, tok, i == depth - 1)
    return x
```

```python
import functools

import jax
import jax.numpy as jnp
from jax import lax
from jax.experimental import pallas as pl
from jax.experimental.pallas import tpu as pltpu

F32 = jnp.float32
BF16 = jnp.bfloat16
EPS = 1e-6

D_MODEL = 1024
S5_GROUP = 16
S5_GROUPS = 64
S5_STATE = 64
S5_CHUNK = 32
S5_FLAT = S5_CHUNK * S5_GROUP
SSD_HEAD_DIM = 64
SSD_HEADS = 16
SSD_GROUPS = 2
SSD_STATE = 128
SSD_CONV = 4
SSD_CHUNK = 128
SSD_WIDTH = SSD_HEADS * SSD_HEAD_DIM
SSD_CONV_DIM = SSD_WIDTH + 2 * SSD_GROUPS * SSD_STATE
FFN_HIDDEN = 2816
LANES = 128
SUBLANES = 8
MXU_WIDTH = 256
S5_GPB = LANES // S5_GROUP
S5_DMAS = 4
VMEM_LIMIT = 56 * 1024 * 1024
IN_U = D_MODEL
IN_Z = IN_U + SSD_WIDTH
IN_X = IN_Z + SSD_CONV_DIM


def _params(*sem):
    return pltpu.CompilerParams(dimension_semantics=sem, vmem_limit_bytes=VMEM_LIMIT)


def _rms(x, gain):
    return x * lax.rsqrt(jnp.mean(x * x, axis=-1, keepdims=True) + EPS) * gain


def _dot(a, b):
    return jnp.dot(a, b, preferred_element_type=F32)


def _split3(v):
    hi = v.astype(BF16)
    r1 = v - hi.astype(F32)
    mid = r1.astype(BF16)
    lo = (r1 - mid.astype(F32)).astype(BF16)
    return hi, mid, lo


def _dot_sel_rhs(v, sel):
    hi, mid, lo = _split3(v)
    return _dot(hi, sel) + _dot(mid, sel) + _dot(lo, sel)


def _dot_sel_lhs(sel, v):
    hi, mid, lo = _split3(v)
    return _dot(sel, hi) + _dot(sel, mid) + _dot(sel, lo)


def _dot_f32(a, b):
    a3 = _split3(a)
    b3 = _split3(b)
    acc = None
    for i in range(3):
        for j in range(3 - i):
            t = _dot(a3[i], b3[j])
            acc = t if acc is None else acc + t
    return acc


def _layer_spec(shape, layer, n_grid, col=0):
    zeros = (0,) * (len(shape) - 2)
    if n_grid == 1:
        return pl.BlockSpec((None,) + shape[1:], lambda i: (layer,) + zeros + (col,))
    return pl.BlockSpec((None,) + shape[1:], lambda i, j: (layer,) + zeros + (col,))


def _cast_kernel(w_ref, o_ref):
    o_ref[...] = w_ref[...].astype(o_ref.dtype)


def _cast_bf16(w, rb):
    depth, rows, cols = w.shape
    spec = pl.BlockSpec((1, rb, cols), lambda i, j: (i, j, 0))
    return pl.pallas_call(
        _cast_kernel, grid=(depth, rows // rb), in_specs=[spec], out_specs=spec,
        out_shape=jax.ShapeDtypeStruct(w.shape, BF16),
        compiler_params=_params("parallel", "parallel"), name="cast",
    )(w)


def _cast_win_kernel(w_ref, o_ref, *, n_out):
    rb = w_ref.shape[1]
    valid = n_out - pl.program_id(1) * rb
    row = lax.broadcasted_iota(jnp.int32, (rb, 1), 0)
    o_ref[0] = jnp.where(row < valid, w_ref[0], 0.0).T.astype(o_ref.dtype)


def _cast_win(w_in_t, rb):
    depth, n_out, d = w_in_t.shape
    nblk = pl.cdiv(n_out, rb)
    return pl.pallas_call(
        functools.partial(_cast_win_kernel, n_out=n_out), grid=(depth, nblk),
        in_specs=[pl.BlockSpec((1, rb, d), lambda i, j: (i, j, 0))],
        out_specs=pl.BlockSpec((1, d, rb), lambda i, j: (i, 0, j)),
        out_shape=jax.ShapeDtypeStruct((depth, d, nblk * rb), BF16),
        compiler_params=_params("parallel", "parallel"), name="cast_win",
    )(w_in_t)


def _s5_rows(c, j, b):
    return pl.ds(((c * (D_MODEL // LANES) + j) * S5_CHUNK) * SUBLANES + b, S5_CHUNK,
                 stride=SUBLANES)


def _inproj_kernel(x_ref, g_ref, wu_ref, wz_ref, wx0_ref, wx1_ref, wx2_ref, wdt_ref,
                   u_ref, z_ref, xbc_ref, dt_ref):
    bsz, tok, _ = x_ref.shape
    hb = _rms(x_ref[...], g_ref[...]).astype(BF16).reshape(bsz * tok, D_MODEL)
    u = _dot(hb, wu_ref[...])
    for b in range(bsz):
        for c in range(tok // S5_CHUNK):
            r0 = b * tok + c * S5_CHUNK
            for j in range(D_MODEL // LANES):
                u_ref[_s5_rows(c, j, b), :] = u[r0:r0 + S5_CHUNK, j * LANES:(j + 1) * LANES]
    z_ref[...] = _dot(hb, wz_ref[...]).astype(z_ref.dtype).reshape(z_ref.shape)
    xbc = jnp.concatenate([_dot(hb, w[...]).astype(xbc_ref.dtype)
                           for w in (wx0_ref, wx1_ref, wx2_ref)], axis=1)
    xbc_ref[...] = xbc.reshape(xbc_ref.shape)
    dt_ref[...] = _dot(hb, wdt_ref[...]).reshape(dt_ref.shape)


def _in_proj(x3, gain, w_all, layer, tok):
    b, seq, _ = x3.shape
    nat = lambda w: pl.BlockSpec((b, tok, w), lambda i: (0, i, 0))
    win = lambda width, start: pl.BlockSpec((None, D_MODEL, width),
                                            lambda i: (layer, 0, start // width))
    xw = SSD_CONV_DIM // 3
    return pl.pallas_call(
        _inproj_kernel,
        grid=(seq // tok,),
        in_specs=[nat(D_MODEL), _layer_spec(gain.shape, layer, 1), win(D_MODEL, 0),
                  win(SSD_WIDTH, IN_U), win(xw, IN_Z), win(xw, IN_Z + xw), win(xw, IN_Z + 2 * xw),
                  win(LANES, IN_X)],
        out_specs=[pl.BlockSpec((tok * b * D_MODEL // LANES, LANES), lambda i: (i, 0)),
                   nat(SSD_WIDTH), nat(SSD_CONV_DIM), nat(LANES)],
        out_shape=[jax.ShapeDtypeStruct((seq * b * D_MODEL // LANES, LANES), F32),
                   jax.ShapeDtypeStruct((b, seq, SSD_WIDTH), BF16),
                   jax.ShapeDtypeStruct((b, seq, SSD_CONV_DIM), BF16),
                   jax.ShapeDtypeStruct((b, seq, LANES), F32)],
        compiler_params=_params("parallel"),
        name="in_proj",
    )(x3, gain, *([w_all] * 6))


def _s5_prep_group(q, ls_ref, lrc_ref, lic_ref, lrr_ref, lir_ref, cre_ref, cim_ref,
                   bre_ref, bim_ref, mt_ref, bt_ref, ct_ref, apow_ref):
    step = jnp.exp(ls_ref[q])
    lrc = lrc_ref[q]
    lic = lic_ref[q]
    p2 = 2 * S5_STATE

    k_lane = lax.broadcasted_iota(jnp.int32, (1, LANES), 1).astype(F32)
    st = step * k_lane
    mag = jnp.exp(lrc * st)
    pre_c = mag * jnp.cos(lic * st)
    pim_c = mag * jnp.sin(lic * st)
    are = pre_c[:, 1:2]
    aim = pim_c[:, 1:2]
    den = lrc * lrc + lic * lic
    nr = are - 1.0
    cfr = (nr * lrc + aim * lic) / den
    cfi = (aim * lrc - nr * lic) / den
    bre = bre_ref[q]
    bim = bim_ref[q]
    bbr = cfr * bre - cfi * bim
    bbi = cfr * bim + cfi * bre

    lane = lax.broadcasted_iota(jnp.int32, (1, S5_FLAT), 1)
    h_of_lane = lax.broadcasted_iota(jnp.int32, (S5_GROUP, S5_FLAT), 1) % S5_GROUP
    e_h = (h_of_lane == lax.broadcasted_iota(jnp.int32, (S5_GROUP, S5_FLAT), 0)).astype(BF16)
    k_row = lax.broadcasted_iota(jnp.int32, (LANES, S5_FLAT), 0)
    j_lane = lax.broadcasted_iota(jnp.int32, (LANES, S5_FLAT), 1) // S5_GROUP
    e_rev = (k_row == (S5_CHUNK - 1) - j_lane).astype(BF16)
    btr = _dot_sel_rhs(bbr, e_h)
    bti = _dot_sel_rhs(bbi, e_h)
    prr = _dot_sel_rhs(pre_c, e_rev)
    pri = _dot_sel_rhs(pim_c, e_rev)
    wre = prr * btr - pri * bti
    wim = prr * bti + pri * btr
    bt_ref[q] = jnp.concatenate([wre, wim, wim, wre], axis=0).astype(bt_ref.dtype)

    lane2 = lax.broadcasted_iota(jnp.int32, (1, p2), 1)
    first = lane2 < S5_STATE
    c1 = cre_ref[q]
    c2 = cim_ref[q]
    lhs = jnp.where(first, c1, c2)
    rev = _dot_f32(lhs, jnp.concatenate([wre, -wim], axis=0))
    for t in range(S5_CHUNK):
        rows = slice(t * S5_GROUP, (t + 1) * S5_GROUP)
        shift = (S5_FLAT - (S5_CHUNK - 1 - t) * S5_GROUP) % S5_FLAT
        blk = rev if shift == 0 else pltpu.roll(rev, shift, axis=1)
        if t < S5_CHUNK - 1:
            blk = jnp.where(lane < (t + 1) * S5_GROUP, blk, 0.0)
        mt_ref[q, rows, :] = blk.astype(mt_ref.dtype)

    lrr = lrr_ref[q]
    lir = lir_ref[q]
    tau = (lax.broadcasted_iota(jnp.int32, (S5_CHUNK, 1), 0) + 1).astype(F32)
    st_r = step * tau
    mag_r = jnp.exp(lrr * st_r)
    pre_r = mag_r * jnp.cos(lir * st_r)
    pim_r = mag_r * jnp.sin(lir * st_r)
    q1 = jnp.where(first, pre_r, pim_r)
    q2 = jnp.where(first, pim_r, pre_r)
    s1 = jnp.where(first, 1.0, -1.0).astype(F32)
    for t in range(S5_CHUNK):
        rows = slice(t * S5_GROUP, (t + 1) * S5_GROUP)
        ct_ref[q, rows, :] = (s1 * (c1 * q1[t:t + 1, :]) - c2 * q2[t:t + 1, :]).astype(ct_ref.dtype)
    atr = pre_r[S5_CHUNK - 1:S5_CHUNK, :]
    a2 = -s1 * pim_r[S5_CHUNK - 1:S5_CHUNK, :]
    apow_ref[q] = jnp.concatenate(
        [atr, a2, -a2, jnp.zeros((SUBLANES - 3, p2), F32)], axis=0)


def _s5_prep_kernel(*refs):
    for q in range(refs[0].shape[0]):
        _s5_prep_group(q, *refs)


def _s5_prep(ls, lrc, lic, lrr, lir, cre, cim, bre, bim):
    dg = ls.shape[0]
    gps = 8
    spec = lambda a: pl.BlockSpec((gps,) + a.shape[1:], lambda i: (i, 0, 0))
    ins = (ls, lrc, lic, lrr, lir, cre, cim, bre, bim)
    shapes = [(dg, S5_FLAT, S5_FLAT), (dg, 4 * S5_STATE, S5_FLAT),
              (dg, S5_FLAT, 2 * S5_STATE), (dg, SUBLANES, 2 * S5_STATE)]
    dtypes = [BF16, BF16, BF16, F32]
    return pl.pallas_call(
        _s5_prep_kernel,
        grid=(dg // gps,),
        in_specs=[spec(a) for a in ins],
        out_specs=[pl.BlockSpec((gps,) + s[1:], lambda i: (i, 0, 0)) for s in shapes],
        out_shape=[jax.ShapeDtypeStruct(s, d) for s, d in zip(shapes, dtypes)],
        compiler_params=_params("parallel"),
        name="s5_prep",
    )(*ins)


def _s5_core_kernel(u_hbm, mt_ref, bt_ref, ct_ref, ap_ref, y_hbm,
                    io_buf, zt_scr, e_scr, sp_scr, yt_scr, sem_in, sem_out, *, batch):
    ng = mt_ref.shape[0]
    nchunk = u_hbm.shape[0]
    cols = nchunk * batch
    half = S5_FLAT // 2
    j = pl.program_id(0)
    last = pl.num_programs(0) - 1
    slot = j % 2
    other = 1 - slot

    nq = sem_in.shape[1]
    cq = nchunk // nq

    def in_copy(step, s, q):
        span = pl.ds(q * cq, cq)
        return pltpu.make_async_copy(u_hbm.at[span, step], io_buf.at[s, span], sem_in.at[s, q])

    def out_copy(step, s, q):
        span = pl.ds(q * cq, cq)
        return pltpu.make_async_copy(io_buf.at[s, span], y_hbm.at[span, step], sem_out.at[s, q])

    @pl.when(j == 0)
    def _():
        for q in range(nq):
            in_copy(j, slot, q).start()

    for q in range(nq):
        in_copy(j, slot, q).wait()
    for t in range(S5_CHUNK):
        blk_t = io_buf[slot, :, t].reshape(cols, LANES).T.astype(BF16)
        for g in range(ng):
            zt_scr[g, t * S5_GROUP:(t + 1) * S5_GROUP, :] = blk_t[g * S5_GROUP:(g + 1) * S5_GROUP, :]

    @pl.when(j > 0)
    def _():
        for q in range(nq):
            out_copy(j - 1, other, q).wait()

    @pl.when(j < last)
    def _():
        for q in range(nq):
            in_copy(j + 1, other, q).start()

    for g in range(ng):
        e_scr[g] = _dot(bt_ref[g], zt_scr[g]).T

    def body(c, carry):
        r = pl.multiple_of(c * batch, batch)
        out = []
        for g in range(ng):
            s, sw = carry[g]
            sp_scr[g, pl.ds(r, batch), :] = s
            e = e_scr[g, pl.ds(r, batch), :]
            a1 = ap_ref[g, 0:1, :]
            a2 = ap_ref[g, 1:2, :]
            a2w = ap_ref[g, 2:3, :]
            out.append((a1 * s + a2 * sw + e[:, :LANES], a1 * sw + a2w * s + e[:, LANES:]))
        return tuple(out)

    zero = jnp.zeros((batch, 2 * S5_STATE), F32)
    lax.fori_loop(0, nchunk, body, tuple((zero, zero) for _ in range(ng)))

    for g in range(ng):
        y_off = lax.dot_general(ct_ref[g], sp_scr[g].astype(BF16), (((1,), (1,)), ((), ())),
                                preferred_element_type=F32)
        top = _dot(mt_ref[g, :half, :half], zt_scr[g, :half, :])
        bot = _dot(mt_ref[g, half:, :], zt_scr[g])
        yt_scr[g, :half, :] = (top + y_off[:half]).astype(yt_scr.dtype)
        yt_scr[g, half:, :] = (bot + y_off[half:]).astype(yt_scr.dtype)

    for t in range(S5_CHUNK):
        rows = slice(t * S5_GROUP, (t + 1) * S5_GROUP)
        blk = jnp.concatenate([yt_scr[g, rows, :] for g in range(ng)], axis=0)
        io_buf[slot, :, t] = blk.astype(F32).T.reshape(nchunk, batch, LANES)
    for q in range(nq):
        out_copy(j, slot, q).start()

    @pl.when(j == last)
    def _():
        for q in range(nq):
            out_copy(j, slot, q).wait()


def _s5_core(u5, mt, bt, ct, apow, layer):
    nchunk, nblk, _, batch, _ = u5.shape
    cols = nchunk * batch
    ng = S5_GPB
    op = lambda a: pl.BlockSpec((ng,) + a.shape[1:], lambda i: (layer * nblk + i, 0, 0))
    hbm = pl.BlockSpec(memory_space=pl.ANY)
    return pl.pallas_call(
        functools.partial(_s5_core_kernel, batch=batch),
        grid=(nblk,),
        in_specs=[hbm, op(mt), op(bt), op(ct), op(apow)],
        out_specs=hbm,
        out_shape=jax.ShapeDtypeStruct(u5.shape, F32),
        scratch_shapes=[pltpu.VMEM((2, nchunk, S5_CHUNK, batch, LANES), F32),
                        pltpu.VMEM((ng, S5_FLAT, cols), BF16),
                        pltpu.VMEM((ng, cols, 4 * S5_STATE), F32),
                        pltpu.VMEM((ng, cols, 2 * S5_STATE), F32),
                        pltpu.VMEM((ng, S5_FLAT, cols), BF16),
                        pltpu.SemaphoreType.DMA((2, S5_DMAS)),
                        pltpu.SemaphoreType.DMA((2, S5_DMAS))],
        compiler_params=_params("arbitrary"),
        name="s5_core",
    )(u5, mt, bt, ct, apow)


def _ssd_kernel(z_ref, xbc_ref, dt_ref, cw_ref, cb_ref, dtb_ref, alog_ref, dsk_ref,
                gn_ref, e64_ref, y_ref, tail_scr, xc_scr, st_scr):
    for i in range(z_ref.shape[0]):
        _ssd_batch(z_ref.at[i], xbc_ref.at[i], dt_ref.at[i], cw_ref, cb_ref, dtb_ref, alog_ref,
                   dsk_ref, gn_ref, e64_ref, y_ref.at[i], tail_scr.at[i], xc_scr.at[i],
                   st_scr.at[i])


def _ssd_batch(z_ref, xbc_ref, dt_ref, cw_ref, cb_ref, dtb_ref, alog_ref, dsk_ref,
               gn_ref, e64_ref, y_ref, tail_scr, xc_scr, st_scr):
    lb = z_ref.shape[0]
    j = pl.program_id(1)

    @pl.when(j == 0)
    def _():
        tail_scr[...] = jnp.zeros_like(tail_scr)
        st_scr[...] = jnp.zeros_like(st_scr)

    xb = xbc_ref[...]
    x0 = xb.astype(F32)
    rr = lax.broadcasted_iota(jnp.int32, (lb, lb), 0)
    cc = lax.broadcasted_iota(jnp.int32, (lb, lb), 1)
    row8 = lax.broadcasted_iota(jnp.int32, (SUBLANES, 1), 0)
    tail = tail_scr[...]
    acc = cb_ref[...] + cw_ref[SSD_CONV - 1:SSD_CONV, :] * x0
    head = jnp.zeros((SUBLANES, SSD_CONV_DIM), F32)
    for k in range(1, SSD_CONV):
        wk = cw_ref[SSD_CONV - 1 - k:SSD_CONV - k, :]
        acc = acc + wk * _dot((rr - cc == k).astype(BF16), xb)
        head = head + wk * jnp.where(row8 < k, pltpu.roll(tail, k, axis=0), 0.0)
    tail_scr[...] = x0[lb - SUBLANES:, :]
    xc_scr[...] = acc * jax.nn.sigmoid(acc)
    acc0 = acc[0:SUBLANES, :] + head
    xc_scr[0:SUBLANES, :] = acc0 * jax.nn.sigmoid(acc0)

    dt_t = (dt_ref[...] + dtb_ref[...]).T[0:SSD_HEADS, :]
    dtp_t = jnp.maximum(dt_t, 0.0) + jnp.log(1.0 + jnp.exp(-jnp.abs(dt_t)))
    dtp_all = jnp.concatenate(
        [dtp_t, jnp.zeros((LANES - SSD_HEADS, lb), F32)], axis=0).T

    ri = lax.broadcasted_iota(jnp.int32, (SSD_CHUNK, SSD_CHUNK), 0)
    ci = lax.broadcasted_iota(jnp.int32, (SSD_CHUNK, SSD_CHUNK), 1)
    causal = ri >= ci
    ltri = causal.astype(BF16)
    lo_half = ci < SSD_HEAD_DIM
    a_neg = -jnp.exp(alog_ref[...])
    e64 = e64_ref[...]
    b0 = SSD_WIDTH
    c0 = SSD_WIDTH + SSD_GROUPS * SSD_STATE
    hpg = SSD_HEADS // SSD_GROUPS

    for c in range(lb // SSD_CHUNK):
        r0 = c * SSD_CHUNK
        xs = xc_scr[r0:r0 + SSD_CHUNK, 0:SSD_WIDTH]
        dtp = dtp_all[r0:r0 + SSD_CHUNK, :]
        dta = dtp * a_neg
        acum = _dot_sel_lhs(ltri, dta)
        acum_t = acum.T
        atot = acum[SSD_CHUNK - 1:SSD_CHUNK, :]
        fac = jnp.concatenate([dtp, dtp * jnp.exp(atot - acum), jnp.exp(acum)], axis=0)
        fac64 = _dot(fac.astype(BF16), e64)
        xdt = xs * fac64[0:SSD_CHUNK]
        xw = (xs * fac64[SSD_CHUNK:2 * SSD_CHUNK]).astype(BF16)
        ea64 = fac64[2 * SSD_CHUNK:]
        tail = SSD_CHUNK - SUBLANES
        etot64 = jnp.exp(_dot_sel_rhs(acum[tail:, :], e64)[SUBLANES - 1:SUBLANES, :])
        gmats = []
        for g in range(SSD_GROUPS):
            bm = xc_scr[r0:r0 + SSD_CHUNK, b0 + g * SSD_STATE:b0 + (g + 1) * SSD_STATE]
            cm = xc_scr[r0:r0 + SSD_CHUNK, c0 + g * SSD_STATE:c0 + (g + 1) * SSD_STATE].astype(BF16)
            bm_t = bm.T.astype(BF16)
            gmats.append((bm_t, cm, _dot(cm, bm_t)))
        ys = []
        for p in range(SSD_HEADS // 2):
            bm_t, cm, gm = gmats[(2 * p) // hpg]
            cols = slice(p * LANES, (p + 1) * LANES)
            xp = xdt[:, cols]
            y = None
            for hh in range(2):
                h = 2 * p + hh
                col = jnp.broadcast_to(acum[:, h:h + 1], (SSD_CHUNK, SSD_CHUNK))
                rowv = acum_t[h:h + 1, :]
                w = jnp.where(causal, gm * jnp.exp(col - rowv), 0.0).astype(BF16)
                xh = jnp.where(lo_half if hh == 0 else ~lo_half, xp, 0.0).astype(BF16)
                t = _dot(w, xh)
                y = t if y is None else y + t
            st = st_scr[p]
            y = y + _dot(cm, st.astype(BF16)) * ea64[:, cols]
            st_scr[p] = st * etot64[:, cols] + _dot(bm_t, xw[:, cols])
            ys.append(y + dsk_ref[:, cols] * xs[:, cols])
        yc = jnp.concatenate(ys, axis=1)
        zc = z_ref[r0:r0 + SSD_CHUNK, :].astype(F32)
        yc = yc * (zc * jax.nn.sigmoid(zc))
        y_ref[r0:r0 + SSD_CHUNK, :] = _rms(yc, gn_ref[...]).astype(y_ref.dtype)


def _ssd(z3, xbc3, dt3, cw, cb, dtb, alog, dsk, gn, e64, layer, lb, nb):
    b, l, _ = z3.shape
    blk = lambda w: pl.BlockSpec((nb, lb, w), lambda i, j: (i, j, 0))
    ls = lambda a: _layer_spec(a.shape, layer, 2)
    return pl.pallas_call(
        _ssd_kernel,
        grid=(b // nb, l // lb),
        in_specs=[blk(SSD_WIDTH), blk(SSD_CONV_DIM), blk(LANES), ls(cw), ls(cb),
                  ls(dtb), ls(alog), ls(dsk), ls(gn),
                  pl.BlockSpec(e64.shape, lambda i, j: (0, 0))],
        out_specs=blk(SSD_WIDTH),
        out_shape=jax.ShapeDtypeStruct((b, l, SSD_WIDTH), BF16),
        scratch_shapes=[pltpu.VMEM((nb, SUBLANES, SSD_CONV_DIM), F32),
                        pltpu.VMEM((nb, lb, SSD_CONV_DIM), F32),
                        pltpu.VMEM((nb, SSD_HEADS // 2, SSD_STATE, LANES), F32)],
        compiler_params=_params("parallel", "arbitrary"),
        name="ssd",
    )(z3, xbc3, dt3, cw, cb, dtb, alog, dsk, gn, e64)


FFN_CHUNK = MXU_WIDTH


def _gelu_tanh(v):
    return 0.5 * v * (1.0 + jnp.tanh(0.7978845608028654 * (v + 0.044715 * (v * v * v))))


def _post_kernel(x_ref, ys_ref, u_ref, yb_ref, d5_ref, wglu_ref, bglu_ref, g5_ref,
                 woa_ref, wob_ref, gf_ref, wg_ref, wu_ref, wd_ref, gfin_ref, o_ref, y_scr,
                 *, final):
    bsz, tok, _ = x_ref.shape
    for b in range(bsz):
        for c in range(tok // S5_CHUNK):
            r0 = b * tok + c * S5_CHUNK
            for j in range(D_MODEL // LANES):
                cols = slice(j * LANES, (j + 1) * LANES)
                rows = _s5_rows(c, j, b)
                y_scr[r0:r0 + S5_CHUNK, cols] = ys_ref[rows, :] + d5_ref[:, cols] * u_ref[rows, :]
    g = _gelu_tanh(y_scr[...])
    ya = g * jax.nn.sigmoid(_dot(g.astype(BF16), wglu_ref[...]) + bglu_ref[...])
    ya = _rms(ya, g5_ref[...]).astype(BF16)
    yb = yb_ref[...].reshape(bsz * tok, SSD_WIDTH)
    x = x_ref[...].reshape(bsz * tok, D_MODEL) + _dot(ya, woa_ref[...]) + _dot(yb, wob_ref[...])
    hb = _rms(x, gf_ref[...]).astype(BF16)
    y_scr[...] = x

    def ffn_chunk(k, carry):
        sl = pl.ds(pl.multiple_of(k * FFN_CHUNK, FFN_CHUNK), FFN_CHUNK)
        gate = _dot(hb, wg_ref[:, sl])
        act = (gate * jax.nn.sigmoid(gate)) * _dot(hb, wu_ref[:, sl])
        y_scr[...] += _dot(act.astype(BF16), wd_ref[sl, :])
        return carry

    lax.fori_loop(0, FFN_HIDDEN // FFN_CHUNK, ffn_chunk, 0, unroll=4)
    out = y_scr[...]
    if final:
        out = _rms(out, gfin_ref[...])
    o_ref[...] = out.reshape(o_ref.shape)


def _post(x3, ys, u, yb3, d5, wglu, bglu, g5, wo, gf, wg, wu, wd, gfin, layer, tok, final):
    b, seq, _ = x3.shape
    nat = pl.BlockSpec((b, tok, D_MODEL), lambda i: (0, i, 0))
    s5 = pl.BlockSpec((tok * b * D_MODEL // LANES, LANES), lambda i: (i, 0))
    one = pl.Buffered(1)
    ls = lambda a: pl.BlockSpec((None,) + a.shape[1:], lambda i: (layer, 0, 0), pipeline_mode=one)
    wo_half = lambda k: pl.BlockSpec((None, D_MODEL, D_MODEL), lambda i: (layer, k, 0),
                                     pipeline_mode=one)
    gfin_spec = pl.BlockSpec(gfin.shape, lambda i: (0, 0), pipeline_mode=one)
    return pl.pallas_call(
        functools.partial(_post_kernel, final=final),
        grid=(seq // tok,),
        in_specs=[nat, s5, s5, nat, ls(d5), ls(wglu), ls(bglu), ls(g5), wo_half(0), wo_half(1),
                  ls(gf), ls(wg), ls(wu), ls(wd), gfin_spec],
        out_specs=nat,
        out_shape=jax.ShapeDtypeStruct(x3.shape, F32),
        scratch_shapes=[pltpu.VMEM((b * tok, D_MODEL), F32)],
        compiler_params=_params("parallel"),
        name="post",
    )(x3, ys, u, yb3, d5, wglu, bglu, g5, wo, wo, gf, wg, wu, wd, gfin)


def kernel(x, norm_mix, w_in, s5_lam_re, s5_lam_im, s5_log_step, s5_b_re, s5_b_im, s5_c_re, s5_c_im, s5_d, s5_w_glu, s5_b_glu, s5_norm, ssd_conv_w, ssd_conv_b, ssd_dt_bias, ssd_a_log, ssd_d, ssd_norm, w_out, norm_ffn, w_gate, w_up, w_down, norm_final):
    bsz, seq, _ = x.shape
    depth = w_in.shape[0]
    assert bsz == SUBLANES and seq % S5_CHUNK == 0 and seq % SSD_CHUNK == 0
    nc5 = seq // S5_CHUNK
    tok = 2 * S5_CHUNK
    lb = min(256, seq)
    dg = depth * S5_GROUPS
    rows = lambda a: a.reshape(depth, 1, -1).astype(F32)

    w_all = _cast_win(jnp.swapaxes(w_in, 1, 2), 256)
    wglu = _cast_bf16(s5_w_glu, 256)
    wo = _cast_bf16(w_out, 256)
    wg = _cast_bf16(w_gate, 256)
    wup = _cast_bf16(w_up, 256)
    wd = _cast_bf16(w_down, 256)

    dbl = lambda a: jnp.concatenate([a, a], axis=-1)
    mt, bt, ct, apow = _s5_prep(
        s5_log_step.reshape(dg, 1, 1),
        s5_lam_re.reshape(dg, S5_STATE, 1), s5_lam_im.reshape(dg, S5_STATE, 1),
        dbl(s5_lam_re).reshape(dg, 1, 2 * S5_STATE), dbl(s5_lam_im).reshape(dg, 1, 2 * S5_STATE),
        dbl(s5_c_re).reshape(dg, S5_GROUP, 2 * S5_STATE),
        dbl(s5_c_im).reshape(dg, S5_GROUP, 2 * S5_STATE),
        s5_b_re.reshape(dg, S5_STATE, S5_GROUP), s5_b_im.reshape(dg, S5_STATE, S5_GROUP))

    head_of_col = jnp.arange(SSD_WIDTH) // SSD_HEAD_DIM
    e64 = (jnp.arange(LANES)[:, None] == head_of_col[None, :]).astype(BF16)
    pad_h = lambda a: jnp.pad(a.astype(F32), ((0, 0), (0, LANES - SSD_HEADS))).reshape(
        depth, 1, LANES)
    dtb, alog = pad_h(ssd_dt_bias), pad_h(ssd_a_log)
    dsk = rows(jnp.repeat(ssd_d, SSD_HEAD_DIM, axis=-1))
    g_mix, g_s5, g_ssd, g_ffn = rows(norm_mix), rows(s5_norm), rows(ssd_norm), rows(norm_ffn)
    d5, bglu, cb = rows(s5_d), rows(s5_b_glu), rows(ssd_conv_b)
    cw = ssd_conv_w.astype(F32)
    gfin = norm_final.reshape(1, -1).astype(F32)

    s5_shape = (nc5, D_MODEL // LANES, S5_CHUNK, bsz, LANES)
    for i in range(depth):
        u, z, xbc, dt = _in_proj(x, g_mix, w_all, i, 2 * tok)
        ys = _s5_core(u.reshape(s5_shape), mt, bt, ct, apow, i).reshape(u.shape)
        yb = _ssd(z, xbc, dt, cw, cb, dtb, alog, dsk, g_ssd, e64, i, lb, 2)
        x = _post(x, ys, u, yb, d5, wglu, bglu, g_s5, wo, g_ffn, wg, wup, wd, gfin,
                  i, tok, i == depth - 1)
    return x
```

```python
import functools

import jax
import jax.numpy as jnp
from jax import lax
from jax.experimental import pallas as pl
from jax.experimental.pallas import tpu as pltpu

F32 = jnp.float32
BF16 = jnp.bfloat16
EPS = 1e-6

D_MODEL = 1024
S5_GROUP = 16
S5_GROUPS = 64
S5_STATE = 64
S5_CHUNK = 32
S5_FLAT = S5_CHUNK * S5_GROUP
SSD_HEAD_DIM = 64
SSD_HEADS = 16
SSD_GROUPS = 2
SSD_STATE = 128
SSD_CONV = 4
SSD_CHUNK = 128
SSD_WIDTH = SSD_HEADS * SSD_HEAD_DIM
SSD_CONV_DIM = SSD_WIDTH + 2 * SSD_GROUPS * SSD_STATE
FFN_HIDDEN = 2816
LANES = 128
SUBLANES = 8
MXU_WIDTH = 256
S5_GPB = LANES // S5_GROUP
S5_BUFS = 3
VMEM_LIMIT = 56 * 1024 * 1024
IN_U = D_MODEL
IN_Z = IN_U + SSD_WIDTH
IN_X = IN_Z + SSD_CONV_DIM


def _params(*sem):
    return pltpu.CompilerParams(dimension_semantics=sem, vmem_limit_bytes=VMEM_LIMIT)


def _rms(x, gain):
    return x * lax.rsqrt(jnp.mean(x * x, axis=-1, keepdims=True) + EPS) * gain


def _dot(a, b):
    return jnp.dot(a, b, preferred_element_type=F32)


def _split3(v):
    hi = v.astype(BF16)
    r1 = v - hi.astype(F32)
    mid = r1.astype(BF16)
    lo = (r1 - mid.astype(F32)).astype(BF16)
    return hi, mid, lo


def _dot_sel_rhs(v, sel):
    hi, mid, lo = _split3(v)
    return _dot(hi, sel) + _dot(mid, sel) + _dot(lo, sel)


def _dot_sel_lhs(sel, v):
    hi, mid, lo = _split3(v)
    return _dot(sel, hi) + _dot(sel, mid) + _dot(sel, lo)


def _dot_f32(a, b):
    a3 = _split3(a)
    b3 = _split3(b)
    acc = None
    for i in range(3):
        for j in range(3 - i):
            t = _dot(a3[i], b3[j])
            acc = t if acc is None else acc + t
    return acc


def _layer_spec(shape, layer, n_grid, col=0):
    zeros = (0,) * (len(shape) - 2)
    if n_grid == 1:
        return pl.BlockSpec((None,) + shape[1:], lambda i: (layer,) + zeros + (col,))
    return pl.BlockSpec((None,) + shape[1:], lambda i, j: (layer,) + zeros + (col,))


def _cast_kernel(w_ref, o_ref):
    o_ref[...] = w_ref[...].astype(o_ref.dtype)


def _cast_bf16(w, rb):
    depth, rows, cols = w.shape
    spec = pl.BlockSpec((1, rb, cols), lambda i, j: (i, j, 0))
    return pl.pallas_call(
        _cast_kernel, grid=(depth, rows // rb), in_specs=[spec], out_specs=spec,
        out_shape=jax.ShapeDtypeStruct(w.shape, BF16),
        compiler_params=_params("parallel", "parallel"), name="cast",
    )(w)


def _cast_win_kernel(w_ref, o_ref, *, n_out):
    rb = w_ref.shape[1]
    valid = n_out - pl.program_id(1) * rb
    row = lax.broadcasted_iota(jnp.int32, (rb, 1), 0)
    o_ref[0] = jnp.where(row < valid, w_ref[0], 0.0).T.astype(o_ref.dtype)


def _cast_win(w_in_t, rb):
    depth, n_out, d = w_in_t.shape
    nblk = pl.cdiv(n_out, rb)
    return pl.pallas_call(
        functools.partial(_cast_win_kernel, n_out=n_out), grid=(depth, nblk),
        in_specs=[pl.BlockSpec((1, rb, d), lambda i, j: (i, j, 0))],
        out_specs=pl.BlockSpec((1, d, rb), lambda i, j: (i, 0, j)),
        out_shape=jax.ShapeDtypeStruct((depth, d, nblk * rb), BF16),
        compiler_params=_params("parallel", "parallel"), name="cast_win",
    )(w_in_t)


def _s5_rows(c, j, b):
    return pl.ds(((c * (D_MODEL // LANES) + j) * S5_CHUNK) * SUBLANES + b, S5_CHUNK,
                 stride=SUBLANES)


def _inproj_kernel(x_ref, g_ref, wu_ref, wz_ref, wx0_ref, wx1_ref, wx2_ref, wdt_ref,
                   u_ref, z_ref, xbc_ref, dt_ref):
    bsz, tok, _ = x_ref.shape
    hb = _rms(x_ref[...], g_ref[...]).astype(BF16).reshape(bsz * tok, D_MODEL)
    u = _dot(hb, wu_ref[...])
    for b in range(bsz):
        for c in range(tok // S5_CHUNK):
            r0 = b * tok + c * S5_CHUNK
            for j in range(D_MODEL // LANES):
                u_ref[_s5_rows(c, j, b), :] = u[r0:r0 + S5_CHUNK, j * LANES:(j + 1) * LANES]
    z_ref[...] = _dot(hb, wz_ref[...]).astype(z_ref.dtype).reshape(z_ref.shape)
    xbc = jnp.concatenate([_dot(hb, w[...]).astype(xbc_ref.dtype)
                           for w in (wx0_ref, wx1_ref, wx2_ref)], axis=1)
    xbc_ref[...] = xbc.reshape(xbc_ref.shape)
    dt_ref[...] = _dot(hb, wdt_ref[...]).reshape(dt_ref.shape)


def _in_proj(x3, gain, w_all, layer, tok):
    b, seq, _ = x3.shape
    nat = lambda w: pl.BlockSpec((b, tok, w), lambda i: (0, i, 0))
    win = lambda width, start: pl.BlockSpec((None, D_MODEL, width),
                                            lambda i: (layer, 0, start // width))
    xw = SSD_CONV_DIM // 3
    return pl.pallas_call(
        _inproj_kernel,
        grid=(seq // tok,),
        in_specs=[nat(D_MODEL), _layer_spec(gain.shape, layer, 1), win(D_MODEL, 0),
                  win(SSD_WIDTH, IN_U), win(xw, IN_Z), win(xw, IN_Z + xw), win(xw, IN_Z + 2 * xw),
                  win(LANES, IN_X)],
        out_specs=[pl.BlockSpec((tok * b * D_MODEL // LANES, LANES), lambda i: (i, 0)),
                   nat(SSD_WIDTH), nat(SSD_CONV_DIM), nat(LANES)],
        out_shape=[jax.ShapeDtypeStruct((seq * b * D_MODEL // LANES, LANES), F32),
                   jax.ShapeDtypeStruct((b, seq, SSD_WIDTH), BF16),
                   jax.ShapeDtypeStruct((b, seq, SSD_CONV_DIM), BF16),
                   jax.ShapeDtypeStruct((b, seq, LANES), F32)],
        compiler_params=_params("parallel"),
        name="in_proj",
    )(x3, gain, *([w_all] * 6))


def _s5_prep_group(q, ls_ref, lrc_ref, lic_ref, lrr_ref, lir_ref, cre_ref, cim_ref,
                   bre_ref, bim_ref, mt_ref, bt_ref, ct_ref, apow_ref):
    step = jnp.exp(ls_ref[q])
    lrc = lrc_ref[q]
    lic = lic_ref[q]
    p2 = 2 * S5_STATE

    k_lane = lax.broadcasted_iota(jnp.int32, (1, LANES), 1).astype(F32)
    st = step * k_lane
    mag = jnp.exp(lrc * st)
    pre_c = mag * jnp.cos(lic * st)
    pim_c = mag * jnp.sin(lic * st)
    are = pre_c[:, 1:2]
    aim = pim_c[:, 1:2]
    den = lrc * lrc + lic * lic
    nr = are - 1.0
    cfr = (nr * lrc + aim * lic) / den
    cfi = (aim * lrc - nr * lic) / den
    bre = bre_ref[q]
    bim = bim_ref[q]
    bbr = cfr * bre - cfi * bim
    bbi = cfr * bim + cfi * bre

    lane = lax.broadcasted_iota(jnp.int32, (1, S5_FLAT), 1)
    h_of_lane = lax.broadcasted_iota(jnp.int32, (S5_GROUP, S5_FLAT), 1) % S5_GROUP
    e_h = (h_of_lane == lax.broadcasted_iota(jnp.int32, (S5_GROUP, S5_FLAT), 0)).astype(BF16)
    k_row = lax.broadcasted_iota(jnp.int32, (LANES, S5_FLAT), 0)
    j_lane = lax.broadcasted_iota(jnp.int32, (LANES, S5_FLAT), 1) // S5_GROUP
    e_rev = (k_row == (S5_CHUNK - 1) - j_lane).astype(BF16)
    btr = _dot_sel_rhs(bbr, e_h)
    bti = _dot_sel_rhs(bbi, e_h)
    prr = _dot_sel_rhs(pre_c, e_rev)
    pri = _dot_sel_rhs(pim_c, e_rev)
    wre = prr * btr - pri * bti
    wim = prr * bti + pri * btr
    bt_ref[q] = jnp.concatenate([wre, wim, wim, wre], axis=0).astype(bt_ref.dtype)

    lane2 = lax.broadcasted_iota(jnp.int32, (1, p2), 1)
    first = lane2 < S5_STATE
    c1 = cre_ref[q]
    c2 = cim_ref[q]
    lhs = jnp.where(first, c1, c2)
    rev = _dot_f32(lhs, jnp.concatenate([wre, -wim], axis=0))
    for t in range(S5_CHUNK):
        rows = slice(t * S5_GROUP, (t + 1) * S5_GROUP)
        shift = (S5_FLAT - (S5_CHUNK - 1 - t) * S5_GROUP) % S5_FLAT
        blk = rev if shift == 0 else pltpu.roll(rev, shift, axis=1)
        if t < S5_CHUNK - 1:
            blk = jnp.where(lane < (t + 1) * S5_GROUP, blk, 0.0)
        mt_ref[q, rows, :] = blk.astype(mt_ref.dtype)

    lrr = lrr_ref[q]
    lir = lir_ref[q]
    tau = (lax.broadcasted_iota(jnp.int32, (S5_CHUNK, 1), 0) + 1).astype(F32)
    st_r = step * tau
    mag_r = jnp.exp(lrr * st_r)
    pre_r = mag_r * jnp.cos(lir * st_r)
    pim_r = mag_r * jnp.sin(lir * st_r)
    q1 = jnp.where(first, pre_r, pim_r)
    q2 = jnp.where(first, pim_r, pre_r)
    s1 = jnp.where(first, 1.0, -1.0).astype(F32)
    for t in range(S5_CHUNK):
        rows = slice(t * S5_GROUP, (t + 1) * S5_GROUP)
        ct_ref[q, rows, :] = (s1 * (c1 * q1[t:t + 1, :]) - c2 * q2[t:t + 1, :]).astype(ct_ref.dtype)
    atr = pre_r[S5_CHUNK - 1:S5_CHUNK, :]
    a2 = -s1 * pim_r[S5_CHUNK - 1:S5_CHUNK, :]
    apow_ref[q] = jnp.concatenate(
        [atr, a2, -a2, jnp.zeros((SUBLANES - 3, p2), F32)], axis=0)


def _s5_prep_kernel(*refs):
    for q in range(refs[0].shape[0]):
        _s5_prep_group(q, *refs)


def _s5_prep(ls, lrc, lic, lrr, lir, cre, cim, bre, bim):
    dg = ls.shape[0]
    gps = 8
    spec = lambda a: pl.BlockSpec((gps,) + a.shape[1:], lambda i: (i, 0, 0))
    ins = (ls, lrc, lic, lrr, lir, cre, cim, bre, bim)
    shapes = [(dg, S5_FLAT, S5_FLAT), (dg, 4 * S5_STATE, S5_FLAT),
              (dg, S5_FLAT, 2 * S5_STATE), (dg, SUBLANES, 2 * S5_STATE)]
    dtypes = [BF16, BF16, BF16, F32]
    return pl.pallas_call(
        _s5_prep_kernel,
        grid=(dg // gps,),
        in_specs=[spec(a) for a in ins],
        out_specs=[pl.BlockSpec((gps,) + s[1:], lambda i: (i, 0, 0)) for s in shapes],
        out_shape=[jax.ShapeDtypeStruct(s, d) for s, d in zip(shapes, dtypes)],
        compiler_params=_params("parallel"),
        name="s5_prep",
    )(*ins)


def _s5_core_kernel(u_hbm, mt_ref, bt_ref, ct_ref, ap_ref, y_hbm,
                    io_buf, zt_scr, e_scr, sp_scr, yt_scr, sem_in, sem_out, *, batch):
    ng = mt_ref.shape[0]
    nchunk = u_hbm.shape[0]
    cols = nchunk * batch
    half = S5_FLAT // 2
    j = pl.program_id(0)
    last = pl.num_programs(0) - 1
    nbuf = io_buf.shape[0]
    slot = j % nbuf
    nxt = (j + 1) % nbuf

    def in_copy(step, s, t):
        return pltpu.make_async_copy(u_hbm.at[:, step, t], io_buf.at[s, t], sem_in.at[s, t])

    def out_copy(step, s, t):
        return pltpu.make_async_copy(io_buf.at[s, t], y_hbm.at[:, step, t], sem_out.at[s, t])

    @pl.when(j == 0)
    def _():
        for t in range(S5_CHUNK):
            in_copy(j, slot, t).start()

    for t in range(S5_CHUNK):
        in_copy(j, slot, t).wait()
    for t in range(S5_CHUNK):
        blk_t = io_buf[slot, t].reshape(cols, LANES).T.astype(BF16)
        for g in range(ng):
            zt_scr[g, t * S5_GROUP:(t + 1) * S5_GROUP, :] = blk_t[g * S5_GROUP:(g + 1) * S5_GROUP, :]

    @pl.when(jnp.logical_and(j >= nbuf - 1, j < last))
    def _():
        for t in range(S5_CHUNK):
            out_copy(j + 1 - nbuf, nxt, t).wait()

    @pl.when(j < last)
    def _():
        for t in range(S5_CHUNK):
            in_copy(j + 1, nxt, t).start()

    for g in range(ng):
        e_scr[g] = _dot(bt_ref[g], zt_scr[g]).T

    def body(c, carry):
        r = pl.multiple_of(c * batch, batch)
        out = []
        for g in range(ng):
            s, sw = carry[g]
            sp_scr[g, pl.ds(r, batch), :] = s
            e = e_scr[g, pl.ds(r, batch), :]
            a1 = ap_ref[g, 0:1, :]
            a2 = ap_ref[g, 1:2, :]
            a2w = ap_ref[g, 2:3, :]
            out.append((a1 * s + a2 * sw + e[:, :LANES], a1 * sw + a2w * s + e[:, LANES:]))
        return tuple(out)

    zero = jnp.zeros((batch, 2 * S5_STATE), F32)
    lax.fori_loop(0, nchunk, body, tuple((zero, zero) for _ in range(ng)))

    for g in range(ng):
        y_off = lax.dot_general(ct_ref[g], sp_scr[g].astype(BF16), (((1,), (1,)), ((), ())),
                                preferred_element_type=F32)
        top = _dot(mt_ref[g, :half, :half], zt_scr[g, :half, :])
        bot = _dot(mt_ref[g, half:, :], zt_scr[g])
        yt_scr[g, :half, :] = (top + y_off[:half]).astype(yt_scr.dtype)
        yt_scr[g, half:, :] = (bot + y_off[half:]).astype(yt_scr.dtype)

    for t in range(S5_CHUNK):
        rows = slice(t * S5_GROUP, (t + 1) * S5_GROUP)
        blk = jnp.concatenate([yt_scr[g, rows, :] for g in range(ng)], axis=0)
        io_buf[slot, t] = blk.astype(F32).T.reshape(nchunk, batch, LANES)
        out_copy(j, slot, t).start()

    @pl.when(j == last)
    def _():
        for back in range(nbuf):
            for t in range(S5_CHUNK):
                out_copy(j - back, (j - back) % nbuf, t).wait()


def _s5_core(u5, mt, bt, ct, apow, layer):
    nchunk, nblk, _, batch, _ = u5.shape
    cols = nchunk * batch
    ng = S5_GPB
    op = lambda a: pl.BlockSpec((ng,) + a.shape[1:], lambda i: (layer * nblk + i, 0, 0))
    hbm = pl.BlockSpec(memory_space=pl.ANY)
    return pl.pallas_call(
        functools.partial(_s5_core_kernel, batch=batch),
        grid=(nblk,),
        in_specs=[hbm, op(mt), op(bt), op(ct), op(apow)],
        out_specs=hbm,
        out_shape=jax.ShapeDtypeStruct(u5.shape, F32),
        scratch_shapes=[pltpu.VMEM((S5_BUFS, S5_CHUNK, nchunk, batch, LANES), F32),
                        pltpu.VMEM((ng, S5_FLAT, cols), BF16),
                        pltpu.VMEM((ng, cols, 4 * S5_STATE), F32),
                        pltpu.VMEM((ng, cols, 2 * S5_STATE), F32),
                        pltpu.VMEM((ng, S5_FLAT, cols), BF16),
                        pltpu.SemaphoreType.DMA((S5_BUFS, S5_CHUNK)),
                        pltpu.SemaphoreType.DMA((S5_BUFS, S5_CHUNK))],
        compiler_params=_params("arbitrary"),
        name="s5_core",
    )(u5, mt, bt, ct, apow)


def _ssd_kernel(z_ref, xbc_ref, dt_ref, cw_ref, cb_ref, dtb_ref, alog_ref, dsk_ref,
                gn_ref, e64_ref, y_ref, tail_scr, xc_scr, st_scr):
    for i in range(z_ref.shape[0]):
        _ssd_batch(z_ref.at[i], xbc_ref.at[i], dt_ref.at[i], cw_ref, cb_ref, dtb_ref, alog_ref,
                   dsk_ref, gn_ref, e64_ref, y_ref.at[i], tail_scr.at[i], xc_scr.at[i],
                   st_scr.at[i])


def _ssd_batch(z_ref, xbc_ref, dt_ref, cw_ref, cb_ref, dtb_ref, alog_ref, dsk_ref,
               gn_ref, e64_ref, y_ref, tail_scr, xc_scr, st_scr):
    lb = z_ref.shape[0]
    j = pl.program_id(1)

    @pl.when(j == 0)
    def _():
        tail_scr[...] = jnp.zeros_like(tail_scr)
        st_scr[...] = jnp.zeros_like(st_scr)

    xb = xbc_ref[...]
    x0 = xb.astype(F32)
    rr = lax.broadcasted_iota(jnp.int32, (lb, lb), 0)
    cc = lax.broadcasted_iota(jnp.int32, (lb, lb), 1)
    row8 = lax.broadcasted_iota(jnp.int32, (SUBLANES, 1), 0)
    tail = tail_scr[...]
    acc = cb_ref[...] + cw_ref[SSD_CONV - 1:SSD_CONV, :] * x0
    head = jnp.zeros((SUBLANES, SSD_CONV_DIM), F32)
    for k in range(1, SSD_CONV):
        wk = cw_ref[SSD_CONV - 1 - k:SSD_CONV - k, :]
        acc = acc + wk * _dot((rr - cc == k).astype(BF16), xb)
        head = head + wk * jnp.where(row8 < k, pltpu.roll(tail, k, axis=0), 0.0)
    tail_scr[...] = x0[lb - SUBLANES:, :]
    xc_scr[...] = acc * jax.nn.sigmoid(acc)
    acc0 = acc[0:SUBLANES, :] + head
    xc_scr[0:SUBLANES, :] = acc0 * jax.nn.sigmoid(acc0)

    dt_t = (dt_ref[...] + dtb_ref[...]).T[0:SSD_HEADS, :]
    dtp_t = jnp.maximum(dt_t, 0.0) + jnp.log(1.0 + jnp.exp(-jnp.abs(dt_t)))
    dtp_all = jnp.concatenate(
        [dtp_t, jnp.zeros((LANES - SSD_HEADS, lb), F32)], axis=0).T

    ri = lax.broadcasted_iota(jnp.int32, (SSD_CHUNK, SSD_CHUNK), 0)
    ci = lax.broadcasted_iota(jnp.int32, (SSD_CHUNK, SSD_CHUNK), 1)
    causal = ri >= ci
    ltri = causal.astype(BF16)
    lo_half = ci < SSD_HEAD_DIM
    a_neg = -jnp.exp(alog_ref[...])
    e64 = e64_ref[...]
    b0 = SSD_WIDTH
    c0 = SSD_WIDTH + SSD_GROUPS * SSD_STATE
    hpg = SSD_HEADS // SSD_GROUPS

    for c in range(lb // SSD_CHUNK):
        r0 = c * SSD_CHUNK
        xs = xc_scr[r0:r0 + SSD_CHUNK, 0:SSD_WIDTH]
        dtp = dtp_all[r0:r0 + SSD_CHUNK, :]
        dta = dtp * a_neg
        acum = _dot_sel_lhs(ltri, dta)
        acum_t = acum.T
        atot = acum[SSD_CHUNK - 1:SSD_CHUNK, :]
        fac = jnp.concatenate([dtp, dtp * jnp.exp(atot - acum), jnp.exp(acum)], axis=0)
        fac64 = _dot(fac.astype(BF16), e64)
        xdt = xs * fac64[0:SSD_CHUNK]
        xw = (xs * fac64[SSD_CHUNK:2 * SSD_CHUNK]).astype(BF16)
        ea64 = fac64[2 * SSD_CHUNK:]
        tail = SSD_CHUNK - SUBLANES
        etot64 = jnp.exp(_dot_sel_rhs(acum[tail:, :], e64)[SUBLANES - 1:SUBLANES, :])
        gmats = []
        for g in range(SSD_GROUPS):
            bm = xc_scr[r0:r0 + SSD_CHUNK, b0 + g * SSD_STATE:b0 + (g + 1) * SSD_STATE]
            cm = xc_scr[r0:r0 + SSD_CHUNK, c0 + g * SSD_STATE:c0 + (g + 1) * SSD_STATE].astype(BF16)
            bm_t = bm.T.astype(BF16)
            gmats.append((bm_t, cm, _dot(cm, bm_t)))
        ys = []
        for p in range(SSD_HEADS // 2):
            bm_t, cm, gm = gmats[(2 * p) // hpg]
            cols = slice(p * LANES, (p + 1) * LANES)
            xp = xdt[:, cols]
            y = None
            for hh in range(2):
                h = 2 * p + hh
                col = jnp.broadcast_to(acum[:, h:h + 1], (SSD_CHUNK, SSD_CHUNK))
                rowv = acum_t[h:h + 1, :]
                w = jnp.where(causal, gm * jnp.exp(col - rowv), 0.0).astype(BF16)
                xh = jnp.where(lo_half if hh == 0 else ~lo_half, xp, 0.0).astype(BF16)
                t = _dot(w, xh)
                y = t if y is None else y + t
            st = st_scr[p]
            y = y + _dot(cm, st.astype(BF16)) * ea64[:, cols]
            st_scr[p] = st * etot64[:, cols] + _dot(bm_t, xw[:, cols])
            ys.append(y + dsk_ref[:, cols] * xs[:, cols])
        yc = jnp.concatenate(ys, axis=1)
        zc = z_ref[r0:r0 + SSD_CHUNK, :].astype(F32)
        yc = yc * (zc * jax.nn.sigmoid(zc))
        y_ref[r0:r0 + SSD_CHUNK, :] = _rms(yc, gn_ref[...]).astype(y_ref.dtype)


def _ssd(z3, xbc3, dt3, cw, cb, dtb, alog, dsk, gn, e64, layer, lb, nb):
    b, l, _ = z3.shape
    blk = lambda w: pl.BlockSpec((nb, lb, w), lambda i, j: (i, j, 0))
    ls = lambda a: _layer_spec(a.shape, layer, 2)
    return pl.pallas_call(
        _ssd_kernel,
        grid=(b // nb, l // lb),
        in_specs=[blk(SSD_WIDTH), blk(SSD_CONV_DIM), blk(LANES), ls(cw), ls(cb),
                  ls(dtb), ls(alog), ls(dsk), ls(gn),
                  pl.BlockSpec(e64.shape, lambda i, j: (0, 0))],
        out_specs=blk(SSD_WIDTH),
        out_shape=jax.ShapeDtypeStruct((b, l, SSD_WIDTH), BF16),
        scratch_shapes=[pltpu.VMEM((nb, SUBLANES, SSD_CONV_DIM), F32),
                        pltpu.VMEM((nb, lb, SSD_CONV_DIM), F32),
                        pltpu.VMEM((nb, SSD_HEADS // 2, SSD_STATE, LANES), F32)],
        compiler_params=_params("parallel", "arbitrary"),
        name="ssd",
    )(z3, xbc3, dt3, cw, cb, dtb, alog, dsk, gn, e64)


FFN_CHUNK = MXU_WIDTH


def _gelu_tanh(v):
    return 0.5 * v * (1.0 + jnp.tanh(0.7978845608028654 * (v + 0.044715 * (v * v * v))))


def _post_kernel(x_ref, ys_ref, u_ref, yb_ref, d5_ref, wglu_ref, bglu_ref, g5_ref,
                 woa_ref, wob_ref, gf_ref, wg_ref, wu_ref, wd_ref, gfin_ref, o_ref, y_scr,
                 *, final):
    bsz, tok, _ = x_ref.shape
    for b in range(bsz):
        for c in range(tok // S5_CHUNK):
            r0 = b * tok + c * S5_CHUNK
            for j in range(D_MODEL // LANES):
                cols = slice(j * LANES, (j + 1) * LANES)
                rows = _s5_rows(c, j, b)
                y_scr[r0:r0 + S5_CHUNK, cols] = ys_ref[rows, :] + d5_ref[:, cols] * u_ref[rows, :]
    g = _gelu_tanh(y_scr[...])
    ya = g * jax.nn.sigmoid(_dot(g.astype(BF16), wglu_ref[...]) + bglu_ref[...])
    ya = _rms(ya, g5_ref[...]).astype(BF16)
    yb = yb_ref[...].reshape(bsz * tok, SSD_WIDTH)
    x = x_ref[...].reshape(bsz * tok, D_MODEL) + _dot(ya, woa_ref[...]) + _dot(yb, wob_ref[...])
    hb = _rms(x, gf_ref[...]).astype(BF16)
    y_scr[...] = x

    def ffn_chunk(k, carry):
        sl = pl.ds(pl.multiple_of(k * FFN_CHUNK, FFN_CHUNK), FFN_CHUNK)
        gate = _dot(hb, wg_ref[:, sl])
        act = (gate * jax.nn.sigmoid(gate)) * _dot(hb, wu_ref[:, sl])
        y_scr[...] += _dot(act.astype(BF16), wd_ref[sl, :])
        return carry

    lax.fori_loop(0, FFN_HIDDEN // FFN_CHUNK, ffn_chunk, 0, unroll=4)
    out = y_scr[...]
    if final:
        out = _rms(out, gfin_ref[...])
    o_ref[...] = out.reshape(o_ref.shape)


def _post(x3, ys, u, yb3, d5, wglu, bglu, g5, wo, gf, wg, wu, wd, gfin, layer, tok, final):
    b, seq, _ = x3.shape
    nat = pl.BlockSpec((b, tok, D_MODEL), lambda i: (0, i, 0))
    s5 = pl.BlockSpec((tok * b * D_MODEL // LANES, LANES), lambda i: (i, 0))
    one = pl.Buffered(1)
    ls = lambda a: pl.BlockSpec((None,) + a.shape[1:], lambda i: (layer, 0, 0), pipeline_mode=one)
    wo_half = lambda k: pl.BlockSpec((None, D_MODEL, D_MODEL), lambda i: (layer, k, 0),
                                     pipeline_mode=one)
    gfin_spec = pl.BlockSpec(gfin.shape, lambda i: (0, 0), pipeline_mode=one)
    return pl.pallas_call(
        functools.partial(_post_kernel, final=final),
        grid=(seq // tok,),
        in_specs=[nat, s5, s5, nat, ls(d5), ls(wglu), ls(bglu), ls(g5), wo_half(0), wo_half(1),
                  ls(gf), ls(wg), ls(wu), ls(wd), gfin_spec],
        out_specs=nat,
        out_shape=jax.ShapeDtypeStruct(x3.shape, F32),
        scratch_shapes=[pltpu.VMEM((b * tok, D_MODEL), F32)],
        compiler_params=_params("parallel"),
        name="post",
    )(x3, ys, u, yb3, d5, wglu, bglu, g5, wo, wo, gf, wg, wu, wd, gfin)


def kernel(x, norm_mix, w_in, s5_lam_re, s5_lam_im, s5_log_step, s5_b_re, s5_b_im, s5_c_re, s5_c_im, s5_d, s5_w_glu, s5_b_glu, s5_norm, ssd_conv_w, ssd_conv_b, ssd_dt_bias, ssd_a_log, ssd_d, ssd_norm, w_out, norm_ffn, w_gate, w_up, w_down, norm_final):
    bsz, seq, _ = x.shape
    depth = w_in.shape[0]
    assert bsz == SUBLANES and seq % S5_CHUNK == 0 and seq % SSD_CHUNK == 0
    nc5 = seq // S5_CHUNK
    tok = 2 * S5_CHUNK
    lb = min(256, seq)
    dg = depth * S5_GROUPS
    rows = lambda a: a.reshape(depth, 1, -1).astype(F32)

    w_all = _cast_win(jnp.swapaxes(w_in, 1, 2), 256)
    wglu = _cast_bf16(s5_w_glu, D_MODEL // 2)
    wo = _cast_bf16(w_out, D_MODEL)
    wg = _cast_bf16(w_gate, D_MODEL // 2)
    wup = _cast_bf16(w_up, D_MODEL // 2)
    wd = _cast_bf16(w_down, FFN_HIDDEN // 4)

    dbl = lambda a: jnp.concatenate([a, a], axis=-1)
    mt, bt, ct, apow = _s5_prep(
        s5_log_step.reshape(dg, 1, 1),
        s5_lam_re.reshape(dg, S5_STATE, 1), s5_lam_im.reshape(dg, S5_STATE, 1),
        dbl(s5_lam_re).reshape(dg, 1, 2 * S5_STATE), dbl(s5_lam_im).reshape(dg, 1, 2 * S5_STATE),
        dbl(s5_c_re).reshape(dg, S5_GROUP, 2 * S5_STATE),
        dbl(s5_c_im).reshape(dg, S5_GROUP, 2 * S5_STATE),
        s5_b_re.reshape(dg, S5_STATE, S5_GROUP), s5_b_im.reshape(dg, S5_STATE, S5_GROUP))

    head_of_col = jnp.arange(SSD_WIDTH) // SSD_HEAD_DIM
    e64 = (jnp.arange(LANES)[:, None] == head_of_col[None, :]).astype(BF16)
    pad_h = lambda a: jnp.pad(a.astype(F32), ((0, 0), (0, LANES - SSD_HEADS))).reshape(
        depth, 1, LANES)
    dtb, alog = pad_h(ssd_dt_bias), pad_h(ssd_a_log)
    dsk = rows(jnp.repeat(ssd_d, SSD_HEAD_DIM, axis=-1))
    g_mix, g_s5, g_ssd, g_ffn = rows(norm_mix), rows(s5_norm), rows(ssd_norm), rows(norm_ffn)
    d5, bglu, cb = rows(s5_d), rows(s5_b_glu), rows(ssd_conv_b)
    cw = ssd_conv_w.astype(F32)
    gfin = norm_final.reshape(1, -1).astype(F32)

    s5_shape = (nc5, D_MODEL // LANES, S5_CHUNK, bsz, LANES)
    for i in range(depth):
        u, z, xbc, dt = _in_proj(x, g_mix, w_all, i, 2 * tok)
        ys = _s5_core(u.reshape(s5_shape), mt, bt, ct, apow, i).reshape(u.shape)
        yb = _ssd(z, xbc, dt, cw, cb, dtb, alog, dsk, g_ssd, e64, i, lb, 2)
        x = _post(x, ys, u, yb, d5, wglu, bglu, g_s5, wo, g_ffn, wg, wup, wd, gfin,
                  i, tok, i == depth - 1)
    return x
```

```python
import functools

import jax
import jax.numpy as jnp
from jax import lax
from jax.experimental import pallas as pl
from jax.experimental.pallas import tpu as pltpu

F32 = jnp.float32
BF16 = jnp.bfloat16
EPS = 1e-6

D_MODEL = 1024
S5_GROUP = 16
S5_GROUPS = 64
S5_STATE = 64
S5_CHUNK = 32
S5_FLAT = S5_CHUNK * S5_GROUP
SSD_HEAD_DIM = 64
SSD_HEADS = 16
SSD_GROUPS = 2
SSD_STATE = 128
SSD_CONV = 4
SSD_CHUNK = 128
SSD_WIDTH = SSD_HEADS * SSD_HEAD_DIM
SSD_CONV_DIM = SSD_WIDTH + 2 * SSD_GROUPS * SSD_STATE
FFN_HIDDEN = 2816
LANES = 128
SUBLANES = 8
MXU_WIDTH = 256
S5_GPB = LANES // S5_GROUP
S5_BUFS = 3
VMEM_LIMIT = 56 * 1024 * 1024
IN_U = D_MODEL
IN_Z = IN_U + SSD_WIDTH
IN_X = IN_Z + SSD_CONV_DIM


def _params(*sem):
    return pltpu.CompilerParams(dimension_semantics=sem, vmem_limit_bytes=VMEM_LIMIT)


def _rms(x, gain):
    return x * lax.rsqrt(jnp.mean(x * x, axis=-1, keepdims=True) + EPS) * gain


def _dot(a, b):
    return jnp.dot(a, b, preferred_element_type=F32)


def _split(v, terms):
    out = []
    for _ in range(terms - 1):
        hi = v.astype(BF16)
        out.append(hi)
        v = v - hi.astype(F32)
    out.append(v.astype(BF16))
    return out


def _dot_sel_rhs(v, sel, terms=3):
    return functools.reduce(jnp.add, [_dot(t, sel) for t in _split(v, terms)])


def _dot_sel_lhs(sel, v, terms=3):
    return functools.reduce(jnp.add, [_dot(sel, t) for t in _split(v, terms)])


def _dot_f32(a, b, terms=3):
    a_t = _split(a, terms)
    b_t = _split(b, terms)
    return functools.reduce(
        jnp.add, [_dot(a_t[i], b_t[j]) for i in range(terms) for j in range(terms - i)])


def _layer_spec(shape, layer, n_grid, col=0):
    zeros = (0,) * (len(shape) - 2)
    if n_grid == 1:
        return pl.BlockSpec((None,) + shape[1:], lambda i: (layer,) + zeros + (col,))
    return pl.BlockSpec((None,) + shape[1:], lambda i, j: (layer,) + zeros + (col,))


def _cast_kernel(w_ref, o_ref):
    o_ref[...] = w_ref[...].astype(o_ref.dtype)


def _cast_bf16(w, rb):
    depth, rows, cols = w.shape
    spec = pl.BlockSpec((1, rb, cols), lambda i, j: (i, j, 0))
    return pl.pallas_call(
        _cast_kernel, grid=(depth, rows // rb), in_specs=[spec], out_specs=spec,
        out_shape=jax.ShapeDtypeStruct(w.shape, BF16),
        compiler_params=_params("parallel", "parallel"), name="cast",
    )(w)


def _cast_win_kernel(w_ref, o_ref, *, n_out):
    rb = w_ref.shape[1]
    valid = n_out - pl.program_id(1) * rb
    row = lax.broadcasted_iota(jnp.int32, (rb, 1), 0)
    o_ref[0] = jnp.where(row < valid, w_ref[0], 0.0).T.astype(o_ref.dtype)


def _cast_win(w_in_t, rb):
    depth, n_out, d = w_in_t.shape
    nblk = pl.cdiv(n_out, rb)
    return pl.pallas_call(
        functools.partial(_cast_win_kernel, n_out=n_out), grid=(depth, nblk),
        in_specs=[pl.BlockSpec((1, rb, d), lambda i, j: (i, j, 0))],
        out_specs=pl.BlockSpec((1, d, rb), lambda i, j: (i, 0, j)),
        out_shape=jax.ShapeDtypeStruct((depth, d, nblk * rb), BF16),
        compiler_params=_params("parallel", "parallel"), name="cast_win",
    )(w_in_t)


def _s5_rows(c, j, b):
    return pl.ds(((c * (D_MODEL // LANES) + j) * S5_CHUNK) * SUBLANES + b, S5_CHUNK,
                 stride=SUBLANES)


def _inproj_kernel(x_ref, g_ref, wu_ref, wz_ref, wx0_ref, wx1_ref, wx2_ref, wdt_ref,
                   u_ref, z_ref, xbc_ref, dt_ref):
    bsz, tok, _ = x_ref.shape
    hb = _rms(x_ref[...], g_ref[...]).astype(BF16).reshape(bsz * tok, D_MODEL)
    u = _dot(hb, wu_ref[...])
    for b in range(bsz):
        for c in range(tok // S5_CHUNK):
            r0 = b * tok + c * S5_CHUNK
            for j in range(D_MODEL // LANES):
                u_ref[_s5_rows(c, j, b), :] = u[r0:r0 + S5_CHUNK, j * LANES:(j + 1) * LANES]
    z_ref[...] = _dot(hb, wz_ref[...]).astype(z_ref.dtype).reshape(z_ref.shape)
    xbc = jnp.concatenate([_dot(hb, w[...]).astype(xbc_ref.dtype)
                           for w in (wx0_ref, wx1_ref, wx2_ref)], axis=1)
    xbc_ref[...] = xbc.reshape(xbc_ref.shape)
    dt_ref[...] = _dot(hb, wdt_ref[...]).reshape(dt_ref.shape)


def _in_proj(x3, gain, w_all, layer, tok):
    b, seq, _ = x3.shape
    nat = lambda w: pl.BlockSpec((b, tok, w), lambda i: (0, i, 0))
    win = lambda width, start: pl.BlockSpec((None, D_MODEL, width),
                                            lambda i: (layer, 0, start // width))
    xw = SSD_CONV_DIM // 3
    return pl.pallas_call(
        _inproj_kernel,
        grid=(seq // tok,),
        in_specs=[nat(D_MODEL), _layer_spec(gain.shape, layer, 1), win(D_MODEL, 0),
                  win(SSD_WIDTH, IN_U), win(xw, IN_Z), win(xw, IN_Z + xw), win(xw, IN_Z + 2 * xw),
                  win(LANES, IN_X)],
        out_specs=[pl.BlockSpec((tok * b * D_MODEL // LANES, LANES), lambda i: (i, 0)),
                   nat(SSD_WIDTH), nat(SSD_CONV_DIM), nat(LANES)],
        out_shape=[jax.ShapeDtypeStruct((seq * b * D_MODEL // LANES, LANES), F32),
                   jax.ShapeDtypeStruct((b, seq, SSD_WIDTH), BF16),
                   jax.ShapeDtypeStruct((b, seq, SSD_CONV_DIM), BF16),
                   jax.ShapeDtypeStruct((b, seq, LANES), F32)],
        compiler_params=_params("parallel"),
        name="in_proj",
    )(x3, gain, *([w_all] * 6))


def _s5_prep_group(q, ls_ref, lrc_ref, lic_ref, lrr_ref, lir_ref, cre_ref, cim_ref,
                   bre_ref, bim_ref, mt_ref, bt_ref, ct_ref, apow_ref):
    step = jnp.exp(ls_ref[q])
    lrc = lrc_ref[q]
    lic = lic_ref[q]
    p2 = 2 * S5_STATE

    k_lane = lax.broadcasted_iota(jnp.int32, (1, LANES), 1).astype(F32)
    st = step * k_lane
    mag = jnp.exp(lrc * st)
    pre_c = mag * jnp.cos(lic * st)
    pim_c = mag * jnp.sin(lic * st)
    are = pre_c[:, 1:2]
    aim = pim_c[:, 1:2]
    den = lrc * lrc + lic * lic
    nr = are - 1.0
    cfr = (nr * lrc + aim * lic) / den
    cfi = (aim * lrc - nr * lic) / den
    bre = bre_ref[q]
    bim = bim_ref[q]
    bbr = cfr * bre - cfi * bim
    bbi = cfr * bim + cfi * bre

    lane = lax.broadcasted_iota(jnp.int32, (1, S5_FLAT), 1)
    h_of_lane = lax.broadcasted_iota(jnp.int32, (S5_GROUP, S5_FLAT), 1) % S5_GROUP
    e_h = (h_of_lane == lax.broadcasted_iota(jnp.int32, (S5_GROUP, S5_FLAT), 0)).astype(BF16)
    k_row = lax.broadcasted_iota(jnp.int32, (LANES, S5_FLAT), 0)
    j_lane = lax.broadcasted_iota(jnp.int32, (LANES, S5_FLAT), 1) // S5_GROUP
    e_rev = (k_row == (S5_CHUNK - 1) - j_lane).astype(BF16)
    btr = _dot_sel_rhs(bbr, e_h, 2)
    bti = _dot_sel_rhs(bbi, e_h, 2)
    prr = _dot_sel_rhs(pre_c, e_rev, 2)
    pri = _dot_sel_rhs(pim_c, e_rev, 2)
    wre = prr * btr - pri * bti
    wim = prr * bti + pri * btr
    bt_ref[q] = jnp.concatenate([wre, wim, wim, wre], axis=0).astype(bt_ref.dtype)

    lane2 = lax.broadcasted_iota(jnp.int32, (1, p2), 1)
    first = lane2 < S5_STATE
    c1 = cre_ref[q]
    c2 = cim_ref[q]
    lhs = jnp.where(first, c1, c2)
    rev = _dot_f32(lhs, jnp.concatenate([wre, -wim], axis=0), 2)
    for t in range(S5_CHUNK):
        rows = slice(t * S5_GROUP, (t + 1) * S5_GROUP)
        shift = (S5_FLAT - (S5_CHUNK - 1 - t) * S5_GROUP) % S5_FLAT
        blk = rev if shift == 0 else pltpu.roll(rev, shift, axis=1)
        if t < S5_CHUNK - 1:
            blk = jnp.where(lane < (t + 1) * S5_GROUP, blk, 0.0)
        mt_ref[q, rows, :] = blk.astype(mt_ref.dtype)

    lrr = lrr_ref[q]
    lir = lir_ref[q]
    tau = (lax.broadcasted_iota(jnp.int32, (S5_CHUNK, 1), 0) + 1).astype(F32)
    st_r = step * tau
    mag_r = jnp.exp(lrr * st_r)
    pre_r = mag_r * jnp.cos(lir * st_r)
    pim_r = mag_r * jnp.sin(lir * st_r)
    q1 = jnp.where(first, pre_r, pim_r)
    q2 = jnp.where(first, pim_r, pre_r)
    s1 = jnp.where(first, 1.0, -1.0).astype(F32)
    for t in range(S5_CHUNK):
        rows = slice(t * S5_GROUP, (t + 1) * S5_GROUP)
        ct_ref[q, rows, :] = (s1 * (c1 * q1[t:t + 1, :]) - c2 * q2[t:t + 1, :]).astype(ct_ref.dtype)
    atr = pre_r[S5_CHUNK - 1:S5_CHUNK, :]
    a2 = -s1 * pim_r[S5_CHUNK - 1:S5_CHUNK, :]
    apow_ref[q] = jnp.concatenate(
        [atr, a2, -a2, jnp.zeros((SUBLANES - 3, p2), F32)], axis=0)


def _s5_prep_kernel(*refs):
    for q in range(refs[0].shape[0]):
        _s5_prep_group(q, *refs)


def _s5_prep(ls, lrc, lic, lrr, lir, cre, cim, bre, bim):
    dg = ls.shape[0]
    gps = 8
    spec = lambda a: pl.BlockSpec((gps,) + a.shape[1:], lambda i: (i, 0, 0))
    ins = (ls, lrc, lic, lrr, lir, cre, cim, bre, bim)
    shapes = [(dg, S5_FLAT, S5_FLAT), (dg, 4 * S5_STATE, S5_FLAT),
              (dg, S5_FLAT, 2 * S5_STATE), (dg, SUBLANES, 2 * S5_STATE)]
    dtypes = [BF16, BF16, BF16, F32]
    return pl.pallas_call(
        _s5_prep_kernel,
        grid=(dg // gps,),
        in_specs=[spec(a) for a in ins],
        out_specs=[pl.BlockSpec((gps,) + s[1:], lambda i: (i, 0, 0)) for s in shapes],
        out_shape=[jax.ShapeDtypeStruct(s, d) for s, d in zip(shapes, dtypes)],
        compiler_params=_params("parallel"),
        name="s5_prep",
    )(*ins)


def _s5_core_kernel(u_hbm, mt_ref, bt_ref, ct_ref, ap_ref, y_hbm,
                    io_buf, zt_scr, e_scr, sp_scr, yt_scr, sem_in, sem_out, *, batch):
    ng = mt_ref.shape[0]
    nchunk = u_hbm.shape[0]
    cols = nchunk * batch
    half = S5_FLAT // 2
    j = pl.program_id(0)
    last = pl.num_programs(0) - 1
    nbuf = io_buf.shape[0]
    slot = j % nbuf
    nxt = (j + 1) % nbuf

    def in_copy(step, s, t):
        return pltpu.make_async_copy(u_hbm.at[:, step, t], io_buf.at[s, t], sem_in.at[s, t])

    def out_copy(step, s, t):
        return pltpu.make_async_copy(io_buf.at[s, t], y_hbm.at[:, step, t], sem_out.at[s, t])

    @pl.when(j == 0)
    def _():
        for t in range(S5_CHUNK):
            in_copy(j, slot, t).start()

    for t in range(S5_CHUNK):
        in_copy(j, slot, t).wait()
    for t in range(S5_CHUNK):
        blk_t = io_buf[slot, t].reshape(cols, LANES).T.astype(BF16)
        for g in range(ng):
            zt_scr[g, t * S5_GROUP:(t + 1) * S5_GROUP, :] = blk_t[g * S5_GROUP:(g + 1) * S5_GROUP, :]

    @pl.when(jnp.logical_and(j >= nbuf - 1, j < last))
    def _():
        for t in range(S5_CHUNK):
            out_copy(j + 1 - nbuf, nxt, t).wait()

    @pl.when(j < last)
    def _():
        for t in range(S5_CHUNK):
            in_copy(j + 1, nxt, t).start()

    for g in range(ng):
        e_scr[g] = _dot(bt_ref[g], zt_scr[g]).T

    def body(c, carry):
        r = pl.multiple_of(c * batch, batch)
        out = []
        for g in range(ng):
            s, sw = carry[g]
            sp_scr[g, pl.ds(r, batch), :] = s
            e = e_scr[g, pl.ds(r, batch), :]
            a1 = ap_ref[g, 0:1, :]
            a2 = ap_ref[g, 1:2, :]
            a2w = ap_ref[g, 2:3, :]
            out.append((a1 * s + a2 * sw + e[:, :LANES], a1 * sw + a2w * s + e[:, LANES:]))
        return tuple(out)

    zero = jnp.zeros((batch, 2 * S5_STATE), F32)
    lax.fori_loop(0, nchunk, body, tuple((zero, zero) for _ in range(ng)))

    for g in range(ng):
        y_off = lax.dot_general(ct_ref[g], sp_scr[g].astype(BF16), (((1,), (1,)), ((), ())),
                                preferred_element_type=F32)
        top = _dot(mt_ref[g, :half, :half], zt_scr[g, :half, :])
        bot = _dot(mt_ref[g, half:, :], zt_scr[g])
        yt_scr[g, :half, :] = (top + y_off[:half]).astype(yt_scr.dtype)
        yt_scr[g, half:, :] = (bot + y_off[half:]).astype(yt_scr.dtype)

    for t in range(S5_CHUNK):
        rows = slice(t * S5_GROUP, (t + 1) * S5_GROUP)
        blk = jnp.concatenate([yt_scr[g, rows, :] for g in range(ng)], axis=0)
        io_buf[slot, t] = blk.astype(F32).T.reshape(nchunk, batch, LANES)
        out_copy(j, slot, t).start()

    @pl.when(j == last)
    def _():
        for back in range(nbuf):
            for t in range(S5_CHUNK):
                out_copy(j - back, (j - back) % nbuf, t).wait()


def _s5_core(u5, mt, bt, ct, apow, layer):
    nchunk, nblk, _, batch, _ = u5.shape
    cols = nchunk * batch
    ng = S5_GPB
    op = lambda a: pl.BlockSpec((ng,) + a.shape[1:], lambda i: (layer * nblk + i, 0, 0))
    hbm = pl.BlockSpec(memory_space=pl.ANY)
    return pl.pallas_call(
        functools.partial(_s5_core_kernel, batch=batch),
        grid=(nblk,),
        in_specs=[hbm, op(mt), op(bt), op(ct), op(apow)],
        out_specs=hbm,
        out_shape=jax.ShapeDtypeStruct(u5.shape, F32),
        scratch_shapes=[pltpu.VMEM((S5_BUFS, S5_CHUNK, nchunk, batch, LANES), F32),
                        pltpu.VMEM((ng, S5_FLAT, cols), BF16),
                        pltpu.VMEM((ng, cols, 4 * S5_STATE), F32),
                        pltpu.VMEM((ng, cols, 2 * S5_STATE), F32),
                        pltpu.VMEM((ng, S5_FLAT, cols), BF16),
                        pltpu.SemaphoreType.DMA((S5_BUFS, S5_CHUNK)),
                        pltpu.SemaphoreType.DMA((S5_BUFS, S5_CHUNK))],
        compiler_params=_params("arbitrary"),
        name="s5_core",
    )(u5, mt, bt, ct, apow)


def _ssd_kernel(z_ref, xbc_ref, dt_ref, cw_ref, cb_ref, dtb_ref, alog_ref, dsk_ref,
                gn_ref, e64_ref, y_ref, tail_scr, xc_scr, st_scr):
    for i in range(z_ref.shape[0]):
        _ssd_batch(z_ref.at[i], xbc_ref.at[i], dt_ref.at[i], cw_ref, cb_ref, dtb_ref, alog_ref,
                   dsk_ref, gn_ref, e64_ref, y_ref.at[i], tail_scr.at[i], xc_scr.at[i],
                   st_scr.at[i])


def _ssd_batch(z_ref, xbc_ref, dt_ref, cw_ref, cb_ref, dtb_ref, alog_ref, dsk_ref,
               gn_ref, e64_ref, y_ref, tail_scr, xc_scr, st_scr):
    lb = z_ref.shape[0]
    j = pl.program_id(1)

    @pl.when(j == 0)
    def _():
        tail_scr[...] = jnp.zeros_like(tail_scr)
        st_scr[...] = jnp.zeros_like(st_scr)

    xb = xbc_ref[...]
    x0 = xb.astype(F32)
    rr = lax.broadcasted_iota(jnp.int32, (lb, lb), 0)
    cc = lax.broadcasted_iota(jnp.int32, (lb, lb), 1)
    row8 = lax.broadcasted_iota(jnp.int32, (SUBLANES, 1), 0)
    tail = tail_scr[...]
    acc = cb_ref[...] + cw_ref[SSD_CONV - 1:SSD_CONV, :] * x0
    head = jnp.zeros((SUBLANES, SSD_CONV_DIM), F32)
    for k in range(1, SSD_CONV):
        wk = cw_ref[SSD_CONV - 1 - k:SSD_CONV - k, :]
        acc = acc + wk * _dot((rr - cc == k).astype(BF16), xb)
        head = head + wk * jnp.where(row8 < k, pltpu.roll(tail, k, axis=0), 0.0)
    tail_scr[...] = x0[lb - SUBLANES:, :]
    xc_scr[...] = acc * jax.nn.sigmoid(acc)
    acc0 = acc[0:SUBLANES, :] + head
    xc_scr[0:SUBLANES, :] = acc0 * jax.nn.sigmoid(acc0)

    dt_t = (dt_ref[...] + dtb_ref[...]).T[0:SSD_HEADS, :]
    dtp_t = jnp.maximum(dt_t, 0.0) + jnp.log(1.0 + jnp.exp(-jnp.abs(dt_t)))
    dtp_all = jnp.concatenate(
        [dtp_t, jnp.zeros((LANES - SSD_HEADS, lb), F32)], axis=0).T

    ri = lax.broadcasted_iota(jnp.int32, (SSD_CHUNK, SSD_CHUNK), 0)
    ci = lax.broadcasted_iota(jnp.int32, (SSD_CHUNK, SSD_CHUNK), 1)
    causal = ri >= ci
    ltri = causal.astype(BF16)
    lo_half = ci < SSD_HEAD_DIM
    a_neg = -jnp.exp(alog_ref[...])
    e64 = e64_ref[...]
    b0 = SSD_WIDTH
    c0 = SSD_WIDTH + SSD_GROUPS * SSD_STATE
    hpg = SSD_HEADS // SSD_GROUPS

    for c in range(lb // SSD_CHUNK):
        r0 = c * SSD_CHUNK
        xs = xc_scr[r0:r0 + SSD_CHUNK, 0:SSD_WIDTH]
        dtp = dtp_all[r0:r0 + SSD_CHUNK, :]
        dta = dtp * a_neg
        acum = _dot_sel_lhs(ltri, dta)
        acum_t = acum.T
        atot = acum[SSD_CHUNK - 1:SSD_CHUNK, :]
        fac = jnp.concatenate([dtp, dtp * jnp.exp(atot - acum), jnp.exp(acum)], axis=0)
        fac64 = _dot(fac.astype(BF16), e64)
        xdt = xs * fac64[0:SSD_CHUNK]
        xw = (xs * fac64[SSD_CHUNK:2 * SSD_CHUNK]).astype(BF16)
        ea64 = fac64[2 * SSD_CHUNK:]
        tail = SSD_CHUNK - SUBLANES
        etot64 = jnp.exp(_dot_sel_rhs(acum[tail:, :], e64)[SUBLANES - 1:SUBLANES, :])
        gmats = []
        for g in range(SSD_GROUPS):
            bm = xc_scr[r0:r0 + SSD_CHUNK, b0 + g * SSD_STATE:b0 + (g + 1) * SSD_STATE]
            cm = xc_scr[r0:r0 + SSD_CHUNK, c0 + g * SSD_STATE:c0 + (g + 1) * SSD_STATE].astype(BF16)
            bm_t = bm.T.astype(BF16)
            gmats.append((bm_t, cm, _dot(cm, bm_t)))
        ys = []
        for p in range(SSD_HEADS // 2):
            bm_t, cm, gm = gmats[(2 * p) // hpg]
            cols = slice(p * LANES, (p + 1) * LANES)
            xp = xdt[:, cols]
            y = None
            for hh in range(2):
                h = 2 * p + hh
                col = jnp.broadcast_to(acum[:, h:h + 1], (SSD_CHUNK, SSD_CHUNK))
                rowv = acum_t[h:h + 1, :]
                w = jnp.where(causal, gm * jnp.exp(col - rowv), 0.0).astype(BF16)
                xh = jnp.where(lo_half if hh == 0 else ~lo_half, xp, 0.0).astype(BF16)
                t = _dot(w, xh)
                y = t if y is None else y + t
            st = st_scr[p]
            y = y + _dot(cm, st.astype(BF16)) * ea64[:, cols]
            st_scr[p] = st * etot64[:, cols] + _dot(bm_t, xw[:, cols])
            ys.append(y + dsk_ref[:, cols] * xs[:, cols])
        yc = jnp.concatenate(ys, axis=1)
        zc = z_ref[r0:r0 + SSD_CHUNK, :].astype(F32)
        yc = yc * (zc * jax.nn.sigmoid(zc))
        y_ref[r0:r0 + SSD_CHUNK, :] = _rms(yc, gn_ref[...]).astype(y_ref.dtype)


def _ssd(z3, xbc3, dt3, cw, cb, dtb, alog, dsk, gn, e64, layer, lb, nb):
    b, l, _ = z3.shape
    blk = lambda w: pl.BlockSpec((nb, lb, w), lambda i, j: (i, j, 0))
    ls = lambda a: _layer_spec(a.shape, layer, 2)
    return pl.pallas_call(
        _ssd_kernel,
        grid=(b // nb, l // lb),
        in_specs=[blk(SSD_WIDTH), blk(SSD_CONV_DIM), blk(LANES), ls(cw), ls(cb),
                  ls(dtb), ls(alog), ls(dsk), ls(gn),
                  pl.BlockSpec(e64.shape, lambda i, j: (0, 0))],
        out_specs=blk(SSD_WIDTH),
        out_shape=jax.ShapeDtypeStruct((b, l, SSD_WIDTH), BF16),
        scratch_shapes=[pltpu.VMEM((nb, SUBLANES, SSD_CONV_DIM), F32),
                        pltpu.VMEM((nb, lb, SSD_CONV_DIM), F32),
                        pltpu.VMEM((nb, SSD_HEADS // 2, SSD_STATE, LANES), F32)],
        compiler_params=_params("parallel", "arbitrary"),
        name="ssd",
    )(z3, xbc3, dt3, cw, cb, dtb, alog, dsk, gn, e64)


FFN_CHUNK = MXU_WIDTH


def _gelu_tanh(v):
    return 0.5 * v * (1.0 + jnp.tanh(0.7978845608028654 * (v + 0.044715 * (v * v * v))))


def _post_kernel(x_ref, ys_ref, u_ref, yb_ref, d5_ref, wglu_ref, bglu_ref, g5_ref,
                 woa_ref, wob_ref, gf_ref, wg_ref, wu_ref, wd_ref, gfin_ref, o_ref, y_scr,
                 *, final):
    bsz, tok, _ = x_ref.shape
    for b in range(bsz):
        for c in range(tok // S5_CHUNK):
            r0 = b * tok + c * S5_CHUNK
            for j in range(D_MODEL // LANES):
                cols = slice(j * LANES, (j + 1) * LANES)
                rows = _s5_rows(c, j, b)
                y_scr[r0:r0 + S5_CHUNK, cols] = ys_ref[rows, :] + d5_ref[:, cols] * u_ref[rows, :]
    g = _gelu_tanh(y_scr[...])
    ya = g * jax.nn.sigmoid(_dot(g.astype(BF16), wglu_ref[...]) + bglu_ref[...])
    ya = _rms(ya, g5_ref[...]).astype(BF16)
    yb = yb_ref[...].reshape(bsz * tok, SSD_WIDTH)
    x = x_ref[...].reshape(bsz * tok, D_MODEL) + _dot(ya, woa_ref[...]) + _dot(yb, wob_ref[...])
    hb = _rms(x, gf_ref[...]).astype(BF16)
    y_scr[...] = x

    def ffn_chunk(k, carry):
        sl = pl.ds(pl.multiple_of(k * FFN_CHUNK, FFN_CHUNK), FFN_CHUNK)
        gate = _dot(hb, wg_ref[:, sl])
        act = (gate * jax.nn.sigmoid(gate)) * _dot(hb, wu_ref[:, sl])
        y_scr[...] += _dot(act.astype(BF16), wd_ref[sl, :])
        return carry

    lax.fori_loop(0, FFN_HIDDEN // FFN_CHUNK, ffn_chunk, 0, unroll=4)
    out = y_scr[...]
    if final:
        out = _rms(out, gfin_ref[...])
    o_ref[...] = out.reshape(o_ref.shape)


def _post(x3, ys, u, yb3, d5, wglu, bglu, g5, wo, gf, wg, wu, wd, gfin, layer, tok, final):
    b, seq, _ = x3.shape
    nat = pl.BlockSpec((b, tok, D_MODEL), lambda i: (0, i, 0))
    s5 = pl.BlockSpec((tok * b * D_MODEL // LANES, LANES), lambda i: (i, 0))
    one = pl.Buffered(1)
    ls = lambda a: pl.BlockSpec((None,) + a.shape[1:], lambda i: (layer, 0, 0), pipeline_mode=one)
    wo_half = lambda k: pl.BlockSpec((None, D_MODEL, D_MODEL), lambda i: (layer, k, 0),
                                     pipeline_mode=one)
    gfin_spec = pl.BlockSpec(gfin.shape, lambda i: (0, 0), pipeline_mode=one)
    return pl.pallas_call(
        functools.partial(_post_kernel, final=final),
        grid=(seq // tok,),
        in_specs=[nat, s5, s5, nat, ls(d5), ls(wglu), ls(bglu), ls(g5), wo_half(0), wo_half(1),
                  ls(gf), ls(wg), ls(wu), ls(wd), gfin_spec],
        out_specs=nat,
        out_shape=jax.ShapeDtypeStruct(x3.shape, F32),
        scratch_shapes=[pltpu.VMEM((b * tok, D_MODEL), F32)],
        compiler_params=_params("parallel"),
        name="post",
    )(x3, ys, u, yb3, d5, wglu, bglu, g5, wo, wo, gf, wg, wu, wd, gfin)


def kernel(x, norm_mix, w_in, s5_lam_re, s5_lam_im, s5_log_step, s5_b_re, s5_b_im, s5_c_re, s5_c_im, s5_d, s5_w_glu, s5_b_glu, s5_norm, ssd_conv_w, ssd_conv_b, ssd_dt_bias, ssd_a_log, ssd_d, ssd_norm, w_out, norm_ffn, w_gate, w_up, w_down, norm_final):
    bsz, seq, _ = x.shape
    depth = w_in.shape[0]
    assert bsz == SUBLANES and seq % S5_CHUNK == 0 and seq % SSD_CHUNK == 0
    nc5 = seq // S5_CHUNK
    tok = 2 * S5_CHUNK
    lb = min(256, seq)
    dg = depth * S5_GROUPS
    rows = lambda a: a.reshape(depth, 1, -1).astype(F32)

    w_all = _cast_win(jnp.swapaxes(w_in, 1, 2), 256)
    wglu = _cast_bf16(s5_w_glu, D_MODEL // 2)
    wo = _cast_bf16(w_out, D_MODEL)
    wg = _cast_bf16(w_gate, D_MODEL // 2)
    wup = _cast_bf16(w_up, D_MODEL // 2)
    wd = _cast_bf16(w_down, FFN_HIDDEN // 4)

    dbl = lambda a: jnp.concatenate([a, a], axis=-1)
    mt, bt, ct, apow = _s5_prep(
        s5_log_step.reshape(dg, 1, 1),
        s5_lam_re.reshape(dg, S5_STATE, 1), s5_lam_im.reshape(dg, S5_STATE, 1),
        dbl(s5_lam_re).reshape(dg, 1, 2 * S5_STATE), dbl(s5_lam_im).reshape(dg, 1, 2 * S5_STATE),
        dbl(s5_c_re).reshape(dg, S5_GROUP, 2 * S5_STATE),
        dbl(s5_c_im).reshape(dg, S5_GROUP, 2 * S5_STATE),
        s5_b_re.reshape(dg, S5_STATE, S5_GROUP), s5_b_im.reshape(dg, S5_STATE, S5_GROUP))

    head_of_col = jnp.arange(SSD_WIDTH) // SSD_HEAD_DIM
    e64 = (jnp.arange(LANES)[:, None] == head_of_col[None, :]).astype(BF16)
    pad_h = lambda a: jnp.pad(a.astype(F32), ((0, 0), (0, LANES - SSD_HEADS))).reshape(
        depth, 1, LANES)
    dtb, alog = pad_h(ssd_dt_bias), pad_h(ssd_a_log)
    dsk = rows(jnp.repeat(ssd_d, SSD_HEAD_DIM, axis=-1))
    g_mix, g_s5, g_ssd, g_ffn = rows(norm_mix), rows(s5_norm), rows(ssd_norm), rows(norm_ffn)
    d5, bglu, cb = rows(s5_d), rows(s5_b_glu), rows(ssd_conv_b)
    cw = ssd_conv_w.astype(F32)
    gfin = norm_final.reshape(1, -1).astype(F32)

    s5_shape = (nc5, D_MODEL // LANES, S5_CHUNK, bsz, LANES)
    for i in range(depth):
        u, z, xbc, dt = _in_proj(x, g_mix, w_all, i, 2 * tok)
        ys = _s5_core(u.reshape(s5_shape), mt, bt, ct, apow, i).reshape(u.shape)
        yb = _ssd(z, xbc, dt, cw, cb, dtb, alog, dsk, g_ssd, e64, i, lb, 4)
        x = _post(x, ys, u, yb, d5, wglu, bglu, g_s5, wo, g_ffn, wg, wup, wd, gfin,
                  i, tok, i == depth - 1)
    return x
```

```python
import functools

import jax
import jax.numpy as jnp
from jax import lax
from jax.experimental import pallas as pl
from jax.experimental.pallas import tpu as pltpu

F32 = jnp.float32
BF16 = jnp.bfloat16
EPS = 1e-6

D_MODEL = 1024
S5_GROUP = 16
S5_GROUPS = 64
S5_STATE = 64
S5_CHUNK = 32
S5_FLAT = S5_CHUNK * S5_GROUP
SSD_HEAD_DIM = 64
SSD_HEADS = 16
SSD_GROUPS = 2
SSD_STATE = 128
SSD_CONV = 4
SSD_CHUNK = 128
SSD_WIDTH = SSD_HEADS * SSD_HEAD_DIM
SSD_CONV_DIM = SSD_WIDTH + 2 * SSD_GROUPS * SSD_STATE
FFN_HIDDEN = 2816
LANES = 128
SUBLANES = 8
MXU_WIDTH = 256
S5_GPB = LANES // S5_GROUP
S5_BUFS = 3
VMEM_LIMIT = 56 * 1024 * 1024
IN_U = D_MODEL
IN_Z = IN_U + SSD_WIDTH
IN_X = IN_Z + SSD_CONV_DIM


def _params(*sem):
    return pltpu.CompilerParams(dimension_semantics=sem, vmem_limit_bytes=VMEM_LIMIT)


def _rms(x, gain):
    return x * lax.rsqrt(jnp.mean(x * x, axis=-1, keepdims=True) + EPS) * gain


def _dot(a, b):
    return jnp.dot(a, b, preferred_element_type=F32)


def _split(v, terms):
    out = []
    for _ in range(terms - 1):
        hi = v.astype(BF16)
        out.append(hi)
        v = v - hi.astype(F32)
    out.append(v.astype(BF16))
    return out


def _dot_sel_rhs(v, sel, terms=3):
    return functools.reduce(jnp.add, [_dot(t, sel) for t in _split(v, terms)])


def _dot_sel_lhs(sel, v, terms=3):
    return functools.reduce(jnp.add, [_dot(sel, t) for t in _split(v, terms)])


def _dot_f32(a, b, terms=3):
    a_t = _split(a, terms)
    b_t = _split(b, terms)
    return functools.reduce(
        jnp.add, [_dot(a_t[i], b_t[j]) for i in range(terms) for j in range(terms - i)])


def _layer_spec(shape, layer, n_grid, col=0):
    zeros = (0,) * (len(shape) - 2)
    if n_grid == 1:
        return pl.BlockSpec((None,) + shape[1:], lambda i: (layer,) + zeros + (col,))
    return pl.BlockSpec((None,) + shape[1:], lambda i, j: (layer,) + zeros + (col,))


def _cast_kernel(w_ref, o_ref):
    o_ref[...] = w_ref[...].astype(o_ref.dtype)


def _cast_bf16(w, rb):
    depth, rows, cols = w.shape
    spec = pl.BlockSpec((1, rb, cols), lambda i, j: (i, j, 0))
    return pl.pallas_call(
        _cast_kernel, grid=(depth, rows // rb), in_specs=[spec], out_specs=spec,
        out_shape=jax.ShapeDtypeStruct(w.shape, BF16),
        compiler_params=_params("parallel", "parallel"), name="cast",
    )(w)


def _cast_win_kernel(w_ref, o_ref, *, n_out):
    rb = w_ref.shape[1]
    valid = n_out - pl.program_id(1) * rb
    row = lax.broadcasted_iota(jnp.int32, (rb, 1), 0)
    o_ref[0] = jnp.where(row < valid, w_ref[0], 0.0).T.astype(o_ref.dtype)


def _cast_win(w_in_t, rb):
    depth, n_out, d = w_in_t.shape
    nblk = pl.cdiv(n_out, rb)
    return pl.pallas_call(
        functools.partial(_cast_win_kernel, n_out=n_out), grid=(depth, nblk),
        in_specs=[pl.BlockSpec((1, rb, d), lambda i, j: (i, j, 0))],
        out_specs=pl.BlockSpec((1, d, rb), lambda i, j: (i, 0, j)),
        out_shape=jax.ShapeDtypeStruct((depth, d, nblk * rb), BF16),
        compiler_params=_params("parallel", "parallel"), name="cast_win",
    )(w_in_t)


def _s5_rows(c, j, b):
    return pl.ds(((c * (D_MODEL // LANES) + j) * S5_CHUNK) * SUBLANES + b, S5_CHUNK,
                 stride=SUBLANES)


def _inproj_kernel(x_ref, g_ref, wu_ref, wz_ref, wx0_ref, wx1_ref, wx2_ref, wdt_ref,
                   u_ref, z_ref, xbc_ref, dt_ref):
    bsz, tok, _ = x_ref.shape
    hb = _rms(x_ref[...], g_ref[...]).astype(BF16).reshape(bsz * tok, D_MODEL)
    u = _dot(hb, wu_ref[...])
    for b in range(bsz):
        for c in range(tok // S5_CHUNK):
            r0 = b * tok + c * S5_CHUNK
            for j in range(D_MODEL // LANES):
                u_ref[_s5_rows(c, j, b), :] = u[r0:r0 + S5_CHUNK, j * LANES:(j + 1) * LANES]
    z_ref[...] = _dot(hb, wz_ref[...]).astype(z_ref.dtype).reshape(z_ref.shape)
    xbc = jnp.concatenate([_dot(hb, w[...]).astype(xbc_ref.dtype)
                           for w in (wx0_ref, wx1_ref, wx2_ref)], axis=1)
    xbc_ref[...] = xbc.reshape(xbc_ref.shape)
    dt_ref[...] = _dot(hb, wdt_ref[...]).reshape(dt_ref.shape)


def _in_proj(x3, gain, w_all, layer, tok):
    b, seq, _ = x3.shape
    nat = lambda w: pl.BlockSpec((b, tok, w), lambda i: (0, i, 0))
    win = lambda width, start: pl.BlockSpec((None, D_MODEL, width),
                                            lambda i: (layer, 0, start // width))
    xw = SSD_CONV_DIM // 3
    return pl.pallas_call(
        _inproj_kernel,
        grid=(seq // tok,),
        in_specs=[nat(D_MODEL), _layer_spec(gain.shape, layer, 1), win(D_MODEL, 0),
                  win(SSD_WIDTH, IN_U), win(xw, IN_Z), win(xw, IN_Z + xw), win(xw, IN_Z + 2 * xw),
                  win(LANES, IN_X)],
        out_specs=[pl.BlockSpec((tok * b * D_MODEL // LANES, LANES), lambda i: (i, 0)),
                   nat(SSD_WIDTH), nat(SSD_CONV_DIM), nat(LANES)],
        out_shape=[jax.ShapeDtypeStruct((seq * b * D_MODEL // LANES, LANES), F32),
                   jax.ShapeDtypeStruct((b, seq, SSD_WIDTH), BF16),
                   jax.ShapeDtypeStruct((b, seq, SSD_CONV_DIM), BF16),
                   jax.ShapeDtypeStruct((b, seq, LANES), F32)],
        compiler_params=_params("parallel"),
        name="in_proj",
    )(x3, gain, *([w_all] * 6))


def _s5_prep_group(q, ls_ref, lrc_ref, lic_ref, lrr_ref, lir_ref, cre_ref, cim_ref,
                   bre_ref, bim_ref, mt_ref, bt_ref, ct_ref, apow_ref):
    step = jnp.exp(ls_ref[q])
    lrc = lrc_ref[q]
    lic = lic_ref[q]
    p2 = 2 * S5_STATE

    k_lane = lax.broadcasted_iota(jnp.int32, (1, LANES), 1).astype(F32)
    st = step * k_lane
    mag = jnp.exp(lrc * st)
    pre_c = mag * jnp.cos(lic * st)
    pim_c = mag * jnp.sin(lic * st)
    are = pre_c[:, 1:2]
    aim = pim_c[:, 1:2]
    den = lrc * lrc + lic * lic
    nr = are - 1.0
    cfr = (nr * lrc + aim * lic) / den
    cfi = (aim * lrc - nr * lic) / den
    bre = bre_ref[q]
    bim = bim_ref[q]
    bbr = cfr * bre - cfi * bim
    bbi = cfr * bim + cfi * bre

    lane = lax.broadcasted_iota(jnp.int32, (1, S5_FLAT), 1)
    h_of_lane = lax.broadcasted_iota(jnp.int32, (S5_GROUP, S5_FLAT), 1) % S5_GROUP
    e_h = (h_of_lane == lax.broadcasted_iota(jnp.int32, (S5_GROUP, S5_FLAT), 0)).astype(BF16)
    k_row = lax.broadcasted_iota(jnp.int32, (LANES, S5_FLAT), 0)
    j_lane = lax.broadcasted_iota(jnp.int32, (LANES, S5_FLAT), 1) // S5_GROUP
    e_rev = (k_row == (S5_CHUNK - 1) - j_lane).astype(BF16)
    btr = _dot_sel_rhs(bbr, e_h, 2)
    bti = _dot_sel_rhs(bbi, e_h, 2)
    prr = _dot_sel_rhs(pre_c, e_rev, 2)
    pri = _dot_sel_rhs(pim_c, e_rev, 2)
    wre = prr * btr - pri * bti
    wim = prr * bti + pri * btr
    bt_ref[q] = jnp.concatenate([wre, wim, wim, wre], axis=0).astype(bt_ref.dtype)

    lane2 = lax.broadcasted_iota(jnp.int32, (1, p2), 1)
    first = lane2 < S5_STATE
    c1 = cre_ref[q]
    c2 = cim_ref[q]
    lhs = jnp.where(first, c1, c2)
    rev = _dot_f32(lhs, jnp.concatenate([wre, -wim], axis=0), 2)
    for t in range(S5_CHUNK):
        rows = slice(t * S5_GROUP, (t + 1) * S5_GROUP)
        shift = (S5_FLAT - (S5_CHUNK - 1 - t) * S5_GROUP) % S5_FLAT
        blk = rev if shift == 0 else pltpu.roll(rev, shift, axis=1)
        if t < S5_CHUNK - 1:
            blk = jnp.where(lane < (t + 1) * S5_GROUP, blk, 0.0)
        mt_ref[q, rows, :] = blk.astype(mt_ref.dtype)

    lrr = lrr_ref[q]
    lir = lir_ref[q]
    tau = (lax.broadcasted_iota(jnp.int32, (S5_CHUNK, 1), 0) + 1).astype(F32)
    st_r = step * tau
    mag_r = jnp.exp(lrr * st_r)
    pre_r = mag_r * jnp.cos(lir * st_r)
    pim_r = mag_r * jnp.sin(lir * st_r)
    q1 = jnp.where(first, pre_r, pim_r)
    q2 = jnp.where(first, pim_r, pre_r)
    s1 = jnp.where(first, 1.0, -1.0).astype(F32)
    for t in range(S5_CHUNK):
        rows = slice(t * S5_GROUP, (t + 1) * S5_GROUP)
        ct_ref[q, rows, :] = (s1 * (c1 * q1[t:t + 1, :]) - c2 * q2[t:t + 1, :]).astype(ct_ref.dtype)
    atr = pre_r[S5_CHUNK - 1:S5_CHUNK, :]
    a2 = -s1 * pim_r[S5_CHUNK - 1:S5_CHUNK, :]
    apow_ref[q] = jnp.concatenate(
        [atr, a2, -a2, jnp.zeros((SUBLANES - 3, p2), F32)], axis=0)


def _s5_prep_kernel(*refs):
    for q in range(refs[0].shape[0]):
        _s5_prep_group(q, *refs)


def _s5_prep(ls, lrc, lic, lrr, lir, cre, cim, bre, bim):
    dg = ls.shape[0]
    gps = 8
    spec = lambda a: pl.BlockSpec((gps,) + a.shape[1:], lambda i: (i, 0, 0))
    ins = (ls, lrc, lic, lrr, lir, cre, cim, bre, bim)
    shapes = [(dg, S5_FLAT, S5_FLAT), (dg, 4 * S5_STATE, S5_FLAT),
              (dg, S5_FLAT, 2 * S5_STATE), (dg, SUBLANES, 2 * S5_STATE)]
    dtypes = [BF16, BF16, BF16, F32]
    return pl.pallas_call(
        _s5_prep_kernel,
        grid=(dg // gps,),
        in_specs=[spec(a) for a in ins],
        out_specs=[pl.BlockSpec((gps,) + s[1:], lambda i: (i, 0, 0)) for s in shapes],
        out_shape=[jax.ShapeDtypeStruct(s, d) for s, d in zip(shapes, dtypes)],
        compiler_params=_params("parallel"),
        name="s5_prep",
    )(*ins)


def _s5_core_kernel(u_hbm, mt_ref, bt_ref, ct_ref, ap_ref, y_hbm,
                    io_buf, zt_scr, e_scr, sp_scr, yt_scr, sem_in, sem_out, *, batch):
    ng = mt_ref.shape[0]
    nchunk = u_hbm.shape[0]
    cols = nchunk * batch
    half = S5_FLAT // 2
    j = pl.program_id(0)
    last = pl.num_programs(0) - 1
    nbuf = io_buf.shape[0]
    slot = j % nbuf
    nxt = (j + 1) % nbuf

    def in_copy(step, s, t):
        return pltpu.make_async_copy(u_hbm.at[:, step, t], io_buf.at[s, t], sem_in.at[s, t])

    def out_copy(step, s, t):
        return pltpu.make_async_copy(io_buf.at[s, t], y_hbm.at[:, step, t], sem_out.at[s, t])

    @pl.when(j == 0)
    def _():
        for t in range(S5_CHUNK):
            in_copy(j, slot, t).start()

    @pl.when(jnp.logical_and(j >= nbuf - 1, j < last))
    def _():
        for t in range(S5_CHUNK):
            out_copy(j + 1 - nbuf, nxt, t).wait()

    @pl.when(j < last)
    def _():
        for t in range(S5_CHUNK):
            in_copy(j + 1, nxt, t).start()

    for t in range(S5_CHUNK):
        in_copy(j, slot, t).wait()
    for t in range(S5_CHUNK):
        blk_t = io_buf[slot, t].reshape(cols, LANES).T.astype(BF16)
        for g in range(ng):
            zt_scr[g, t * S5_GROUP:(t + 1) * S5_GROUP, :] = blk_t[g * S5_GROUP:(g + 1) * S5_GROUP, :]

    for g in range(ng):
        e_scr[g] = _dot(bt_ref[g], zt_scr[g]).T

    def body(c, carry):
        r = pl.multiple_of(c * batch, batch)
        out = []
        for g in range(ng):
            s, sw = carry[g]
            sp_scr[g, pl.ds(r, batch), :] = s
            e = e_scr[g, pl.ds(r, batch), :]
            a1 = ap_ref[g, 0:1, :]
            a2 = ap_ref[g, 1:2, :]
            a2w = ap_ref[g, 2:3, :]
            out.append((a1 * s + a2 * sw + e[:, :LANES], a1 * sw + a2w * s + e[:, LANES:]))
        return tuple(out)

    zero = jnp.zeros((batch, 2 * S5_STATE), F32)
    lax.fori_loop(0, nchunk, body, tuple((zero, zero) for _ in range(ng)))

    for g in range(ng):
        y_off = lax.dot_general(ct_ref[g], sp_scr[g].astype(BF16), (((1,), (1,)), ((), ())),
                                preferred_element_type=F32)
        top = _dot(mt_ref[g, :half, :half], zt_scr[g, :half, :])
        bot = _dot(mt_ref[g, half:, :], zt_scr[g])
        yt_scr[g, :half, :] = (top + y_off[:half]).astype(yt_scr.dtype)
        yt_scr[g, half:, :] = (bot + y_off[half:]).astype(yt_scr.dtype)

    for t in range(S5_CHUNK):
        rows = slice(t * S5_GROUP, (t + 1) * S5_GROUP)
        blk = jnp.concatenate([yt_scr[g, rows, :] for g in range(ng)], axis=0)
        io_buf[slot, t] = blk.astype(F32).T.reshape(nchunk, batch, LANES)
        out_copy(j, slot, t).start()

    @pl.when(j == last)
    def _():
        for back in range(nbuf):
            for t in range(S5_CHUNK):
                out_copy(j - back, (j - back) % nbuf, t).wait()


def _s5_core(u5, mt, bt, ct, apow, layer):
    nchunk, nblk, _, batch, _ = u5.shape
    cols = nchunk * batch
    ng = S5_GPB
    op = lambda a: pl.BlockSpec((ng,) + a.shape[1:], lambda i: (layer * nblk + i, 0, 0))
    hbm = pl.BlockSpec(memory_space=pl.ANY)
    return pl.pallas_call(
        functools.partial(_s5_core_kernel, batch=batch),
        grid=(nblk,),
        in_specs=[hbm, op(mt), op(bt), op(ct), op(apow)],
        out_specs=hbm,
        out_shape=jax.ShapeDtypeStruct(u5.shape, F32),
        scratch_shapes=[pltpu.VMEM((S5_BUFS, S5_CHUNK, nchunk, batch, LANES), F32),
                        pltpu.VMEM((ng, S5_FLAT, cols), BF16),
                        pltpu.VMEM((ng, cols, 4 * S5_STATE), F32),
                        pltpu.VMEM((ng, cols, 2 * S5_STATE), F32),
                        pltpu.VMEM((ng, S5_FLAT, cols), BF16),
                        pltpu.SemaphoreType.DMA((S5_BUFS, S5_CHUNK)),
                        pltpu.SemaphoreType.DMA((S5_BUFS, S5_CHUNK))],
        compiler_params=_params("arbitrary"),
        name="s5_core",
    )(u5, mt, bt, ct, apow)


def _ssd_kernel(z_ref, xbc_ref, dt_ref, cw_ref, cb_ref, dtb_ref, alog_ref, dsk_ref,
                gn_ref, e64_ref, y_ref, tail_scr, xc_scr, st_scr):
    for i in range(z_ref.shape[0]):
        _ssd_batch(z_ref.at[i], xbc_ref.at[i], dt_ref.at[i], cw_ref, cb_ref, dtb_ref, alog_ref,
                   dsk_ref, gn_ref, e64_ref, y_ref.at[i], tail_scr.at[i], xc_scr.at[i],
                   st_scr.at[i])


def _ssd_batch(z_ref, xbc_ref, dt_ref, cw_ref, cb_ref, dtb_ref, alog_ref, dsk_ref,
               gn_ref, e64_ref, y_ref, tail_scr, xc_scr, st_scr):
    lb = z_ref.shape[0]
    j = pl.program_id(1)

    @pl.when(j == 0)
    def _():
        tail_scr[...] = jnp.zeros_like(tail_scr)
        st_scr[...] = jnp.zeros_like(st_scr)

    xb = xbc_ref[...]
    x0 = xb.astype(F32)
    rr = lax.broadcasted_iota(jnp.int32, (lb, lb), 0)
    cc = lax.broadcasted_iota(jnp.int32, (lb, lb), 1)
    row8 = lax.broadcasted_iota(jnp.int32, (SUBLANES, 1), 0)
    tail = tail_scr[...]
    acc = cb_ref[...] + cw_ref[SSD_CONV - 1:SSD_CONV, :] * x0
    head = jnp.zeros((SUBLANES, SSD_CONV_DIM), F32)
    for k in range(1, SSD_CONV):
        wk = cw_ref[SSD_CONV - 1 - k:SSD_CONV - k, :]
        acc = acc + wk * _dot((rr - cc == k).astype(BF16), xb)
        head = head + wk * jnp.where(row8 < k, pltpu.roll(tail, k, axis=0), 0.0)
    tail_scr[...] = x0[lb - SUBLANES:, :]
    xc_scr[...] = acc * jax.nn.sigmoid(acc)
    acc0 = acc[0:SUBLANES, :] + head
    xc_scr[0:SUBLANES, :] = acc0 * jax.nn.sigmoid(acc0)

    dt_t = (dt_ref[...] + dtb_ref[...]).T[0:SSD_HEADS, :]
    dtp_t = jnp.maximum(dt_t, 0.0) + jnp.log(1.0 + jnp.exp(-jnp.abs(dt_t)))
    dtp_all = jnp.concatenate(
        [dtp_t, jnp.zeros((LANES - SSD_HEADS, lb), F32)], axis=0).T

    ri = lax.broadcasted_iota(jnp.int32, (SSD_CHUNK, SSD_CHUNK), 0)
    ci = lax.broadcasted_iota(jnp.int32, (SSD_CHUNK, SSD_CHUNK), 1)
    causal = ri >= ci
    ltri = causal.astype(BF16)
    lo_half = ci < SSD_HEAD_DIM
    a_neg = -jnp.exp(alog_ref[...])
    e64 = e64_ref[...]
    b0 = SSD_WIDTH
    c0 = SSD_WIDTH + SSD_GROUPS * SSD_STATE
    hpg = SSD_HEADS // SSD_GROUPS

    for c in range(lb // SSD_CHUNK):
        r0 = c * SSD_CHUNK
        xs = xc_scr[r0:r0 + SSD_CHUNK, 0:SSD_WIDTH]
        dtp = dtp_all[r0:r0 + SSD_CHUNK, :]
        dta = dtp * a_neg
        acum = _dot_sel_lhs(ltri, dta)
        acum_t = acum.T
        atot = acum[SSD_CHUNK - 1:SSD_CHUNK, :]
        fac = jnp.concatenate([dtp, dtp * jnp.exp(atot - acum), jnp.exp(acum)], axis=0)
        fac64 = _dot(fac.astype(BF16), e64)
        xdt = xs * fac64[0:SSD_CHUNK]
        xw = (xs * fac64[SSD_CHUNK:2 * SSD_CHUNK]).astype(BF16)
        ea64 = fac64[2 * SSD_CHUNK:]
        tail = SSD_CHUNK - SUBLANES
        etot64 = jnp.exp(_dot_sel_rhs(acum[tail:, :], e64)[SUBLANES - 1:SUBLANES, :])
        gmats = []
        for g in range(SSD_GROUPS):
            bm = xc_scr[r0:r0 + SSD_CHUNK, b0 + g * SSD_STATE:b0 + (g + 1) * SSD_STATE]
            cm = xc_scr[r0:r0 + SSD_CHUNK, c0 + g * SSD_STATE:c0 + (g + 1) * SSD_STATE].astype(BF16)
            bm_t = bm.T.astype(BF16)
            gmats.append((bm_t, cm, _dot(cm, bm_t)))
        ys = []
        for p in range(SSD_HEADS // 2):
            bm_t, cm, gm = gmats[(2 * p) // hpg]
            cols = slice(p * LANES, (p + 1) * LANES)
            xp = xdt[:, cols]
            y = None
            for hh in range(2):
                h = 2 * p + hh
                col = jnp.broadcast_to(acum[:, h:h + 1], (SSD_CHUNK, SSD_CHUNK))
                rowv = acum_t[h:h + 1, :]
                w = jnp.where(causal, gm * jnp.exp(col - rowv), 0.0).astype(BF16)
                xh = jnp.where(lo_half if hh == 0 else ~lo_half, xp, 0.0).astype(BF16)
                t = _dot(w, xh)
                y = t if y is None else y + t
            st = st_scr[p]
            y = y + _dot(cm, st.astype(BF16)) * ea64[:, cols]
            st_scr[p] = st * etot64[:, cols] + _dot(bm_t, xw[:, cols])
            ys.append(y + dsk_ref[:, cols] * xs[:, cols])
        yc = jnp.concatenate(ys, axis=1)
        zc = z_ref[r0:r0 + SSD_CHUNK, :].astype(F32)
        yc = yc * (zc * jax.nn.sigmoid(zc))
        y_ref[r0:r0 + SSD_CHUNK, :] = _rms(yc, gn_ref[...]).astype(y_ref.dtype)


def _ssd(z3, xbc3, dt3, cw, cb, dtb, alog, dsk, gn, e64, layer, lb, nb):
    b, l, _ = z3.shape
    blk = lambda w: pl.BlockSpec((nb, lb, w), lambda i, j: (i, j, 0))
    ls = lambda a: _layer_spec(a.shape, layer, 2)
    return pl.pallas_call(
        _ssd_kernel,
        grid=(b // nb, l // lb),
        in_specs=[blk(SSD_WIDTH), blk(SSD_CONV_DIM), blk(LANES), ls(cw), ls(cb),
                  ls(dtb), ls(alog), ls(dsk), ls(gn),
                  pl.BlockSpec(e64.shape, lambda i, j: (0, 0))],
        out_specs=blk(SSD_WIDTH),
        out_shape=jax.ShapeDtypeStruct((b, l, SSD_WIDTH), BF16),
        scratch_shapes=[pltpu.VMEM((nb, SUBLANES, SSD_CONV_DIM), F32),
                        pltpu.VMEM((nb, lb, SSD_CONV_DIM), F32),
                        pltpu.VMEM((nb, SSD_HEADS // 2, SSD_STATE, LANES), F32)],
        compiler_params=_params("parallel", "arbitrary"),
        name="ssd",
    )(z3, xbc3, dt3, cw, cb, dtb, alog, dsk, gn, e64)


FFN_CHUNK = MXU_WIDTH


def _gelu_tanh(v):
    return 0.5 * v * (1.0 + jnp.tanh(0.7978845608028654 * (v + 0.044715 * (v * v * v))))


def _post_kernel(x_ref, ys_ref, u_ref, yb_ref, d5_ref, wglu_ref, bglu_ref, g5_ref,
                 woa_ref, wob_ref, gf_ref, wg_ref, wu_ref, wd_ref, gfin_ref, o_ref, y_scr,
                 *, final):
    bsz, tok, _ = x_ref.shape
    for b in range(bsz):
        for c in range(tok // S5_CHUNK):
            r0 = b * tok + c * S5_CHUNK
            for j in range(D_MODEL // LANES):
                cols = slice(j * LANES, (j + 1) * LANES)
                rows = _s5_rows(c, j, b)
                y_scr[r0:r0 + S5_CHUNK, cols] = ys_ref[rows, :] + d5_ref[:, cols] * u_ref[rows, :]
    g = _gelu_tanh(y_scr[...])
    ya = g * jax.nn.sigmoid(_dot(g.astype(BF16), wglu_ref[...]) + bglu_ref[...])
    ya = _rms(ya, g5_ref[...]).astype(BF16)
    yb = yb_ref[...].reshape(bsz * tok, SSD_WIDTH)
    x = x_ref[...].reshape(bsz * tok, D_MODEL) + _dot(ya, woa_ref[...]) + _dot(yb, wob_ref[...])
    hb = _rms(x, gf_ref[...]).astype(BF16)
    y_scr[...] = x

    def ffn_chunk(k, carry):
        sl = pl.ds(pl.multiple_of(k * FFN_CHUNK, FFN_CHUNK), FFN_CHUNK)
        gate = _dot(hb, wg_ref[:, sl])
        act = (gate * jax.nn.sigmoid(gate)) * _dot(hb, wu_ref[:, sl])
        y_scr[...] += _dot(act.astype(BF16), wd_ref[sl, :])
        return carry

    lax.fori_loop(0, FFN_HIDDEN // FFN_CHUNK, ffn_chunk, 0, unroll=4)
    out = y_scr[...]
    if final:
        out = _rms(out, gfin_ref[...])
    o_ref[...] = out.reshape(o_ref.shape)


def _post(x3, ys, u, yb3, d5, wglu, bglu, g5, wo, gf, wg, wu, wd, gfin, layer, tok, final):
    b, seq, _ = x3.shape
    nat = pl.BlockSpec((b, tok, D_MODEL), lambda i: (0, i, 0))
    s5 = pl.BlockSpec((tok * b * D_MODEL // LANES, LANES), lambda i: (i, 0))
    one = pl.Buffered(1)
    ls = lambda a: pl.BlockSpec((None,) + a.shape[1:], lambda i: (layer, 0, 0), pipeline_mode=one)
    wo_half = lambda k: pl.BlockSpec((None, D_MODEL, D_MODEL), lambda i: (layer, k, 0),
                                     pipeline_mode=one)
    gfin_spec = pl.BlockSpec(gfin.shape, lambda i: (0, 0), pipeline_mode=one)
    return pl.pallas_call(
        functools.partial(_post_kernel, final=final),
        grid=(seq // tok,),
        in_specs=[nat, s5, s5, nat, ls(d5), ls(wglu), ls(bglu), ls(g5), wo_half(0), wo_half(1),
                  ls(gf), ls(wg), ls(wu), ls(wd), gfin_spec],
        out_specs=nat,
        out_shape=jax.ShapeDtypeStruct(x3.shape, F32),
        scratch_shapes=[pltpu.VMEM((b * tok, D_MODEL), F32)],
        compiler_params=_params("parallel"),
        name="post",
    )(x3, ys, u, yb3, d5, wglu, bglu, g5, wo, wo, gf, wg, wu, wd, gfin)


def kernel(x, norm_mix, w_in, s5_lam_re, s5_lam_im, s5_log_step, s5_b_re, s5_b_im, s5_c_re, s5_c_im, s5_d, s5_w_glu, s5_b_glu, s5_norm, ssd_conv_w, ssd_conv_b, ssd_dt_bias, ssd_a_log, ssd_d, ssd_norm, w_out, norm_ffn, w_gate, w_up, w_down, norm_final):
    bsz, seq, _ = x.shape
    depth = w_in.shape[0]
    assert bsz == SUBLANES and seq % S5_CHUNK == 0 and seq % SSD_CHUNK == 0
    nc5 = seq // S5_CHUNK
    tok = 2 * S5_CHUNK
    lb = min(256, seq)
    dg = depth * S5_GROUPS
    rows = lambda a: a.reshape(depth, 1, -1).astype(F32)

    w_all = _cast_win(jnp.swapaxes(w_in, 1, 2), 256)
    wglu = _cast_bf16(s5_w_glu, D_MODEL // 2)
    wo = _cast_bf16(w_out, D_MODEL)
    wg = _cast_bf16(w_gate, D_MODEL // 2)
    wup = _cast_bf16(w_up, D_MODEL // 2)
    wd = _cast_bf16(w_down, FFN_HIDDEN // 4)

    dbl = lambda a: jnp.concatenate([a, a], axis=-1)
    mt, bt, ct, apow = _s5_prep(
        s5_log_step.reshape(dg, 1, 1),
        s5_lam_re.reshape(dg, S5_STATE, 1), s5_lam_im.reshape(dg, S5_STATE, 1),
        dbl(s5_lam_re).reshape(dg, 1, 2 * S5_STATE), dbl(s5_lam_im).reshape(dg, 1, 2 * S5_STATE),
        dbl(s5_c_re).reshape(dg, S5_GROUP, 2 * S5_STATE),
        dbl(s5_c_im).reshape(dg, S5_GROUP, 2 * S5_STATE),
        s5_b_re.reshape(dg, S5_STATE, S5_GROUP), s5_b_im.reshape(dg, S5_STATE, S5_GROUP))

    head_of_col = jnp.arange(SSD_WIDTH) // SSD_HEAD_DIM
    e64 = (jnp.arange(LANES)[:, None] == head_of_col[None, :]).astype(BF16)
    pad_h = lambda a: jnp.pad(a.astype(F32), ((0, 0), (0, LANES - SSD_HEADS))).reshape(
        depth, 1, LANES)
    dtb, alog = pad_h(ssd_dt_bias), pad_h(ssd_a_log)
    dsk = rows(jnp.repeat(ssd_d, SSD_HEAD_DIM, axis=-1))
    g_mix, g_s5, g_ssd, g_ffn = rows(norm_mix), rows(s5_norm), rows(ssd_norm), rows(norm_ffn)
    d5, bglu, cb = rows(s5_d), rows(s5_b_glu), rows(ssd_conv_b)
    cw = ssd_conv_w.astype(F32)
    gfin = norm_final.reshape(1, -1).astype(F32)

    s5_shape = (nc5, D_MODEL // LANES, S5_CHUNK, bsz, LANES)
    for i in range(depth):
        u, z, xbc, dt = _in_proj(x, g_mix, w_all, i, 2 * tok)
        ys = _s5_core(u.reshape(s5_shape), mt, bt, ct, apow, i).reshape(u.shape)
        yb = _ssd(z, xbc, dt, cw, cb, dtb, alog, dsk, g_ssd, e64, i, lb, 4)
        x = _post(x, ys, u, yb, d5, wglu, bglu, g_s5, wo, g_ffn, wg, wup, wd, gfin,
                  i, tok, i == depth - 1)
    return x
```

```python
import functools

import jax
import jax.numpy as jnp
from jax import lax
from jax.experimental import pallas as pl
from jax.experimental.pallas import tpu as pltpu

F32 = jnp.float32
BF16 = jnp.bfloat16
EPS = 1e-6

D_MODEL = 1024
S5_GROUP = 16
S5_GROUPS = 64
S5_STATE = 64
S5_CHUNK = 32
S5_FLAT = S5_CHUNK * S5_GROUP
SSD_HEAD_DIM = 64
SSD_HEADS = 16
SSD_GROUPS = 2
SSD_STATE = 128
SSD_CONV = 4
SSD_CHUNK = 128
SSD_WIDTH = SSD_HEADS * SSD_HEAD_DIM
SSD_CONV_DIM = SSD_WIDTH + 2 * SSD_GROUPS * SSD_STATE
FFN_HIDDEN = 2816
LANES = 128
SUBLANES = 8
MXU_WIDTH = 256
S5_GPB = LANES // S5_GROUP
S5_BUFS = 3
VMEM_LIMIT = 56 * 1024 * 1024
IN_U = D_MODEL
IN_Z = IN_U + SSD_WIDTH
IN_X = IN_Z + SSD_CONV_DIM


def _params(*sem):
    return pltpu.CompilerParams(dimension_semantics=sem, vmem_limit_bytes=VMEM_LIMIT)


def _rms(x, gain):
    return x * lax.rsqrt(jnp.mean(x * x, axis=-1, keepdims=True) + EPS) * gain


def _dot(a, b):
    return jnp.dot(a, b, preferred_element_type=F32)


def _split(v, terms):
    out = []
    for _ in range(terms - 1):
        hi = v.astype(BF16)
        out.append(hi)
        v = v - hi.astype(F32)
    out.append(v.astype(BF16))
    return out


def _dot_sel_rhs(v, sel, terms=3):
    return functools.reduce(jnp.add, [_dot(t, sel) for t in _split(v, terms)])


def _dot_sel_lhs(sel, v, terms=3):
    return functools.reduce(jnp.add, [_dot(sel, t) for t in _split(v, terms)])


def _dot_f32(a, b, terms=3):
    a_t = _split(a, terms)
    b_t = _split(b, terms)
    return functools.reduce(
        jnp.add, [_dot(a_t[i], b_t[j]) for i in range(terms) for j in range(terms - i)])


def _layer_spec(shape, layer, n_grid, col=0):
    zeros = (0,) * (len(shape) - 2)
    if n_grid == 1:
        return pl.BlockSpec((None,) + shape[1:], lambda i: (layer,) + zeros + (col,))
    return pl.BlockSpec((None,) + shape[1:], lambda i, j: (layer,) + zeros + (col,))


def _cast_kernel(w_ref, o_ref):
    o_ref[...] = w_ref[...].astype(o_ref.dtype)


def _cast_bf16(w, rb):
    depth, rows, cols = w.shape
    spec = pl.BlockSpec((1, rb, cols), lambda i, j: (i, j, 0))
    return pl.pallas_call(
        _cast_kernel, grid=(depth, rows // rb), in_specs=[spec], out_specs=spec,
        out_shape=jax.ShapeDtypeStruct(w.shape, BF16),
        compiler_params=_params("parallel", "parallel"), name="cast",
    )(w)


def _cast_win_kernel(w_ref, o_ref, *, n_out):
    rb = w_ref.shape[1]
    valid = n_out - pl.program_id(1) * rb
    row = lax.broadcasted_iota(jnp.int32, (rb, 1), 0)
    o_ref[0] = jnp.where(row < valid, w_ref[0], 0.0).T.astype(o_ref.dtype)


def _cast_win(w_in_t, rb):
    depth, n_out, d = w_in_t.shape
    nblk = pl.cdiv(n_out, rb)
    return pl.pallas_call(
        functools.partial(_cast_win_kernel, n_out=n_out), grid=(depth, nblk),
        in_specs=[pl.BlockSpec((1, rb, d), lambda i, j: (i, j, 0))],
        out_specs=pl.BlockSpec((1, d, rb), lambda i, j: (i, 0, j)),
        out_shape=jax.ShapeDtypeStruct((depth, d, nblk * rb), BF16),
        compiler_params=_params("parallel", "parallel"), name="cast_win",
    )(w_in_t)


def _s5_rows(c, j, b):
    return pl.ds(((c * (D_MODEL // LANES) + j) * S5_CHUNK) * SUBLANES + b, S5_CHUNK,
                 stride=SUBLANES)


def _inproj_kernel(x_ref, g_ref, wu_ref, wz_ref, wx0_ref, wx1_ref, wx2_ref, wdt_ref,
                   u_ref, z_ref, xbc_ref, dt_ref):
    bsz, tok, _ = x_ref.shape
    hb = _rms(x_ref[...], g_ref[...]).astype(BF16).reshape(bsz * tok, D_MODEL)
    u = _dot(hb, wu_ref[...])
    for b in range(bsz):
        for c in range(tok // S5_CHUNK):
            r0 = b * tok + c * S5_CHUNK
            for j in range(D_MODEL // LANES):
                u_ref[_s5_rows(c, j, b), :] = u[r0:r0 + S5_CHUNK, j * LANES:(j + 1) * LANES]
    z_ref[...] = _dot(hb, wz_ref[...]).astype(z_ref.dtype).reshape(z_ref.shape)
    xbc = jnp.concatenate([_dot(hb, w[...]).astype(xbc_ref.dtype)
                           for w in (wx0_ref, wx1_ref, wx2_ref)], axis=1)
    xbc_ref[...] = xbc.reshape(xbc_ref.shape)
    dt_ref[...] = _dot(hb, wdt_ref[...]).reshape(dt_ref.shape)


def _in_proj(x3, gain, w_all, layer, tok):
    b, seq, _ = x3.shape
    nat = lambda w: pl.BlockSpec((b, tok, w), lambda i: (0, i, 0))
    win = lambda width, start: pl.BlockSpec((None, D_MODEL, width),
                                            lambda i: (layer, 0, start // width))
    xw = SSD_CONV_DIM // 3
    return pl.pallas_call(
        _inproj_kernel,
        grid=(seq // tok,),
        in_specs=[nat(D_MODEL), _layer_spec(gain.shape, layer, 1), win(D_MODEL, 0),
                  win(SSD_WIDTH, IN_U), win(xw, IN_Z), win(xw, IN_Z + xw), win(xw, IN_Z + 2 * xw),
                  win(LANES, IN_X)],
        out_specs=[pl.BlockSpec((tok * b * D_MODEL // LANES, LANES), lambda i: (i, 0)),
                   nat(SSD_WIDTH), nat(SSD_CONV_DIM), nat(LANES)],
        out_shape=[jax.ShapeDtypeStruct((seq * b * D_MODEL // LANES, LANES), F32),
                   jax.ShapeDtypeStruct((b, seq, SSD_WIDTH), BF16),
                   jax.ShapeDtypeStruct((b, seq, SSD_CONV_DIM), BF16),
                   jax.ShapeDtypeStruct((b, seq, LANES), F32)],
        compiler_params=_params("parallel"),
        name="in_proj",
    )(x3, gain, *([w_all] * 6))


def _s5_prep_group(q, ls_ref, lrc_ref, lic_ref, lrr_ref, lir_ref, cre_ref, cim_ref,
                   bre_ref, bim_ref, mt_ref, bt_ref, ct_ref, apow_ref):
    step = jnp.exp(ls_ref[q])
    lrc = lrc_ref[q]
    lic = lic_ref[q]
    p2 = 2 * S5_STATE

    k_lane = lax.broadcasted_iota(jnp.int32, (1, LANES), 1).astype(F32)
    st = step * k_lane
    mag = jnp.exp(lrc * st)
    pre_c = mag * jnp.cos(lic * st)
    pim_c = mag * jnp.sin(lic * st)
    are = pre_c[:, 1:2]
    aim = pim_c[:, 1:2]
    den = lrc * lrc + lic * lic
    nr = are - 1.0
    cfr = (nr * lrc + aim * lic) / den
    cfi = (aim * lrc - nr * lic) / den
    bre = bre_ref[q]
    bim = bim_ref[q]
    bbr = cfr * bre - cfi * bim
    bbi = cfr * bim + cfi * bre

    lane = lax.broadcasted_iota(jnp.int32, (1, S5_FLAT), 1)
    h_of_lane = lax.broadcasted_iota(jnp.int32, (S5_GROUP, S5_FLAT), 1) % S5_GROUP
    e_h = (h_of_lane == lax.broadcasted_iota(jnp.int32, (S5_GROUP, S5_FLAT), 0)).astype(BF16)
    k_row = lax.broadcasted_iota(jnp.int32, (LANES, S5_FLAT), 0)
    j_lane = lax.broadcasted_iota(jnp.int32, (LANES, S5_FLAT), 1) // S5_GROUP
    e_rev = (k_row == (S5_CHUNK - 1) - j_lane).astype(BF16)
    btr = _dot_sel_rhs(bbr, e_h, 2)
    bti = _dot_sel_rhs(bbi, e_h, 2)
    prr = _dot_sel_rhs(pre_c, e_rev, 2)
    pri = _dot_sel_rhs(pim_c, e_rev, 2)
    wre = prr * btr - pri * bti
    wim = prr * bti + pri * btr
    bt_ref[q] = jnp.concatenate([wre, wim, wim, wre], axis=0).astype(bt_ref.dtype)

    lane2 = lax.broadcasted_iota(jnp.int32, (1, p2), 1)
    first = lane2 < S5_STATE
    c1 = cre_ref[q]
    c2 = cim_ref[q]
    lhs = jnp.where(first, c1, c2)
    rev = _dot_f32(lhs, jnp.concatenate([wre, -wim], axis=0), 2)
    for t in range(S5_CHUNK):
        rows = slice(t * S5_GROUP, (t + 1) * S5_GROUP)
        shift = (S5_FLAT - (S5_CHUNK - 1 - t) * S5_GROUP) % S5_FLAT
        blk = rev if shift == 0 else pltpu.roll(rev, shift, axis=1)
        if t < S5_CHUNK - 1:
            blk = jnp.where(lane < (t + 1) * S5_GROUP, blk, 0.0)
        mt_ref[q, rows, :] = blk.astype(mt_ref.dtype)

    lrr = lrr_ref[q]
    lir = lir_ref[q]
    tau = (lax.broadcasted_iota(jnp.int32, (S5_CHUNK, 1), 0) + 1).astype(F32)
    st_r = step * tau
    mag_r = jnp.exp(lrr * st_r)
    pre_r = mag_r * jnp.cos(lir * st_r)
    pim_r = mag_r * jnp.sin(lir * st_r)
    q1 = jnp.where(first, pre_r, pim_r)
    q2 = jnp.where(first, pim_r, pre_r)
    s1 = jnp.where(first, 1.0, -1.0).astype(F32)
    for t in range(S5_CHUNK):
        rows = slice(t * S5_GROUP, (t + 1) * S5_GROUP)
        ct_ref[q, rows, :] = (s1 * (c1 * q1[t:t + 1, :]) - c2 * q2[t:t + 1, :]).astype(ct_ref.dtype)
    atr = pre_r[S5_CHUNK - 1:S5_CHUNK, :]
    a2 = -s1 * pim_r[S5_CHUNK - 1:S5_CHUNK, :]
    apow_ref[q] = jnp.concatenate(
        [atr, a2, -a2, jnp.zeros((SUBLANES - 3, p2), F32)], axis=0)


def _s5_prep_kernel(*refs):
    for q in range(refs[0].shape[0]):
        _s5_prep_group(q, *refs)


def _s5_prep(ls, lrc, lic, lrr, lir, cre, cim, bre, bim):
    dg = ls.shape[0]
    gps = 8
    spec = lambda a: pl.BlockSpec((gps,) + a.shape[1:], lambda i: (i, 0, 0))
    ins = (ls, lrc, lic, lrr, lir, cre, cim, bre, bim)
    shapes = [(dg, S5_FLAT, S5_FLAT), (dg, 4 * S5_STATE, S5_FLAT),
              (dg, S5_FLAT, 2 * S5_STATE), (dg, SUBLANES, 2 * S5_STATE)]
    dtypes = [BF16, BF16, BF16, F32]
    return pl.pallas_call(
        _s5_prep_kernel,
        grid=(dg // gps,),
        in_specs=[spec(a) for a in ins],
        out_specs=[pl.BlockSpec((gps,) + s[1:], lambda i: (i, 0, 0)) for s in shapes],
        out_shape=[jax.ShapeDtypeStruct(s, d) for s, d in zip(shapes, dtypes)],
        compiler_params=_params("parallel"),
        name="s5_prep",
    )(*ins)


def _s5_core_kernel(u_hbm, mt_ref, bt_ref, ct_ref, ap_ref, y_hbm,
                    io_buf, zt_scr, e_scr, sp_scr, yt_scr, sem_in, sem_out, *, batch):
    ng = mt_ref.shape[0]
    nchunk = u_hbm.shape[0]
    cols = nchunk * batch
    half = S5_FLAT // 2
    j = pl.program_id(0)
    last = pl.num_programs(0) - 1
    nbuf = io_buf.shape[0]
    slot = j % nbuf
    nxt = (j + 1) % nbuf

    def in_copy(step, s, t):
        return pltpu.make_async_copy(u_hbm.at[:, step, t], io_buf.at[s, t], sem_in.at[s, t])

    def out_copy(step, s, t):
        return pltpu.make_async_copy(io_buf.at[s, t], y_hbm.at[:, step, t], sem_out.at[s, t])

    @pl.when(j == 0)
    def _():
        for t in range(S5_CHUNK):
            in_copy(j, slot, t).start()

    @pl.when(jnp.logical_and(j >= nbuf - 1, j < last))
    def _():
        for t in range(S5_CHUNK):
            out_copy(j + 1 - nbuf, nxt, t).wait()

    @pl.when(j < last)
    def _():
        for t in range(S5_CHUNK):
            in_copy(j + 1, nxt, t).start()

    for t in range(S5_CHUNK):
        in_copy(j, slot, t).wait()
    for t in range(S5_CHUNK):
        blk_t = io_buf[slot, t].reshape(cols, LANES).T.astype(BF16)
        for g in range(ng):
            zt_scr[g, t * S5_GROUP:(t + 1) * S5_GROUP, :] = blk_t[g * S5_GROUP:(g + 1) * S5_GROUP, :]

    for g in range(ng):
        e_scr[g] = _dot(bt_ref[g], zt_scr[g]).T

    def body(c, carry):
        r = pl.multiple_of(c * batch, batch)
        out = []
        for g in range(ng):
            s, sw = carry[g]
            sp_scr[g, pl.ds(r, batch), :] = s
            e = e_scr[g, pl.ds(r, batch), :]
            a1 = ap_ref[g, 0:1, :]
            a2 = ap_ref[g, 1:2, :]
            a2w = ap_ref[g, 2:3, :]
            out.append((a1 * s + a2 * sw + e[:, :LANES], a1 * sw + a2w * s + e[:, LANES:]))
        return tuple(out)

    zero = jnp.zeros((batch, 2 * S5_STATE), F32)
    lax.fori_loop(0, nchunk, body, tuple((zero, zero) for _ in range(ng)))

    for g in range(ng):
        y_off = lax.dot_general(ct_ref[g], sp_scr[g].astype(BF16), (((1,), (1,)), ((), ())),
                                preferred_element_type=F32)
        top = _dot(mt_ref[g, :half, :half], zt_scr[g, :half, :])
        bot = _dot(mt_ref[g, half:, :], zt_scr[g])
        yt_scr[g, :half, :] = (top + y_off[:half]).astype(yt_scr.dtype)
        yt_scr[g, half:, :] = (bot + y_off[half:]).astype(yt_scr.dtype)

    for t in range(S5_CHUNK):
        rows = slice(t * S5_GROUP, (t + 1) * S5_GROUP)
        blk = jnp.concatenate([yt_scr[g, rows, :] for g in range(ng)], axis=0)
        io_buf[slot, t] = blk.astype(F32).T.reshape(nchunk, batch, LANES)
        out_copy(j, slot, t).start()

    @pl.when(j == last)
    def _():
        for back in range(nbuf):
            for t in range(S5_CHUNK):
                out_copy(j - back, (j - back) % nbuf, t).wait()


def _s5_core(u5, mt, bt, ct, apow, layer):
    nchunk, nblk, _, batch, _ = u5.shape
    cols = nchunk * batch
    ng = S5_GPB
    op = lambda a: pl.BlockSpec((ng,) + a.shape[1:], lambda i: (layer * nblk + i, 0, 0))
    hbm = pl.BlockSpec(memory_space=pl.ANY)
    return pl.pallas_call(
        functools.partial(_s5_core_kernel, batch=batch),
        grid=(nblk,),
        in_specs=[hbm, op(mt), op(bt), op(ct), op(apow)],
        out_specs=hbm,
        out_shape=jax.ShapeDtypeStruct(u5.shape, F32),
        scratch_shapes=[pltpu.VMEM((S5_BUFS, S5_CHUNK, nchunk, batch, LANES), F32),
                        pltpu.VMEM((ng, S5_FLAT, cols), BF16),
                        pltpu.VMEM((ng, cols, 4 * S5_STATE), F32),
                        pltpu.VMEM((ng, cols, 2 * S5_STATE), F32),
                        pltpu.VMEM((ng, S5_FLAT, cols), BF16),
                        pltpu.SemaphoreType.DMA((S5_BUFS, S5_CHUNK)),
                        pltpu.SemaphoreType.DMA((S5_BUFS, S5_CHUNK))],
        compiler_params=_params("arbitrary"),
        name="s5_core",
    )(u5, mt, bt, ct, apow)


def _ssd_kernel(z_ref, xbc_ref, dt_ref, cw_ref, cb_ref, dtb_ref, alog_ref, dsk_ref,
                gn_ref, e64_ref, y_ref, tail_scr, xc_scr, st_scr):
    for i in range(z_ref.shape[0]):
        _ssd_batch(z_ref.at[i], xbc_ref.at[i], dt_ref.at[i], cw_ref, cb_ref, dtb_ref, alog_ref,
                   dsk_ref, gn_ref, e64_ref, y_ref.at[i], tail_scr.at[i], xc_scr.at[i],
                   st_scr.at[i])


def _ssd_batch(z_ref, xbc_ref, dt_ref, cw_ref, cb_ref, dtb_ref, alog_ref, dsk_ref,
               gn_ref, e64_ref, y_ref, tail_scr, xc_scr, st_scr):
    lb = z_ref.shape[0]
    j = pl.program_id(1)

    @pl.when(j == 0)
    def _():
        tail_scr[...] = jnp.zeros_like(tail_scr)
        st_scr[...] = jnp.zeros_like(st_scr)

    xb = xbc_ref[...]
    x0 = xb.astype(F32)
    rr = lax.broadcasted_iota(jnp.int32, (lb, lb), 0)
    cc = lax.broadcasted_iota(jnp.int32, (lb, lb), 1)
    row8 = lax.broadcasted_iota(jnp.int32, (SUBLANES, 1), 0)
    tail = tail_scr[...]
    acc = cb_ref[...] + cw_ref[SSD_CONV - 1:SSD_CONV, :] * x0
    head = jnp.zeros((SUBLANES, SSD_CONV_DIM), F32)
    for k in range(1, SSD_CONV):
        wk = cw_ref[SSD_CONV - 1 - k:SSD_CONV - k, :]
        acc = acc + wk * _dot((rr - cc == k).astype(BF16), xb)
        head = head + wk * jnp.where(row8 < k, pltpu.roll(tail, k, axis=0), 0.0)
    tail_scr[...] = x0[lb - SUBLANES:, :]
    xc_scr[...] = acc * jax.nn.sigmoid(acc)
    acc0 = acc[0:SUBLANES, :] + head
    xc_scr[0:SUBLANES, :] = acc0 * jax.nn.sigmoid(acc0)

    dt_t = (dt_ref[...] + dtb_ref[...]).T[0:SSD_HEADS, :]
    dtp_t = jnp.maximum(dt_t, 0.0) + jnp.log(1.0 + jnp.exp(-jnp.abs(dt_t)))
    dtp_all = jnp.concatenate(
        [dtp_t, jnp.zeros((LANES - SSD_HEADS, lb), F32)], axis=0).T

    ri = lax.broadcasted_iota(jnp.int32, (SSD_CHUNK, SSD_CHUNK), 0)
    ci = lax.broadcasted_iota(jnp.int32, (SSD_CHUNK, SSD_CHUNK), 1)
    causal = ri >= ci
    ltri = causal.astype(BF16)
    lo_half = ci < SSD_HEAD_DIM
    a_neg = -jnp.exp(alog_ref[...])
    e64 = e64_ref[...]
    b0 = SSD_WIDTH
    c0 = SSD_WIDTH + SSD_GROUPS * SSD_STATE
    hpg = SSD_HEADS // SSD_GROUPS

    for c in range(lb // SSD_CHUNK):
        r0 = c * SSD_CHUNK
        xs = xc_scr[r0:r0 + SSD_CHUNK, 0:SSD_WIDTH]
        dtp = dtp_all[r0:r0 + SSD_CHUNK, :]
        dta = dtp * a_neg
        acum = _dot_sel_lhs(ltri, dta)
        acum_t = acum.T
        atot = acum[SSD_CHUNK - 1:SSD_CHUNK, :]
        fac = jnp.concatenate([dtp, dtp * jnp.exp(atot - acum), jnp.exp(acum)], axis=0)
        fac64 = _dot(fac.astype(BF16), e64)
        xdt = xs * fac64[0:SSD_CHUNK]
        xw = (xs * fac64[SSD_CHUNK:2 * SSD_CHUNK]).astype(BF16)
        ea64 = fac64[2 * SSD_CHUNK:]
        tail = SSD_CHUNK - SUBLANES
        etot64 = jnp.exp(_dot_sel_rhs(acum[tail:, :], e64)[SUBLANES - 1:SUBLANES, :])
        gmats = []
        for g in range(SSD_GROUPS):
            bm = xc_scr[r0:r0 + SSD_CHUNK, b0 + g * SSD_STATE:b0 + (g + 1) * SSD_STATE]
            cm = xc_scr[r0:r0 + SSD_CHUNK, c0 + g * SSD_STATE:c0 + (g + 1) * SSD_STATE].astype(BF16)
            bm_t = bm.T.astype(BF16)
            gmats.append((bm_t, cm, _dot(cm, bm_t)))
        ys = []
        for p in range(SSD_HEADS // 2):
            bm_t, cm, gm = gmats[(2 * p) // hpg]
            cols = slice(p * LANES, (p + 1) * LANES)
            xp = xdt[:, cols]
            y = None
            for hh in range(2):
                h = 2 * p + hh
                col = jnp.broadcast_to(acum[:, h:h + 1], (SSD_CHUNK, SSD_CHUNK))
                rowv = acum_t[h:h + 1, :]
                w = jnp.where(causal, gm * jnp.exp(col - rowv), 0.0).astype(BF16)
                xh = jnp.where(lo_half if hh == 0 else ~lo_half, xp, 0.0).astype(BF16)
                t = _dot(w, xh)
                y = t if y is None else y + t
            st = st_scr[p]
            y = y + _dot(cm, st.astype(BF16)) * ea64[:, cols]
            st_scr[p] = st * etot64[:, cols] + _dot(bm_t, xw[:, cols])
            ys.append(y + dsk_ref[:, cols] * xs[:, cols])
        yc = jnp.concatenate(ys, axis=1)
        zc = z_ref[r0:r0 + SSD_CHUNK, :].astype(F32)
        yc = yc * (zc * jax.nn.sigmoid(zc))
        y_ref[r0:r0 + SSD_CHUNK, :] = _rms(yc, gn_ref[...]).astype(y_ref.dtype)


def _ssd(z3, xbc3, dt3, cw, cb, dtb, alog, dsk, gn, e64, layer, lb, nb):
    b, l, _ = z3.shape
    blk = lambda w: pl.BlockSpec((nb, lb, w), lambda i, j: (i, j, 0))
    ls = lambda a: _layer_spec(a.shape, layer, 2)
    return pl.pallas_call(
        _ssd_kernel,
        grid=(b // nb, l // lb),
        in_specs=[blk(SSD_WIDTH), blk(SSD_CONV_DIM), blk(LANES), ls(cw), ls(cb),
                  ls(dtb), ls(alog), ls(dsk), ls(gn),
                  pl.BlockSpec(e64.shape, lambda i, j: (0, 0))],
        out_specs=blk(SSD_WIDTH),
        out_shape=jax.ShapeDtypeStruct((b, l, SSD_WIDTH), BF16),
        scratch_shapes=[pltpu.VMEM((nb, SUBLANES, SSD_CONV_DIM), F32),
                        pltpu.VMEM((nb, lb, SSD_CONV_DIM), F32),
                        pltpu.VMEM((nb, SSD_HEADS // 2, SSD_STATE, LANES), F32)],
        compiler_params=_params("parallel", "arbitrary"),
        name="ssd",
    )(z3, xbc3, dt3, cw, cb, dtb, alog, dsk, gn, e64)


FFN_CHUNK = MXU_WIDTH


def _gelu_tanh(v):
    return 0.5 * v * (1.0 + jnp.tanh(0.7978845608028654 * (v + 0.044715 * (v * v * v))))


def _post_kernel(x_ref, ys_ref, u_ref, yb_ref, d5_ref, wglu_ref, bglu_ref, g5_ref,
                 woa_ref, wob_ref, gf_ref, wg_ref, wu_ref, wd_ref, gfin_ref, o_ref, y_scr,
                 *, final):
    bsz, tok, _ = x_ref.shape
    for b in range(bsz):
        for c in range(tok // S5_CHUNK):
            r0 = b * tok + c * S5_CHUNK
            for j in range(D_MODEL // LANES):
                cols = slice(j * LANES, (j + 1) * LANES)
                rows = _s5_rows(c, j, b)
                y_scr[r0:r0 + S5_CHUNK, cols] = ys_ref[rows, :] + d5_ref[:, cols] * u_ref[rows, :]
    g = _gelu_tanh(y_scr[...])
    ya = g * jax.nn.sigmoid(_dot(g.astype(BF16), wglu_ref[...]) + bglu_ref[...])
    ya = _rms(ya, g5_ref[...]).astype(BF16)
    yb = yb_ref[...].reshape(bsz * tok, SSD_WIDTH)
    x = x_ref[...].reshape(bsz * tok, D_MODEL) + _dot(ya, woa_ref[...]) + _dot(yb, wob_ref[...])
    hb = _rms(x, gf_ref[...]).astype(BF16)
    y_scr[...] = x

    def ffn_chunk(k, carry):
        sl = pl.ds(pl.multiple_of(k * FFN_CHUNK, FFN_CHUNK), FFN_CHUNK)
        gate = _dot(hb, wg_ref[:, sl])
        act = (gate * jax.nn.sigmoid(gate)) * _dot(hb, wu_ref[:, sl])
        y_scr[...] += _dot(act.astype(BF16), wd_ref[sl, :])
        return carry

    lax.fori_loop(0, FFN_HIDDEN // FFN_CHUNK, ffn_chunk, 0, unroll=4)
    out = y_scr[...]
    if final:
        out = _rms(out, gfin_ref[...])
    o_ref[...] = out.reshape(o_ref.shape)


def _post(x3, ys, u, yb3, d5, wglu, bglu, g5, wo, gf, wg, wu, wd, gfin, layer, tok, final):
    b, seq, _ = x3.shape
    nat = pl.BlockSpec((b, tok, D_MODEL), lambda i: (0, i, 0))
    s5 = pl.BlockSpec((tok * b * D_MODEL // LANES, LANES), lambda i: (i, 0))
    one = pl.Buffered(1)
    ls = lambda a: pl.BlockSpec((None,) + a.shape[1:], lambda i: (layer, 0, 0), pipeline_mode=one)
    wo_half = lambda k: pl.BlockSpec((None, D_MODEL, D_MODEL), lambda i: (layer, k, 0),
                                     pipeline_mode=one)
    gfin_spec = pl.BlockSpec(gfin.shape, lambda i: (0, 0), pipeline_mode=one)
    return pl.pallas_call(
        functools.partial(_post_kernel, final=final),
        grid=(seq // tok,),
        in_specs=[nat, s5, s5, nat, ls(d5), ls(wglu), ls(bglu), ls(g5), wo_half(0), wo_half(1),
                  ls(gf), ls(wg), ls(wu), ls(wd), gfin_spec],
        out_specs=nat,
        out_shape=jax.ShapeDtypeStruct(x3.shape, F32),
        scratch_shapes=[pltpu.VMEM((b * tok, D_MODEL), F32)],
        compiler_params=_params("parallel"),
        name="post",
    )(x3, ys, u, yb3, d5, wglu, bglu, g5, wo, wo, gf, wg, wu, wd, gfin)


def kernel(x, norm_mix, w_in, s5_lam_re, s5_lam_im, s5_log_step, s5_b_re, s5_b_im, s5_c_re, s5_c_im, s5_d, s5_w_glu, s5_b_glu, s5_norm, ssd_conv_w, ssd_conv_b, ssd_dt_bias, ssd_a_log, ssd_d, ssd_norm, w_out, norm_ffn, w_gate, w_up, w_down, norm_final):
    bsz, seq, _ = x.shape
    depth = w_in.shape[0]
    assert bsz == SUBLANES and seq % S5_CHUNK == 0 and seq % SSD_CHUNK == 0
    nc5 = seq // S5_CHUNK
    tok = 2 * S5_CHUNK
    lb = min(2 * SSD_CHUNK, seq)
    dg = depth * S5_GROUPS
    rows = lambda a: a.reshape(depth, 1, -1).astype(F32)

    w_all = _cast_win(jnp.swapaxes(w_in, 1, 2), 2 * MXU_WIDTH)
    wglu = _cast_bf16(s5_w_glu, D_MODEL // 2)
    wo = _cast_bf16(w_out, D_MODEL)
    wg = _cast_bf16(w_gate, D_MODEL // 2)
    wup = _cast_bf16(w_up, D_MODEL // 2)
    wd = _cast_bf16(w_down, FFN_HIDDEN // 4)

    dbl = lambda a: jnp.concatenate([a, a], axis=-1)
    mt, bt, ct, apow = _s5_prep(
        s5_log_step.reshape(dg, 1, 1),
        s5_lam_re.reshape(dg, S5_STATE, 1), s5_lam_im.reshape(dg, S5_STATE, 1),
        dbl(s5_lam_re).reshape(dg, 1, 2 * S5_STATE), dbl(s5_lam_im).reshape(dg, 1, 2 * S5_STATE),
        dbl(s5_c_re).reshape(dg, S5_GROUP, 2 * S5_STATE),
        dbl(s5_c_im).reshape(dg, S5_GROUP, 2 * S5_STATE),
        s5_b_re.reshape(dg, S5_STATE, S5_GROUP), s5_b_im.reshape(dg, S5_STATE, S5_GROUP))

    head_of_col = jnp.arange(SSD_WIDTH) // SSD_HEAD_DIM
    e64 = (jnp.arange(LANES)[:, None] == head_of_col[None, :]).astype(BF16)
    pad_h = lambda a: jnp.pad(a.astype(F32), ((0, 0), (0, LANES - SSD_HEADS))).reshape(
        depth, 1, LANES)
    dtb, alog = pad_h(ssd_dt_bias), pad_h(ssd_a_log)
    dsk = rows(jnp.repeat(ssd_d, SSD_HEAD_DIM, axis=-1))
    g_mix, g_s5, g_ssd, g_ffn = rows(norm_mix), rows(s5_norm), rows(ssd_norm), rows(norm_ffn)
    d5, bglu, cb = rows(s5_d), rows(s5_b_glu), rows(ssd_conv_b)
    cw = ssd_conv_w.astype(F32)
    gfin = norm_final.reshape(1, -1).astype(F32)

    s5_shape = (nc5, D_MODEL // LANES, S5_CHUNK, bsz, LANES)
    for i in range(depth):
        u, z, xbc, dt = _in_proj(x, g_mix, w_all, i, 2 * tok)
        ys = _s5_core(u.reshape(s5_shape), mt, bt, ct, apow, i).reshape(u.shape)
        yb = _ssd(z, xbc, dt, cw, cb, dtb, alog, dsk, g_ssd, e64, i, lb, 4)
        x = _post(x, ys, u, yb, d5, wglu, bglu, g_s5, wo, g_ffn, wg, wup, wd, gfin,
                  i, tok, i == depth - 1)
    return x
```

```python
import functools

import jax
import jax.numpy as jnp
from jax import lax
from jax.experimental import pallas as pl
from jax.experimental.pallas import tpu as pltpu

F32 = jnp.float32
BF16 = jnp.bfloat16
EPS = 1e-6

D_MODEL = 1024
S5_GROUP = 16
S5_GROUPS = 64
S5_STATE = 64
S5_CHUNK = 32
S5_FLAT = S5_CHUNK * S5_GROUP
SSD_HEAD_DIM = 64
SSD_HEADS = 16
SSD_GROUPS = 2
SSD_STATE = 128
SSD_CONV = 4
SSD_CHUNK = 128
SSD_WIDTH = SSD_HEADS * SSD_HEAD_DIM
SSD_CONV_DIM = SSD_WIDTH + 2 * SSD_GROUPS * SSD_STATE
FFN_HIDDEN = 2816
LANES = 128
SUBLANES = 8
MXU_WIDTH = 256
S5_GPB = LANES // S5_GROUP
S5_BUFS = 3
VMEM_LIMIT = 56 * 1024 * 1024
IN_U = D_MODEL
IN_Z = IN_U + SSD_WIDTH
IN_X = IN_Z + SSD_CONV_DIM


def _params(*sem):
    return pltpu.CompilerParams(dimension_semantics=sem, vmem_limit_bytes=VMEM_LIMIT)


def _rms(x, gain):
    return x * lax.rsqrt(jnp.mean(x * x, axis=-1, keepdims=True) + EPS) * gain


def _dot(a, b):
    return jnp.dot(a, b, preferred_element_type=F32)


def _split(v, terms):
    out = []
    for _ in range(terms - 1):
        hi = v.astype(BF16)
        out.append(hi)
        v = v - hi.astype(F32)
    out.append(v.astype(BF16))
    return out


def _dot_sel_rhs(v, sel, terms=3):
    return functools.reduce(jnp.add, [_dot(t, sel) for t in _split(v, terms)])


def _dot_sel_lhs(sel, v, terms=3):
    return functools.reduce(jnp.add, [_dot(sel, t) for t in _split(v, terms)])


def _dot_f32(a, b, terms=3):
    a_t = _split(a, terms)
    b_t = _split(b, terms)
    return functools.reduce(
        jnp.add, [_dot(a_t[i], b_t[j]) for i in range(terms) for j in range(terms - i)])


def _layer_spec(shape, layer, n_grid, col=0):
    zeros = (0,) * (len(shape) - 2)
    if n_grid == 1:
        return pl.BlockSpec((None,) + shape[1:], lambda i: (layer,) + zeros + (col,))
    return pl.BlockSpec((None,) + shape[1:], lambda i, j: (layer,) + zeros + (col,))


def _cast_kernel(w_ref, o_ref):
    o_ref[...] = w_ref[...].astype(o_ref.dtype)


def _cast_bf16(w, rb):
    depth, rows, cols = w.shape
    spec = pl.BlockSpec((1, rb, cols), lambda i, j: (i, j, 0))
    return pl.pallas_call(
        _cast_kernel, grid=(depth, rows // rb), in_specs=[spec], out_specs=spec,
        out_shape=jax.ShapeDtypeStruct(w.shape, BF16),
        compiler_params=_params("parallel", "parallel"), name="cast",
    )(w)


def _cast_win_kernel(w_ref, o_ref, *, n_out):
    rb = w_ref.shape[1]
    valid = n_out - pl.program_id(1) * rb
    row = lax.broadcasted_iota(jnp.int32, (rb, 1), 0)
    o_ref[0] = jnp.where(row < valid, w_ref[0], 0.0).T.astype(o_ref.dtype)


def _cast_win(w_in_t, rb):
    depth, n_out, d = w_in_t.shape
    nblk = pl.cdiv(n_out, rb)
    return pl.pallas_call(
        functools.partial(_cast_win_kernel, n_out=n_out), grid=(depth, nblk),
        in_specs=[pl.BlockSpec((1, rb, d), lambda i, j: (i, j, 0))],
        out_specs=pl.BlockSpec((1, d, rb), lambda i, j: (i, 0, j)),
        out_shape=jax.ShapeDtypeStruct((depth, d, nblk * rb), BF16),
        compiler_params=_params("parallel", "parallel"), name="cast_win",
    )(w_in_t)


def _s5_rows(c, j, b):
    return pl.ds(((c * (D_MODEL // LANES) + j) * S5_CHUNK) * SUBLANES + b, S5_CHUNK,
                 stride=SUBLANES)


def _inproj_kernel(x_ref, g_ref, wu_ref, wz_ref, wx0_ref, wx1_ref, wx2_ref, wdt_ref,
                   u_ref, z_ref, xbc_ref, dt_ref):
    bsz, tok, _ = x_ref.shape
    hb = _rms(x_ref[...], g_ref[...]).astype(BF16).reshape(bsz * tok, D_MODEL)
    u = _dot(hb, wu_ref[...])
    for b in range(bsz):
        for c in range(tok // S5_CHUNK):
            r0 = b * tok + c * S5_CHUNK
            for j in range(D_MODEL // LANES):
                u_ref[_s5_rows(c, j, b), :] = u[r0:r0 + S5_CHUNK, j * LANES:(j + 1) * LANES]
    z_ref[...] = _dot(hb, wz_ref[...]).astype(z_ref.dtype).reshape(z_ref.shape)
    xbc = jnp.concatenate([_dot(hb, w[...]).astype(xbc_ref.dtype)
                           for w in (wx0_ref, wx1_ref, wx2_ref)], axis=1)
    xbc_ref[...] = xbc.reshape(xbc_ref.shape)
    dt_ref[...] = _dot(hb, wdt_ref[...]).reshape(dt_ref.shape)


def _in_proj(x3, gain, w_all, layer, tok):
    b, seq, _ = x3.shape
    nat = lambda w: pl.BlockSpec((b, tok, w), lambda i: (0, i, 0))
    win = lambda width, start: pl.BlockSpec((None, D_MODEL, width),
                                            lambda i: (layer, 0, start // width))
    xw = SSD_CONV_DIM // 3
    return pl.pallas_call(
        _inproj_kernel,
        grid=(seq // tok,),
        in_specs=[nat(D_MODEL), _layer_spec(gain.shape, layer, 1), win(D_MODEL, 0),
                  win(SSD_WIDTH, IN_U), win(xw, IN_Z), win(xw, IN_Z + xw), win(xw, IN_Z + 2 * xw),
                  win(LANES, IN_X)],
        out_specs=[pl.BlockSpec((tok * b * D_MODEL // LANES, LANES), lambda i: (i, 0)),
                   nat(SSD_WIDTH), nat(SSD_CONV_DIM), nat(LANES)],
        out_shape=[jax.ShapeDtypeStruct((seq * b * D_MODEL // LANES, LANES), F32),
                   jax.ShapeDtypeStruct((b, seq, SSD_WIDTH), BF16),
                   jax.ShapeDtypeStruct((b, seq, SSD_CONV_DIM), BF16),
                   jax.ShapeDtypeStruct((b, seq, LANES), F32)],
        compiler_params=_params("parallel"),
        name="in_proj",
    )(x3, gain, *([w_all] * 6))


def _s5_prep_group(q, e_h, e_rev, ls_ref, lrc_ref, lic_ref, lrr_ref, lir_ref, cre_ref, cim_ref,
                   bre_ref, bim_ref, mt_ref, bt_ref, ct_ref, apow_ref):
    step = jnp.exp(ls_ref[q])
    lrc = lrc_ref[q]
    lic = lic_ref[q]
    lrr = lrr_ref[q]
    lir = lir_ref[q]
    p2 = 2 * S5_STATE

    n_pow = S5_CHUNK + SUBLANES
    st = step * lax.broadcasted_iota(jnp.int32, (n_pow, 1), 0).astype(F32)
    mag = jnp.exp(lrr * st)
    pre_k = mag * jnp.cos(lir * st)
    pim_k = mag * jnp.sin(lir * st)

    def to_cols(p_k):
        return jnp.concatenate(
            [p_k, jnp.zeros((LANES - n_pow, p2), F32)], axis=0).T[0:S5_STATE, :]

    pre_c = to_cols(pre_k)
    pim_c = to_cols(pim_k)
    are = pre_c[:, 1:2]
    aim = pim_c[:, 1:2]
    den = lrc * lrc + lic * lic
    nr = are - 1.0
    cfr = (nr * lrc + aim * lic) / den
    cfi = (aim * lrc - nr * lic) / den
    bre = bre_ref[q]
    bim = bim_ref[q]
    bbr = cfr * bre - cfi * bim
    bbi = cfr * bim + cfi * bre

    lane = lax.broadcasted_iota(jnp.int32, (1, S5_FLAT), 1)
    btr = _dot_sel_rhs(bbr, e_h, 2)
    bti = _dot_sel_rhs(bbi, e_h, 2)
    prr = _dot_sel_rhs(pre_c, e_rev, 2)
    pri = _dot_sel_rhs(pim_c, e_rev, 2)
    wre = prr * btr - pri * bti
    wim = prr * bti + pri * btr
    bt_ref[q] = jnp.concatenate([wre, wim, wim, wre], axis=0).astype(bt_ref.dtype)

    lane2 = lax.broadcasted_iota(jnp.int32, (1, p2), 1)
    first = lane2 < S5_STATE
    c1 = cre_ref[q]
    c2 = cim_ref[q]
    lhs = jnp.where(first, c1, c2)
    rev = _dot_f32(lhs, jnp.concatenate([wre, -wim], axis=0), 2)
    for t in range(S5_CHUNK):
        rows = slice(t * S5_GROUP, (t + 1) * S5_GROUP)
        shift = (S5_FLAT - (S5_CHUNK - 1 - t) * S5_GROUP) % S5_FLAT
        blk = rev if shift == 0 else pltpu.roll(rev, shift, axis=1)
        if t < S5_CHUNK - 1:
            blk = jnp.where(lane < (t + 1) * S5_GROUP, blk, 0.0)
        mt_ref[q, rows, :] = blk.astype(mt_ref.dtype)

    pre_r = pre_k[1:S5_CHUNK + 1, :]
    pim_r = pim_k[1:S5_CHUNK + 1, :]
    q1 = jnp.where(first, pre_r, pim_r)
    q2 = jnp.where(first, pim_r, pre_r)
    s1 = jnp.where(first, 1.0, -1.0).astype(F32)
    for t in range(S5_CHUNK):
        rows = slice(t * S5_GROUP, (t + 1) * S5_GROUP)
        ct_ref[q, rows, :] = (s1 * (c1 * q1[t:t + 1, :]) - c2 * q2[t:t + 1, :]).astype(ct_ref.dtype)
    atr = pre_r[S5_CHUNK - 1:S5_CHUNK, :]
    a2 = -s1 * pim_r[S5_CHUNK - 1:S5_CHUNK, :]
    apow_ref[q] = jnp.concatenate(
        [atr, a2, -a2, jnp.zeros((SUBLANES - 3, p2), F32)], axis=0)


def _s5_prep_kernel(*refs):
    h_of_lane = lax.broadcasted_iota(jnp.int32, (S5_GROUP, S5_FLAT), 1) % S5_GROUP
    e_h = (h_of_lane == lax.broadcasted_iota(jnp.int32, (S5_GROUP, S5_FLAT), 0)).astype(BF16)
    k_row = lax.broadcasted_iota(jnp.int32, (LANES, S5_FLAT), 0)
    j_lane = lax.broadcasted_iota(jnp.int32, (LANES, S5_FLAT), 1) // S5_GROUP
    e_rev = (k_row == (S5_CHUNK - 1) - j_lane).astype(BF16)
    for q in range(refs[0].shape[0]):
        _s5_prep_group(q, e_h, e_rev, *refs)


def _s5_prep(ls, lrc, lic, lrr, lir, cre, cim, bre, bim):
    dg = ls.shape[0]
    gps = 8
    spec = lambda a: pl.BlockSpec((gps,) + a.shape[1:], lambda i: (i, 0, 0))
    ins = (ls, lrc, lic, lrr, lir, cre, cim, bre, bim)
    shapes = [(dg, S5_FLAT, S5_FLAT), (dg, 4 * S5_STATE, S5_FLAT),
              (dg, S5_FLAT, 2 * S5_STATE), (dg, SUBLANES, 2 * S5_STATE)]
    dtypes = [BF16, BF16, BF16, F32]
    return pl.pallas_call(
        _s5_prep_kernel,
        grid=(dg // gps,),
        in_specs=[spec(a) for a in ins],
        out_specs=[pl.BlockSpec((gps,) + s[1:], lambda i: (i, 0, 0)) for s in shapes],
        out_shape=[jax.ShapeDtypeStruct(s, d) for s, d in zip(shapes, dtypes)],
        compiler_params=_params("parallel"),
        name="s5_prep",
    )(*ins)


def _s5_core_kernel(u_hbm, mt_ref, bt_ref, ct_ref, ap_ref, y_hbm,
                    io_buf, zt_scr, e_scr, sp_scr, yt_scr, sem_in, sem_out, *, batch):
    ng = mt_ref.shape[0]
    nchunk = u_hbm.shape[0]
    cols = nchunk * batch
    half = S5_FLAT // 2
    j = pl.program_id(0)
    last = pl.num_programs(0) - 1
    nbuf = io_buf.shape[0]
    slot = j % nbuf
    nxt = (j + 1) % nbuf

    def in_copy(step, s, t):
        return pltpu.make_async_copy(u_hbm.at[:, step, t], io_buf.at[s, t], sem_in.at[s, t])

    def out_copy(step, s, t):
        return pltpu.make_async_copy(io_buf.at[s, t], y_hbm.at[:, step, t], sem_out.at[s, t])

    @pl.when(j == 0)
    def _():
        for t in range(S5_CHUNK):
            in_copy(j, slot, t).start()

    @pl.when(jnp.logical_and(j >= nbuf - 1, j < last))
    def _():
        for t in range(S5_CHUNK):
            out_copy(j + 1 - nbuf, nxt, t).wait()

    @pl.when(j < last)
    def _():
        for t in range(S5_CHUNK):
            in_copy(j + 1, nxt, t).start()

    for t in range(S5_CHUNK):
        in_copy(j, slot, t).wait()
    for t in range(S5_CHUNK):
        blk_t = io_buf[slot, t].reshape(cols, LANES).T.astype(BF16)
        for g in range(ng):
            zt_scr[g, t * S5_GROUP:(t + 1) * S5_GROUP, :] = blk_t[g * S5_GROUP:(g + 1) * S5_GROUP, :]

    for g in range(ng):
        e_scr[g] = _dot(bt_ref[g], zt_scr[g]).T

    def body(c, carry):
        r = pl.multiple_of(c * batch, batch)
        out = []
        for g in range(ng):
            s, sw = carry[g]
            sp_scr[g, pl.ds(r, batch), :] = s
            e = e_scr[g, pl.ds(r, batch), :]
            a1 = ap_ref[g, 0:1, :]
            a2 = ap_ref[g, 1:2, :]
            a2w = ap_ref[g, 2:3, :]
            out.append((a1 * s + a2 * sw + e[:, :LANES], a1 * sw + a2w * s + e[:, LANES:]))
        return tuple(out)

    zero = jnp.zeros((batch, 2 * S5_STATE), F32)
    lax.fori_loop(0, nchunk, body, tuple((zero, zero) for _ in range(ng)))

    for g in range(ng):
        y_off = lax.dot_general(ct_ref[g], sp_scr[g].astype(BF16), (((1,), (1,)), ((), ())),
                                preferred_element_type=F32)
        top = _dot(mt_ref[g, :half, :half], zt_scr[g, :half, :])
        bot = _dot(mt_ref[g, half:, :], zt_scr[g])
        yt_scr[g, :half, :] = (top + y_off[:half]).astype(yt_scr.dtype)
        yt_scr[g, half:, :] = (bot + y_off[half:]).astype(yt_scr.dtype)

    for t in range(S5_CHUNK):
        rows = slice(t * S5_GROUP, (t + 1) * S5_GROUP)
        blk = jnp.concatenate([yt_scr[g, rows, :] for g in range(ng)], axis=0)
        io_buf[slot, t] = blk.astype(F32).T.reshape(nchunk, batch, LANES)
        out_copy(j, slot, t).start()

    @pl.when(j == last)
    def _():
        for back in range(nbuf):
            for t in range(S5_CHUNK):
                out_copy(j - back, (j - back) % nbuf, t).wait()


def _s5_core(u5, mt, bt, ct, apow, layer):
    nchunk, nblk, _, batch, _ = u5.shape
    cols = nchunk * batch
    ng = S5_GPB
    op = lambda a: pl.BlockSpec((ng,) + a.shape[1:], lambda i: (layer * nblk + i, 0, 0))
    hbm = pl.BlockSpec(memory_space=pl.ANY)
    return pl.pallas_call(
        functools.partial(_s5_core_kernel, batch=batch),
        grid=(nblk,),
        in_specs=[hbm, op(mt), op(bt), op(ct), op(apow)],
        out_specs=hbm,
        out_shape=jax.ShapeDtypeStruct(u5.shape, F32),
        scratch_shapes=[pltpu.VMEM((S5_BUFS, S5_CHUNK, nchunk, batch, LANES), F32),
                        pltpu.VMEM((ng, S5_FLAT, cols), BF16),
                        pltpu.VMEM((ng, cols, 4 * S5_STATE), F32),
                        pltpu.VMEM((ng, cols, 2 * S5_STATE), F32),
                        pltpu.VMEM((ng, S5_FLAT, cols), BF16),
                        pltpu.SemaphoreType.DMA((S5_BUFS, S5_CHUNK)),
                        pltpu.SemaphoreType.DMA((S5_BUFS, S5_CHUNK))],
        compiler_params=_params("arbitrary"),
        name="s5_core",
    )(u5, mt, bt, ct, apow)


def _ssd_kernel(z_ref, xbc_ref, dt_ref, cw_ref, cb_ref, dtb_ref, alog_ref, dsk_ref,
                gn_ref, e64_ref, y_ref, tail_scr, xc_scr, st_scr):
    for i in range(z_ref.shape[0]):
        _ssd_batch(z_ref.at[i], xbc_ref.at[i], dt_ref.at[i], cw_ref, cb_ref, dtb_ref, alog_ref,
                   dsk_ref, gn_ref, e64_ref, y_ref.at[i], tail_scr.at[i], xc_scr.at[i],
                   st_scr.at[i])


def _ssd_batch(z_ref, xbc_ref, dt_ref, cw_ref, cb_ref, dtb_ref, alog_ref, dsk_ref,
               gn_ref, e64_ref, y_ref, tail_scr, xc_scr, st_scr):
    lb = z_ref.shape[0]
    j = pl.program_id(1)

    @pl.when(j == 0)
    def _():
        tail_scr[...] = jnp.zeros_like(tail_scr)
        st_scr[...] = jnp.zeros_like(st_scr)

    xb = xbc_ref[...]
    x0 = xb.astype(F32)
    rr = lax.broadcasted_iota(jnp.int32, (lb, lb), 0)
    cc = lax.broadcasted_iota(jnp.int32, (lb, lb), 1)
    row8 = lax.broadcasted_iota(jnp.int32, (SUBLANES, 1), 0)
    tail = tail_scr[...]
    acc = cb_ref[...] + cw_ref[SSD_CONV - 1:SSD_CONV, :] * x0
    head = jnp.zeros((SUBLANES, SSD_CONV_DIM), F32)
    for k in range(1, SSD_CONV):
        wk = cw_ref[SSD_CONV - 1 - k:SSD_CONV - k, :]
        acc = acc + wk * _dot((rr - cc == k).astype(BF16), xb)
        head = head + wk * jnp.where(row8 < k, pltpu.roll(tail, k, axis=0), 0.0)
    tail_scr[...] = x0[lb - SUBLANES:, :]
    xc_scr[...] = acc * jax.nn.sigmoid(acc)
    acc0 = acc[0:SUBLANES, :] + head
    xc_scr[0:SUBLANES, :] = acc0 * jax.nn.sigmoid(acc0)

    dt_t = (dt_ref[...] + dtb_ref[...]).T[0:SSD_HEADS, :]
    dtp_t = jnp.maximum(dt_t, 0.0) + jnp.log(1.0 + jnp.exp(-jnp.abs(dt_t)))
    dtp_all = jnp.concatenate(
        [dtp_t, jnp.zeros((LANES - SSD_HEADS, lb), F32)], axis=0).T

    ri = lax.broadcasted_iota(jnp.int32, (SSD_CHUNK, SSD_CHUNK), 0)
    ci = lax.broadcasted_iota(jnp.int32, (SSD_CHUNK, SSD_CHUNK), 1)
    causal = ri >= ci
    ltri = causal.astype(BF16)
    lo_half = ci < SSD_HEAD_DIM
    a_neg = -jnp.exp(alog_ref[...])
    e64 = e64_ref[...]
    b0 = SSD_WIDTH
    c0 = SSD_WIDTH + SSD_GROUPS * SSD_STATE
    hpg = SSD_HEADS // SSD_GROUPS

    for c in range(lb // SSD_CHUNK):
        r0 = c * SSD_CHUNK
        xs = xc_scr[r0:r0 + SSD_CHUNK, 0:SSD_WIDTH]
        dtp = dtp_all[r0:r0 + SSD_CHUNK, :]
        dta = dtp * a_neg
        acum = _dot_sel_lhs(ltri, dta)
        acum_t = acum.T
        atot = acum[SSD_CHUNK - 1:SSD_CHUNK, :]
        fac = jnp.concatenate([dtp, dtp * jnp.exp(atot - acum), jnp.exp(acum)], axis=0)
        fac64 = _dot(fac.astype(BF16), e64)
        xdt = xs * fac64[0:SSD_CHUNK]
        xw = (xs * fac64[SSD_CHUNK:2 * SSD_CHUNK]).astype(BF16)
        ea64 = fac64[2 * SSD_CHUNK:]
        tail = SSD_CHUNK - SUBLANES
        etot64 = jnp.exp(_dot_sel_rhs(acum[tail:, :], e64)[SUBLANES - 1:SUBLANES, :])
        gmats = []
        for g in range(SSD_GROUPS):
            bm = xc_scr[r0:r0 + SSD_CHUNK, b0 + g * SSD_STATE:b0 + (g + 1) * SSD_STATE]
            cm = xc_scr[r0:r0 + SSD_CHUNK, c0 + g * SSD_STATE:c0 + (g + 1) * SSD_STATE].astype(BF16)
            bm_t = bm.T.astype(BF16)
            gmats.append((bm_t, cm, _dot(cm, bm_t)))
        ys = []
        for p in range(SSD_HEADS // 2):
            bm_t, cm, gm = gmats[(2 * p) // hpg]
            cols = slice(p * LANES, (p + 1) * LANES)
            xp = xdt[:, cols]
            y = None
            for hh in range(2):
                h = 2 * p + hh
                col = jnp.broadcast_to(acum[:, h:h + 1], (SSD_CHUNK, SSD_CHUNK))
                rowv = acum_t[h:h + 1, :]
                w = jnp.where(causal, gm * jnp.exp(col - rowv), 0.0).astype(BF16)
                xh = jnp.where(lo_half if hh == 0 else ~lo_half, xp, 0.0).astype(BF16)
                t = _dot(w, xh)
                y = t if y is None else y + t
            st = st_scr[p]
            y = y + _dot(cm, st.astype(BF16)) * ea64[:, cols]
            st_scr[p] = st * etot64[:, cols] + _dot(bm_t, xw[:, cols])
            ys.append(y + dsk_ref[:, cols] * xs[:, cols])
        yc = jnp.concatenate(ys, axis=1)
        zc = z_ref[r0:r0 + SSD_CHUNK, :].astype(F32)
        yc = yc * (zc * jax.nn.sigmoid(zc))
        y_ref[r0:r0 + SSD_CHUNK, :] = _rms(yc, gn_ref[...]).astype(y_ref.dtype)


def _ssd(z3, xbc3, dt3, cw, cb, dtb, alog, dsk, gn, e64, layer, lb, nb):
    b, l, _ = z3.shape
    blk = lambda w: pl.BlockSpec((nb, lb, w), lambda i, j: (i, j, 0))
    ls = lambda a: _layer_spec(a.shape, layer, 2)
    return pl.pallas_call(
        _ssd_kernel,
        grid=(b // nb, l // lb),
        in_specs=[blk(SSD_WIDTH), blk(SSD_CONV_DIM), blk(LANES), ls(cw), ls(cb),
                  ls(dtb), ls(alog), ls(dsk), ls(gn),
                  pl.BlockSpec(e64.shape, lambda i, j: (0, 0))],
        out_specs=blk(SSD_WIDTH),
        out_shape=jax.ShapeDtypeStruct((b, l, SSD_WIDTH), BF16),
        scratch_shapes=[pltpu.VMEM((nb, SUBLANES, SSD_CONV_DIM), F32),
                        pltpu.VMEM((nb, lb, SSD_CONV_DIM), F32),
                        pltpu.VMEM((nb, SSD_HEADS // 2, SSD_STATE, LANES), F32)],
        compiler_params=_params("parallel", "arbitrary"),
        name="ssd",
    )(z3, xbc3, dt3, cw, cb, dtb, alog, dsk, gn, e64)


FFN_CHUNK = MXU_WIDTH


def _gelu_tanh(v):
    return 0.5 * v * (1.0 + jnp.tanh(0.7978845608028654 * (v + 0.044715 * (v * v * v))))


def _post_kernel(x_ref, ys_ref, u_ref, yb_ref, d5_ref, wglu_ref, bglu_ref, g5_ref,
                 woa_ref, wob_ref, gf_ref, wg_ref, wu_ref, wd_ref, gfin_ref, o_ref, y_scr,
                 *, final):
    bsz, tok, _ = x_ref.shape
    for b in range(bsz):
        for c in range(tok // S5_CHUNK):
            r0 = b * tok + c * S5_CHUNK
            for j in range(D_MODEL // LANES):
                cols = slice(j * LANES, (j + 1) * LANES)
                rows = _s5_rows(c, j, b)
                y_scr[r0:r0 + S5_CHUNK, cols] = ys_ref[rows, :] + d5_ref[:, cols] * u_ref[rows, :]
    g = _gelu_tanh(y_scr[...])
    ya = g * jax.nn.sigmoid(_dot(g.astype(BF16), wglu_ref[...]) + bglu_ref[...])
    ya = _rms(ya, g5_ref[...]).astype(BF16)
    yb = yb_ref[...].reshape(bsz * tok, SSD_WIDTH)
    x = x_ref[...].reshape(bsz * tok, D_MODEL) + _dot(ya, woa_ref[...]) + _dot(yb, wob_ref[...])
    hb = _rms(x, gf_ref[...]).astype(BF16)
    y_scr[...] = x

    def ffn_chunk(k, carry):
        sl = pl.ds(pl.multiple_of(k * FFN_CHUNK, FFN_CHUNK), FFN_CHUNK)
        gate = _dot(hb, wg_ref[:, sl])
        act = (gate * jax.nn.sigmoid(gate)) * _dot(hb, wu_ref[:, sl])
        y_scr[...] += _dot(act.astype(BF16), wd_ref[sl, :])
        return carry

    lax.fori_loop(0, FFN_HIDDEN // FFN_CHUNK, ffn_chunk, 0, unroll=4)
    out = y_scr[...]
    if final:
        out = _rms(out, gfin_ref[...])
    o_ref[...] = out.reshape(o_ref.shape)


def _post(x3, ys, u, yb3, d5, wglu, bglu, g5, wo, gf, wg, wu, wd, gfin, layer, tok, final):
    b, seq, _ = x3.shape
    nat = pl.BlockSpec((b, tok, D_MODEL), lambda i: (0, i, 0))
    s5 = pl.BlockSpec((tok * b * D_MODEL // LANES, LANES), lambda i: (i, 0))
    one = pl.Buffered(1)
    ls = lambda a: pl.BlockSpec((None,) + a.shape[1:], lambda i: (layer, 0, 0), pipeline_mode=one)
    wo_half = lambda k: pl.BlockSpec((None, D_MODEL, D_MODEL), lambda i: (layer, k, 0),
                                     pipeline_mode=one)
    gfin_spec = pl.BlockSpec(gfin.shape, lambda i: (0, 0), pipeline_mode=one)
    return pl.pallas_call(
        functools.partial(_post_kernel, final=final),
        grid=(seq // tok,),
        in_specs=[nat, s5, s5, nat, ls(d5), ls(wglu), ls(bglu), ls(g5), wo_half(0), wo_half(1),
                  ls(gf), ls(wg), ls(wu), ls(wd), gfin_spec],
        out_specs=nat,
        out_shape=jax.ShapeDtypeStruct(x3.shape, F32),
        scratch_shapes=[pltpu.VMEM((b * tok, D_MODEL), F32)],
        compiler_params=_params("parallel"),
        name="post",
    )(x3, ys, u, yb3, d5, wglu, bglu, g5, wo, wo, gf, wg, wu, wd, gfin)


def kernel(x, norm_mix, w_in, s5_lam_re, s5_lam_im, s5_log_step, s5_b_re, s5_b_im, s5_c_re, s5_c_im, s5_d, s5_w_glu, s5_b_glu, s5_norm, ssd_conv_w, ssd_conv_b, ssd_dt_bias, ssd_a_log, ssd_d, ssd_norm, w_out, norm_ffn, w_gate, w_up, w_down, norm_final):
    bsz, seq, _ = x.shape
    depth = w_in.shape[0]
    assert bsz == SUBLANES and seq % S5_CHUNK == 0 and seq % SSD_CHUNK == 0
    nc5 = seq // S5_CHUNK
    tok = 2 * S5_CHUNK
    lb = min(2 * SSD_CHUNK, seq)
    dg = depth * S5_GROUPS
    rows = lambda a: a.reshape(depth, 1, -1).astype(F32)

    w_all = _cast_win(jnp.swapaxes(w_in, 1, 2), 2 * MXU_WIDTH)
    wglu = _cast_bf16(s5_w_glu, D_MODEL // 2)
    wo = _cast_bf16(w_out, D_MODEL)
    wg = _cast_bf16(w_gate, D_MODEL // 2)
    wup = _cast_bf16(w_up, D_MODEL // 2)
    wd = _cast_bf16(w_down, FFN_HIDDEN // 4)

    dbl = lambda a: jnp.concatenate([a, a], axis=-1)
    mt, bt, ct, apow = _s5_prep(
        s5_log_step.reshape(dg, 1, 1),
        s5_lam_re.reshape(dg, S5_STATE, 1), s5_lam_im.reshape(dg, S5_STATE, 1),
        dbl(s5_lam_re).reshape(dg, 1, 2 * S5_STATE), dbl(s5_lam_im).reshape(dg, 1, 2 * S5_STATE),
        dbl(s5_c_re).reshape(dg, S5_GROUP, 2 * S5_STATE),
        dbl(s5_c_im).reshape(dg, S5_GROUP, 2 * S5_STATE),
        s5_b_re.reshape(dg, S5_STATE, S5_GROUP), s5_b_im.reshape(dg, S5_STATE, S5_GROUP))

    head_of_col = jnp.arange(SSD_WIDTH) // SSD_HEAD_DIM
    e64 = (jnp.arange(LANES)[:, None] == head_of_col[None, :]).astype(BF16)
    pad_h = lambda a: jnp.pad(a.astype(F32), ((0, 0), (0, LANES - SSD_HEADS))).reshape(
        depth, 1, LANES)
    dtb, alog = pad_h(ssd_dt_bias), pad_h(ssd_a_log)
    dsk = rows(jnp.repeat(ssd_d, SSD_HEAD_DIM, axis=-1))
    g_mix, g_s5, g_ssd, g_ffn = rows(norm_mix), rows(s5_norm), rows(ssd_norm), rows(norm_ffn)
    d5, bglu, cb = rows(s5_d), rows(s5_b_glu), rows(ssd_conv_b)
    cw = ssd_conv_w.astype(F32)
    gfin = norm_final.reshape(1, -1).astype(F32)

    s5_shape = (nc5, D_MODEL // LANES, S5_CHUNK, bsz, LANES)
    for i in range(depth):
        u, z, xbc, dt = _in_proj(x, g_mix, w_all, i, 2 * tok)
        ys = _s5_core(u.reshape(s5_shape), mt, bt, ct, apow, i).reshape(u.shape)
        yb = _ssd(z, xbc, dt, cw, cb, dtb, alog, dsk, g_ssd, e64, i, lb, 4)
        x = _post(x, ys, u, yb, d5, wglu, bglu, g_s5, wo, g_ffn, wg, wup, wd, gfin,
                  i, tok, i == depth - 1)
    return x
```

```python
import functools

import jax
import jax.numpy as jnp
from jax import lax
from jax.experimental import pallas as pl
from jax.experimental.pallas import tpu as pltpu

F32 = jnp.float32
BF16 = jnp.bfloat16
EPS = 1e-6

D_MODEL = 1024
S5_GROUP = 16
S5_GROUPS = 64
S5_STATE = 64
S5_CHUNK = 32
S5_FLAT = S5_CHUNK * S5_GROUP
SSD_HEAD_DIM = 64
SSD_HEADS = 16
SSD_GROUPS = 2
SSD_STATE = 128
SSD_CONV = 4
SSD_CHUNK = 128
SSD_WIDTH = SSD_HEADS * SSD_HEAD_DIM
SSD_CONV_DIM = SSD_WIDTH + 2 * SSD_GROUPS * SSD_STATE
FFN_HIDDEN = 2816
LANES = 128
SUBLANES = 8
MXU_WIDTH = 256
S5_GPB = LANES // S5_GROUP
S5_BUFS = 3
VMEM_LIMIT = 56 * 1024 * 1024
IN_U = D_MODEL
IN_Z = IN_U + SSD_WIDTH
IN_X = IN_Z + SSD_CONV_DIM


def _params(*sem):
    return pltpu.CompilerParams(dimension_semantics=sem, vmem_limit_bytes=VMEM_LIMIT)


def _rms(x, gain):
    return x * lax.rsqrt(jnp.mean(x * x, axis=-1, keepdims=True) + EPS) * gain


def _dot(a, b):
    return jnp.dot(a, b, preferred_element_type=F32)


def _split(v, terms):
    out = []
    for _ in range(terms - 1):
        hi = v.astype(BF16)
        out.append(hi)
        v = v - hi.astype(F32)
    out.append(v.astype(BF16))
    return out


def _dot_sel_rhs(v, sel, terms=3):
    return functools.reduce(jnp.add, [_dot(t, sel) for t in _split(v, terms)])


def _dot_sel_lhs(sel, v, terms=3):
    return functools.reduce(jnp.add, [_dot(sel, t) for t in _split(v, terms)])


def _dot_f32(a, b, terms=3):
    a_t = _split(a, terms)
    b_t = _split(b, terms)
    return functools.reduce(
        jnp.add, [_dot(a_t[i], b_t[j]) for i in range(terms) for j in range(terms - i)])


def _layer_spec(shape, layer, n_grid, col=0):
    zeros = (0,) * (len(shape) - 2)
    if n_grid == 1:
        return pl.BlockSpec((None,) + shape[1:], lambda i: (layer,) + zeros + (col,))
    return pl.BlockSpec((None,) + shape[1:], lambda i, j: (layer,) + zeros + (col,))


def _cast_kernel(w_ref, o_ref):
    o_ref[...] = w_ref[...].astype(o_ref.dtype)


def _cast_bf16(w, rb):
    depth, rows, cols = w.shape
    spec = pl.BlockSpec((1, rb, cols), lambda i, j: (i, j, 0))
    return pl.pallas_call(
        _cast_kernel, grid=(depth, rows // rb), in_specs=[spec], out_specs=spec,
        out_shape=jax.ShapeDtypeStruct(w.shape, BF16),
        compiler_params=_params("parallel", "parallel"), name="cast",
    )(w)


def _cast_win_kernel(w_ref, o_ref, *, n_out):
    rb = w_ref.shape[1]
    valid = n_out - pl.program_id(1) * rb
    row = lax.broadcasted_iota(jnp.int32, (rb, 1), 0)
    o_ref[0] = jnp.where(row < valid, w_ref[0], 0.0).T.astype(o_ref.dtype)


def _cast_win(w_in_t, rb):
    depth, n_out, d = w_in_t.shape
    nblk = pl.cdiv(n_out, rb)
    return pl.pallas_call(
        functools.partial(_cast_win_kernel, n_out=n_out), grid=(depth, nblk),
        in_specs=[pl.BlockSpec((1, rb, d), lambda i, j: (i, j, 0))],
        out_specs=pl.BlockSpec((1, d, rb), lambda i, j: (i, 0, j)),
        out_shape=jax.ShapeDtypeStruct((depth, d, nblk * rb), BF16),
        compiler_params=_params("parallel", "parallel"), name="cast_win",
    )(w_in_t)


def _s5_rows(c, j, b):
    return pl.ds(((c * (D_MODEL // LANES) + j) * S5_CHUNK) * SUBLANES + b, S5_CHUNK,
                 stride=SUBLANES)


def _inproj_kernel(x_ref, g_ref, wu_ref, wz_ref, wx0_ref, wx1_ref, wx2_ref, wdt_ref,
                   u_ref, z_ref, xbc_ref, dt_ref):
    bsz, tok, _ = x_ref.shape
    hb = _rms(x_ref[...], g_ref[...]).astype(BF16).reshape(bsz * tok, D_MODEL)
    u = _dot(hb, wu_ref[...])
    for b in range(bsz):
        for c in range(tok // S5_CHUNK):
            r0 = b * tok + c * S5_CHUNK
            for j in range(D_MODEL // LANES):
                u_ref[_s5_rows(c, j, b), :] = u[r0:r0 + S5_CHUNK, j * LANES:(j + 1) * LANES]
    z_ref[...] = _dot(hb, wz_ref[...]).astype(z_ref.dtype).reshape(z_ref.shape)
    xbc = jnp.concatenate([_dot(hb, w[...]).astype(xbc_ref.dtype)
                           for w in (wx0_ref, wx1_ref, wx2_ref)], axis=1)
    xbc_ref[...] = xbc.reshape(xbc_ref.shape)
    dt_ref[...] = _dot(hb, wdt_ref[...]).reshape(dt_ref.shape)


def _in_proj(x3, gain, w_all, layer, tok):
    b, seq, _ = x3.shape
    nat = lambda w: pl.BlockSpec((b, tok, w), lambda i: (0, i, 0))
    win = lambda width, start: pl.BlockSpec((None, D_MODEL, width),
                                            lambda i: (layer, 0, start // width))
    xw = SSD_CONV_DIM // 3
    return pl.pallas_call(
        _inproj_kernel,
        grid=(seq // tok,),
        in_specs=[nat(D_MODEL), _layer_spec(gain.shape, layer, 1), win(D_MODEL, 0),
                  win(SSD_WIDTH, IN_U), win(xw, IN_Z), win(xw, IN_Z + xw), win(xw, IN_Z + 2 * xw),
                  win(LANES, IN_X)],
        out_specs=[pl.BlockSpec((tok * b * D_MODEL // LANES, LANES), lambda i: (i, 0)),
                   nat(SSD_WIDTH), nat(SSD_CONV_DIM), nat(LANES)],
        out_shape=[jax.ShapeDtypeStruct((seq * b * D_MODEL // LANES, LANES), F32),
                   jax.ShapeDtypeStruct((b, seq, SSD_WIDTH), BF16),
                   jax.ShapeDtypeStruct((b, seq, SSD_CONV_DIM), BF16),
                   jax.ShapeDtypeStruct((b, seq, LANES), F32)],
        compiler_params=_params("parallel"),
        name="in_proj",
    )(x3, gain, *([w_all] * 6))


def _s5_prep_group(q, e_h, e_rev, ls_ref, lrc_ref, lic_ref, lrr_ref, lir_ref, cre_ref, cim_ref,
                   bre_ref, bim_ref, mt_ref, bt_ref, ct_ref, apow_ref):
    step = jnp.exp(ls_ref[q])
    lrc = lrc_ref[q]
    lic = lic_ref[q]
    lrr = lrr_ref[q]
    lir = lir_ref[q]
    p2 = 2 * S5_STATE

    n_pow = S5_CHUNK + SUBLANES
    st = step * lax.broadcasted_iota(jnp.int32, (n_pow, 1), 0).astype(F32)
    mag = jnp.exp(lrr * st)
    pre_k = mag * jnp.cos(lir * st)
    pim_k = mag * jnp.sin(lir * st)

    def to_cols(p_k):
        return jnp.concatenate(
            [p_k, jnp.zeros((LANES - n_pow, p2), F32)], axis=0).T[0:S5_STATE, :]

    pre_c = to_cols(pre_k)
    pim_c = to_cols(pim_k)
    are = pre_c[:, 1:2]
    aim = pim_c[:, 1:2]
    den = lrc * lrc + lic * lic
    nr = are - 1.0
    cfr = (nr * lrc + aim * lic) / den
    cfi = (aim * lrc - nr * lic) / den
    bre = bre_ref[q]
    bim = bim_ref[q]
    bbr = cfr * bre - cfi * bim
    bbi = cfr * bim + cfi * bre

    lane = lax.broadcasted_iota(jnp.int32, (1, S5_FLAT), 1)
    btr = _dot_sel_rhs(bbr, e_h, 2)
    bti = _dot_sel_rhs(bbi, e_h, 2)
    prr = _dot_sel_rhs(pre_c, e_rev, 2)
    pri = _dot_sel_rhs(pim_c, e_rev, 2)
    wre = prr * btr - pri * bti
    wim = prr * bti + pri * btr
    bt_ref[q] = jnp.concatenate([wre, wim, wim, wre], axis=0).astype(bt_ref.dtype)

    lane2 = lax.broadcasted_iota(jnp.int32, (1, p2), 1)
    first = lane2 < S5_STATE
    c1 = cre_ref[q]
    c2 = cim_ref[q]
    lhs = jnp.where(first, c1, c2)
    rev = _dot_f32(lhs, jnp.concatenate([wre, -wim], axis=0), 2)
    for t in range(S5_CHUNK):
        rows = slice(t * S5_GROUP, (t + 1) * S5_GROUP)
        shift = (S5_FLAT - (S5_CHUNK - 1 - t) * S5_GROUP) % S5_FLAT
        blk = rev if shift == 0 else pltpu.roll(rev, shift, axis=1)
        if t < S5_CHUNK - 1:
            blk = jnp.where(lane < (t + 1) * S5_GROUP, blk, 0.0)
        mt_ref[q, rows, :] = blk.astype(mt_ref.dtype)

    pre_r = pre_k[1:S5_CHUNK + 1, :]
    pim_r = pim_k[1:S5_CHUNK + 1, :]
    q1 = jnp.where(first, pre_r, pim_r)
    q2 = jnp.where(first, pim_r, pre_r)
    s1 = jnp.where(first, 1.0, -1.0).astype(F32)
    for t in range(S5_CHUNK):
        rows = slice(t * S5_GROUP, (t + 1) * S5_GROUP)
        ct_ref[q, rows, :] = (s1 * (c1 * q1[t:t + 1, :]) - c2 * q2[t:t + 1, :]).astype(ct_ref.dtype)
    atr = pre_r[S5_CHUNK - 1:S5_CHUNK, :]
    a2 = -s1 * pim_r[S5_CHUNK - 1:S5_CHUNK, :]
    apow_ref[q] = jnp.concatenate(
        [atr, a2, -a2, jnp.zeros((SUBLANES - 3, p2), F32)], axis=0)


def _s5_prep_kernel(*refs):
    h_of_lane = lax.broadcasted_iota(jnp.int32, (S5_GROUP, S5_FLAT), 1) % S5_GROUP
    e_h = (h_of_lane == lax.broadcasted_iota(jnp.int32, (S5_GROUP, S5_FLAT), 0)).astype(BF16)
    k_row = lax.broadcasted_iota(jnp.int32, (LANES, S5_FLAT), 0)
    j_lane = lax.broadcasted_iota(jnp.int32, (LANES, S5_FLAT), 1) // S5_GROUP
    e_rev = (k_row == (S5_CHUNK - 1) - j_lane).astype(BF16)
    for q in range(refs[0].shape[0]):
        _s5_prep_group(q, e_h, e_rev, *refs)


def _s5_prep(ls, lrc, lic, lrr, lir, cre, cim, bre, bim):
    dg = ls.shape[0]
    gps = 16
    spec = lambda a: pl.BlockSpec((gps,) + a.shape[1:], lambda i: (i, 0, 0))
    ins = (ls, lrc, lic, lrr, lir, cre, cim, bre, bim)
    shapes = [(dg, S5_FLAT, S5_FLAT), (dg, 4 * S5_STATE, S5_FLAT),
              (dg, S5_FLAT, 2 * S5_STATE), (dg, SUBLANES, 2 * S5_STATE)]
    dtypes = [BF16, BF16, BF16, F32]
    return pl.pallas_call(
        _s5_prep_kernel,
        grid=(dg // gps,),
        in_specs=[spec(a) for a in ins],
        out_specs=[pl.BlockSpec((gps,) + s[1:], lambda i: (i, 0, 0)) for s in shapes],
        out_shape=[jax.ShapeDtypeStruct(s, d) for s, d in zip(shapes, dtypes)],
        compiler_params=_params("parallel"),
        name="s5_prep",
    )(*ins)


def _s5_core_kernel(u_hbm, mt_ref, bt_ref, ct_ref, ap_ref, y_hbm,
                    io_buf, zt_scr, e_scr, sp_scr, yt_scr, sem_in, sem_out, *, batch):
    ng = mt_ref.shape[0]
    nchunk = u_hbm.shape[0]
    cols = nchunk * batch
    half = S5_FLAT // 2
    j = pl.program_id(0)
    last = pl.num_programs(0) - 1
    nbuf = io_buf.shape[0]
    slot = j % nbuf
    nxt = (j + 1) % nbuf

    def in_copy(step, s, t):
        return pltpu.make_async_copy(u_hbm.at[:, step, t], io_buf.at[s, t], sem_in.at[s, t])

    def out_copy(step, s, t):
        return pltpu.make_async_copy(io_buf.at[s, t], y_hbm.at[:, step, t], sem_out.at[s, t])

    @pl.when(j == 0)
    def _():
        for t in range(S5_CHUNK):
            in_copy(j, slot, t).start()

    @pl.when(jnp.logical_and(j >= nbuf - 1, j < last))
    def _():
        for t in range(S5_CHUNK):
            out_copy(j + 1 - nbuf, nxt, t).wait()

    @pl.when(j < last)
    def _():
        for t in range(S5_CHUNK):
            in_copy(j + 1, nxt, t).start()

    for t in range(S5_CHUNK):
        in_copy(j, slot, t).wait()
    for t in range(S5_CHUNK):
        blk_t = io_buf[slot, t].reshape(cols, LANES).T.astype(BF16)
        for g in range(ng):
            zt_scr[g, t * S5_GROUP:(t + 1) * S5_GROUP, :] = blk_t[g * S5_GROUP:(g + 1) * S5_GROUP, :]

    for g in range(ng):
        e_scr[g] = _dot(bt_ref[g], zt_scr[g]).T

    def body(c, carry):
        r = pl.multiple_of(c * batch, batch)
        out = []
        for g in range(ng):
            s, sw = carry[g]
            sp_scr[g, pl.ds(r, batch), :] = s
            e = e_scr[g, pl.ds(r, batch), :]
            a1 = ap_ref[g, 0:1, :]
            a2 = ap_ref[g, 1:2, :]
            a2w = ap_ref[g, 2:3, :]
            out.append((a1 * s + a2 * sw + e[:, :LANES], a1 * sw + a2w * s + e[:, LANES:]))
        return tuple(out)

    zero = jnp.zeros((batch, 2 * S5_STATE), F32)
    lax.fori_loop(0, nchunk, body, tuple((zero, zero) for _ in range(ng)))

    for g in range(ng):
        y_off = lax.dot_general(ct_ref[g], sp_scr[g].astype(BF16), (((1,), (1,)), ((), ())),
                                preferred_element_type=F32)
        top = _dot(mt_ref[g, :half, :half], zt_scr[g, :half, :])
        bot = _dot(mt_ref[g, half:, :], zt_scr[g])
        yt_scr[g, :half, :] = (top + y_off[:half]).astype(yt_scr.dtype)
        yt_scr[g, half:, :] = (bot + y_off[half:]).astype(yt_scr.dtype)

    for t in range(S5_CHUNK):
        rows = slice(t * S5_GROUP, (t + 1) * S5_GROUP)
        blk = jnp.concatenate([yt_scr[g, rows, :] for g in range(ng)], axis=0)
        io_buf[slot, t] = blk.astype(F32).T.reshape(nchunk, batch, LANES)
        out_copy(j, slot, t).start()

    @pl.when(j == last)
    def _():
        for back in range(nbuf):
            for t in range(S5_CHUNK):
                out_copy(j - back, (j - back) % nbuf, t).wait()


def _s5_core(u5, mt, bt, ct, apow, layer):
    nchunk, nblk, _, batch, _ = u5.shape
    cols = nchunk * batch
    ng = S5_GPB
    op = lambda a: pl.BlockSpec((ng,) + a.shape[1:], lambda i: (layer * nblk + i, 0, 0))
    hbm = pl.BlockSpec(memory_space=pl.ANY)
    return pl.pallas_call(
        functools.partial(_s5_core_kernel, batch=batch),
        grid=(nblk,),
        in_specs=[hbm, op(mt), op(bt), op(ct), op(apow)],
        out_specs=hbm,
        out_shape=jax.ShapeDtypeStruct(u5.shape, F32),
        scratch_shapes=[pltpu.VMEM((S5_BUFS, S5_CHUNK, nchunk, batch, LANES), F32),
                        pltpu.VMEM((ng, S5_FLAT, cols), BF16),
                        pltpu.VMEM((ng, cols, 4 * S5_STATE), F32),
                        pltpu.VMEM((ng, cols, 2 * S5_STATE), F32),
                        pltpu.VMEM((ng, S5_FLAT, cols), BF16),
                        pltpu.SemaphoreType.DMA((S5_BUFS, S5_CHUNK)),
                        pltpu.SemaphoreType.DMA((S5_BUFS, S5_CHUNK))],
        compiler_params=_params("arbitrary"),
        name="s5_core",
    )(u5, mt, bt, ct, apow)


def _ssd_kernel(z_ref, xbc_ref, dt_ref, cw_ref, cb_ref, dtb_ref, alog_ref, dsk_ref,
                gn_ref, e64_ref, y_ref, tail_scr, xc_scr, st_scr):
    for i in range(z_ref.shape[0]):
        _ssd_batch(z_ref.at[i], xbc_ref.at[i], dt_ref.at[i], cw_ref, cb_ref, dtb_ref, alog_ref,
                   dsk_ref, gn_ref, e64_ref, y_ref.at[i], tail_scr.at[i], xc_scr.at[i],
                   st_scr.at[i])


def _ssd_batch(z_ref, xbc_ref, dt_ref, cw_ref, cb_ref, dtb_ref, alog_ref, dsk_ref,
               gn_ref, e64_ref, y_ref, tail_scr, xc_scr, st_scr):
    lb = z_ref.shape[0]
    j = pl.program_id(1)

    @pl.when(j == 0)
    def _():
        tail_scr[...] = jnp.zeros_like(tail_scr)
        st_scr[...] = jnp.zeros_like(st_scr)

    xb = xbc_ref[...]
    x0 = xb.astype(F32)
    rr = lax.broadcasted_iota(jnp.int32, (lb, lb), 0)
    cc = lax.broadcasted_iota(jnp.int32, (lb, lb), 1)
    row8 = lax.broadcasted_iota(jnp.int32, (SUBLANES, 1), 0)
    tail = tail_scr[...]
    acc = cb_ref[...] + cw_ref[SSD_CONV - 1:SSD_CONV, :] * x0
    head = jnp.zeros((SUBLANES, SSD_CONV_DIM), F32)
    for k in range(1, SSD_CONV):
        wk = cw_ref[SSD_CONV - 1 - k:SSD_CONV - k, :]
        acc = acc + wk * _dot((rr - cc == k).astype(BF16), xb)
        head = head + wk * jnp.where(row8 < k, pltpu.roll(tail, k, axis=0), 0.0)
    tail_scr[...] = x0[lb - SUBLANES:, :]
    xc_scr[...] = acc * jax.nn.sigmoid(acc)
    acc0 = acc[0:SUBLANES, :] + head
    xc_scr[0:SUBLANES, :] = acc0 * jax.nn.sigmoid(acc0)

    dt_t = (dt_ref[...] + dtb_ref[...]).T[0:SSD_HEADS, :]
    dtp_t = jnp.maximum(dt_t, 0.0) + jnp.log(1.0 + jnp.exp(-jnp.abs(dt_t)))
    dtp_all = jnp.concatenate(
        [dtp_t, jnp.zeros((LANES - SSD_HEADS, lb), F32)], axis=0).T

    ri = lax.broadcasted_iota(jnp.int32, (SSD_CHUNK, SSD_CHUNK), 0)
    ci = lax.broadcasted_iota(jnp.int32, (SSD_CHUNK, SSD_CHUNK), 1)
    causal = ri >= ci
    ltri = causal.astype(BF16)
    lo_half = ci < SSD_HEAD_DIM
    a_neg = -jnp.exp(alog_ref[...])
    e64 = e64_ref[...]
    b0 = SSD_WIDTH
    c0 = SSD_WIDTH + SSD_GROUPS * SSD_STATE
    hpg = SSD_HEADS // SSD_GROUPS

    for c in range(lb // SSD_CHUNK):
        r0 = c * SSD_CHUNK
        xs = xc_scr[r0:r0 + SSD_CHUNK, 0:SSD_WIDTH]
        dtp = dtp_all[r0:r0 + SSD_CHUNK, :]
        dta = dtp * a_neg
        acum = _dot_sel_lhs(ltri, dta)
        acum_t = acum.T
        atot = acum[SSD_CHUNK - 1:SSD_CHUNK, :]
        fac = jnp.concatenate([dtp, dtp * jnp.exp(atot - acum), jnp.exp(acum)], axis=0)
        fac64 = _dot(fac.astype(BF16), e64)
        xdt = xs * fac64[0:SSD_CHUNK]
        xw = (xs * fac64[SSD_CHUNK:2 * SSD_CHUNK]).astype(BF16)
        ea64 = fac64[2 * SSD_CHUNK:]
        tail = SSD_CHUNK - SUBLANES
        etot64 = jnp.exp(_dot_sel_rhs(acum[tail:, :], e64)[SUBLANES - 1:SUBLANES, :])
        gmats = []
        for g in range(SSD_GROUPS):
            bm = xc_scr[r0:r0 + SSD_CHUNK, b0 + g * SSD_STATE:b0 + (g + 1) * SSD_STATE]
            cm = xc_scr[r0:r0 + SSD_CHUNK, c0 + g * SSD_STATE:c0 + (g + 1) * SSD_STATE].astype(BF16)
            bm_t = bm.T.astype(BF16)
            gmats.append((bm_t, cm, _dot(cm, bm_t)))
        ys = []
        for p in range(SSD_HEADS // 2):
            bm_t, cm, gm = gmats[(2 * p) // hpg]
            cols = slice(p * LANES, (p + 1) * LANES)
            xp = xdt[:, cols]
            y = None
            for hh in range(2):
                h = 2 * p + hh
                col = jnp.broadcast_to(acum[:, h:h + 1], (SSD_CHUNK, SSD_CHUNK))
                rowv = acum_t[h:h + 1, :]
                w = jnp.where(causal, gm * jnp.exp(col - rowv), 0.0).astype(BF16)
                xh = jnp.where(lo_half if hh == 0 else ~lo_half, xp, 0.0).astype(BF16)
                t = _dot(w, xh)
                y = t if y is None else y + t
            st = st_scr[p]
            y = y + _dot(cm, st.astype(BF16)) * ea64[:, cols]
            st_scr[p] = st * etot64[:, cols] + _dot(bm_t, xw[:, cols])
            ys.append(y + dsk_ref[:, cols] * xs[:, cols])
        yc = jnp.concatenate(ys, axis=1)
        zc = z_ref[r0:r0 + SSD_CHUNK, :].astype(F32)
        yc = yc * (zc * jax.nn.sigmoid(zc))
        y_ref[r0:r0 + SSD_CHUNK, :] = _rms(yc, gn_ref[...]).astype(y_ref.dtype)


def _ssd(z3, xbc3, dt3, cw, cb, dtb, alog, dsk, gn, e64, layer, lb, nb):
    b, l, _ = z3.shape
    blk = lambda w: pl.BlockSpec((nb, lb, w), lambda i, j: (i, j, 0))
    ls = lambda a: _layer_spec(a.shape, layer, 2)
    return pl.pallas_call(
        _ssd_kernel,
        grid=(b // nb, l // lb),
        in_specs=[blk(SSD_WIDTH), blk(SSD_CONV_DIM), blk(LANES), ls(cw), ls(cb),
                  ls(dtb), ls(alog), ls(dsk), ls(gn),
                  pl.BlockSpec(e64.shape, lambda i, j: (0, 0))],
        out_specs=blk(SSD_WIDTH),
        out_shape=jax.ShapeDtypeStruct((b, l, SSD_WIDTH), BF16),
        scratch_shapes=[pltpu.VMEM((nb, SUBLANES, SSD_CONV_DIM), F32),
                        pltpu.VMEM((nb, lb, SSD_CONV_DIM), F32),
                        pltpu.VMEM((nb, SSD_HEADS // 2, SSD_STATE, LANES), F32)],
        compiler_params=_params("parallel", "arbitrary"),
        name="ssd",
    )(z3, xbc3, dt3, cw, cb, dtb, alog, dsk, gn, e64)


FFN_CHUNK = MXU_WIDTH


def _gelu_tanh(v):
    return 0.5 * v * (1.0 + jnp.tanh(0.7978845608028654 * (v + 0.044715 * (v * v * v))))


def _post_kernel(x_ref, ys_ref, u_ref, yb_ref, d5_ref, wglu_ref, bglu_ref, g5_ref,
                 woa_ref, wob_ref, gf_ref, wg_ref, wu_ref, wd_ref, gfin_ref, o_ref, y_scr,
                 *, final):
    bsz, tok, _ = x_ref.shape
    for b in range(bsz):
        for c in range(tok // S5_CHUNK):
            r0 = b * tok + c * S5_CHUNK
            for j in range(D_MODEL // LANES):
                cols = slice(j * LANES, (j + 1) * LANES)
                rows = _s5_rows(c, j, b)
                y_scr[r0:r0 + S5_CHUNK, cols] = ys_ref[rows, :] + d5_ref[:, cols] * u_ref[rows, :]
    g = _gelu_tanh(y_scr[...])
    ya = g * jax.nn.sigmoid(_dot(g.astype(BF16), wglu_ref[...]) + bglu_ref[...])
    ya = _rms(ya, g5_ref[...]).astype(BF16)
    yb = yb_ref[...].reshape(bsz * tok, SSD_WIDTH)
    x = x_ref[...].reshape(bsz * tok, D_MODEL) + _dot(ya, woa_ref[...]) + _dot(yb, wob_ref[...])
    hb = _rms(x, gf_ref[...]).astype(BF16)
    y_scr[...] = x

    def ffn_chunk(k, carry):
        sl = pl.ds(pl.multiple_of(k * FFN_CHUNK, FFN_CHUNK), FFN_CHUNK)
        gate = _dot(hb, wg_ref[:, sl])
        act = (gate * jax.nn.sigmoid(gate)) * _dot(hb, wu_ref[:, sl])
        y_scr[...] += _dot(act.astype(BF16), wd_ref[sl, :])
        return carry

    lax.fori_loop(0, FFN_HIDDEN // FFN_CHUNK, ffn_chunk, 0, unroll=4)
    out = y_scr[...]
    if final:
        out = _rms(out, gfin_ref[...])
    o_ref[...] = out.reshape(o_ref.shape)


def _post(x3, ys, u, yb3, d5, wglu, bglu, g5, wo, gf, wg, wu, wd, gfin, layer, tok, final):
    b, seq, _ = x3.shape
    nat = pl.BlockSpec((b, tok, D_MODEL), lambda i: (0, i, 0))
    s5 = pl.BlockSpec((tok * b * D_MODEL // LANES, LANES), lambda i: (i, 0))
    one = pl.Buffered(1)
    ls = lambda a: pl.BlockSpec((None,) + a.shape[1:], lambda i: (layer, 0, 0), pipeline_mode=one)
    wo_half = lambda k: pl.BlockSpec((None, D_MODEL, D_MODEL), lambda i: (layer, k, 0),
                                     pipeline_mode=one)
    gfin_spec = pl.BlockSpec(gfin.shape, lambda i: (0, 0), pipeline_mode=one)
    return pl.pallas_call(
        functools.partial(_post_kernel, final=final),
        grid=(seq // tok,),
        in_specs=[nat, s5, s5, nat, ls(d5), ls(wglu), ls(bglu), ls(g5), wo_half(0), wo_half(1),
                  ls(gf), ls(wg), ls(wu), ls(wd), gfin_spec],
        out_specs=nat,
        out_shape=jax.ShapeDtypeStruct(x3.shape, F32),
        scratch_shapes=[pltpu.VMEM((b * tok, D_MODEL), F32)],
        compiler_params=_params("parallel"),
        name="post",
    )(x3, ys, u, yb3, d5, wglu, bglu, g5, wo, wo, gf, wg, wu, wd, gfin)


def kernel(x, norm_mix, w_in, s5_lam_re, s5_lam_im, s5_log_step, s5_b_re, s5_b_im, s5_c_re, s5_c_im, s5_d, s5_w_glu, s5_b_glu, s5_norm, ssd_conv_w, ssd_conv_b, ssd_dt_bias, ssd_a_log, ssd_d, ssd_norm, w_out, norm_ffn, w_gate, w_up, w_down, norm_final):
    bsz, seq, _ = x.shape
    depth = w_in.shape[0]
    assert bsz == SUBLANES and seq % S5_CHUNK == 0 and seq % SSD_CHUNK == 0
    nc5 = seq // S5_CHUNK
    tok = 2 * S5_CHUNK
    lb = min(2 * SSD_CHUNK, seq)
    dg = depth * S5_GROUPS
    rows = lambda a: a.reshape(depth, 1, -1).astype(F32)

    w_all = _cast_win(jnp.swapaxes(w_in, 1, 2), 2 * MXU_WIDTH)
    wglu = _cast_bf16(s5_w_glu, D_MODEL // 2)
    wo = _cast_bf16(w_out, D_MODEL)
    wg = _cast_bf16(w_gate, D_MODEL // 2)
    wup = _cast_bf16(w_up, D_MODEL // 2)
    wd = _cast_bf16(w_down, FFN_HIDDEN // 4)

    dbl = lambda a: jnp.concatenate([a, a], axis=-1)
    mt, bt, ct, apow = _s5_prep(
        s5_log_step.reshape(dg, 1, 1),
        s5_lam_re.reshape(dg, S5_STATE, 1), s5_lam_im.reshape(dg, S5_STATE, 1),
        dbl(s5_lam_re).reshape(dg, 1, 2 * S5_STATE), dbl(s5_lam_im).reshape(dg, 1, 2 * S5_STATE),
        dbl(s5_c_re).reshape(dg, S5_GROUP, 2 * S5_STATE),
        dbl(s5_c_im).reshape(dg, S5_GROUP, 2 * S5_STATE),
        s5_b_re.reshape(dg, S5_STATE, S5_GROUP), s5_b_im.reshape(dg, S5_STATE, S5_GROUP))

    head_of_col = jnp.arange(SSD_WIDTH) // SSD_HEAD_DIM
    e64 = (jnp.arange(LANES)[:, None] == head_of_col[None, :]).astype(BF16)
    pad_h = lambda a: jnp.pad(a.astype(F32), ((0, 0), (0, LANES - SSD_HEADS))).reshape(
        depth, 1, LANES)
    dtb, alog = pad_h(ssd_dt_bias), pad_h(ssd_a_log)
    dsk = rows(jnp.repeat(ssd_d, SSD_HEAD_DIM, axis=-1))
    g_mix, g_s5, g_ssd, g_ffn = rows(norm_mix), rows(s5_norm), rows(ssd_norm), rows(norm_ffn)
    d5, bglu, cb = rows(s5_d), rows(s5_b_glu), rows(ssd_conv_b)
    cw = ssd_conv_w.astype(F32)
    gfin = norm_final.reshape(1, -1).astype(F32)

    s5_shape = (nc5, D_MODEL // LANES, S5_CHUNK, bsz, LANES)
    for i in range(depth):
        u, z, xbc, dt = _in_proj(x, g_mix, w_all, i, 2 * tok)
        ys = _s5_core(u.reshape(s5_shape), mt, bt, ct, apow, i).reshape(u.shape)
        yb = _ssd(z, xbc, dt, cw, cb, dtb, alog, dsk, g_ssd, e64, i, lb, bsz)
        x = _post(x, ys, u, yb, d5, wglu, bglu, g_s5, wo, g_ffn, wg, wup, wd, gfin,
                  i, tok, i == depth - 1)
    return x
```

```python
import functools

import jax
import jax.numpy as jnp
from jax import lax
from jax.experimental import pallas as pl
from jax.experimental.pallas import tpu as pltpu

F32 = jnp.float32
BF16 = jnp.bfloat16
EPS = 1e-6

D_MODEL = 1024
S5_GROUP = 16
S5_GROUPS = 64
S5_STATE = 64
S5_CHUNK = 32
S5_FLAT = S5_CHUNK * S5_GROUP
SSD_HEAD_DIM = 64
SSD_HEADS = 16
SSD_GROUPS = 2
SSD_STATE = 128
SSD_CONV = 4
SSD_CHUNK = 128
SSD_WIDTH = SSD_HEADS * SSD_HEAD_DIM
SSD_CONV_DIM = SSD_WIDTH + 2 * SSD_GROUPS * SSD_STATE
FFN_HIDDEN = 2816
LANES = 128
SUBLANES = 8
MXU_WIDTH = 256
S5_GPB = LANES // S5_GROUP
S5_BUFS = 3
VMEM_LIMIT = 56 * 1024 * 1024
IN_U = D_MODEL
IN_Z = IN_U + SSD_WIDTH
IN_X = IN_Z + SSD_CONV_DIM


def _params(*sem):
    return pltpu.CompilerParams(dimension_semantics=sem, vmem_limit_bytes=VMEM_LIMIT)


def _rms(x, gain):
    return x * lax.rsqrt(jnp.mean(x * x, axis=-1, keepdims=True) + EPS) * gain


def _dot(a, b):
    return jnp.dot(a, b, preferred_element_type=F32)


def _split(v, terms):
    out = []
    for _ in range(terms - 1):
        hi = v.astype(BF16)
        out.append(hi)
        v = v - hi.astype(F32)
    out.append(v.astype(BF16))
    return out


def _dot_sel_rhs(v, sel, terms=3):
    return functools.reduce(jnp.add, [_dot(t, sel) for t in _split(v, terms)])


def _dot_sel_lhs(sel, v, terms=3):
    return functools.reduce(jnp.add, [_dot(sel, t) for t in _split(v, terms)])


def _dot_f32(a, b, terms=3):
    a_t = _split(a, terms)
    b_t = _split(b, terms)
    return functools.reduce(
        jnp.add, [_dot(a_t[i], b_t[j]) for i in range(terms) for j in range(terms - i)])


def _layer_spec(shape, layer, n_grid, col=0):
    zeros = (0,) * (len(shape) - 2)
    if n_grid == 1:
        return pl.BlockSpec((None,) + shape[1:], lambda i: (layer,) + zeros + (col,))
    return pl.BlockSpec((None,) + shape[1:], lambda i, j: (layer,) + zeros + (col,))


def _cast_kernel(w_ref, o_ref):
    o_ref[...] = w_ref[...].astype(o_ref.dtype)


def _cast_bf16(w, rb):
    depth, rows, cols = w.shape
    spec = pl.BlockSpec((1, rb, cols), lambda i, j: (i, j, 0))
    return pl.pallas_call(
        _cast_kernel, grid=(depth, rows // rb), in_specs=[spec], out_specs=spec,
        out_shape=jax.ShapeDtypeStruct(w.shape, BF16),
        compiler_params=_params("parallel", "parallel"), name="cast",
    )(w)


def _cast_win_kernel(w_ref, o_ref, *, n_out):
    rb = w_ref.shape[1]
    valid = n_out - pl.program_id(1) * rb
    row = lax.broadcasted_iota(jnp.int32, (rb, 1), 0)
    o_ref[0] = jnp.where(row < valid, w_ref[0], 0.0).T.astype(o_ref.dtype)


def _cast_win(w_in_t, rb):
    depth, n_out, d = w_in_t.shape
    nblk = pl.cdiv(n_out, rb)
    return pl.pallas_call(
        functools.partial(_cast_win_kernel, n_out=n_out), grid=(depth, nblk),
        in_specs=[pl.BlockSpec((1, rb, d), lambda i, j: (i, j, 0))],
        out_specs=pl.BlockSpec((1, d, rb), lambda i, j: (i, 0, j)),
        out_shape=jax.ShapeDtypeStruct((depth, d, nblk * rb), BF16),
        compiler_params=_params("parallel", "parallel"), name="cast_win",
    )(w_in_t)


def _s5_rows(c, j, b):
    return pl.ds(((c * (D_MODEL // LANES) + j) * S5_CHUNK) * SUBLANES + b, S5_CHUNK,
                 stride=SUBLANES)


def _inproj_kernel(x_ref, g_ref, wu_ref, wz_ref, wx0_ref, wx1_ref, wx2_ref, wdt_ref,
                   u_ref, z_ref, xbc_ref, dt_ref):
    bsz, tok, _ = x_ref.shape
    hb = _rms(x_ref[...], g_ref[...]).astype(BF16).reshape(bsz * tok, D_MODEL)
    u = _dot(hb, wu_ref[...])
    for b in range(bsz):
        for c in range(tok // S5_CHUNK):
            r0 = b * tok + c * S5_CHUNK
            for j in range(D_MODEL // LANES):
                u_ref[_s5_rows(c, j, b), :] = u[r0:r0 + S5_CHUNK, j * LANES:(j + 1) * LANES]
    z_ref[...] = _dot(hb, wz_ref[...]).astype(z_ref.dtype).reshape(z_ref.shape)
    xbc = jnp.concatenate([_dot(hb, w[...]).astype(xbc_ref.dtype)
                           for w in (wx0_ref, wx1_ref, wx2_ref)], axis=1)
    xbc_ref[...] = xbc.reshape(xbc_ref.shape)
    dt_ref[...] = _dot(hb, wdt_ref[...]).reshape(dt_ref.shape)


def _in_proj(x3, gain, w_all, layer, tok):
    b, seq, _ = x3.shape
    nat = lambda w: pl.BlockSpec((b, tok, w), lambda i: (0, i, 0))
    win = lambda width, start: pl.BlockSpec((None, D_MODEL, width),
                                            lambda i: (layer, 0, start // width))
    xw = SSD_CONV_DIM // 3
    return pl.pallas_call(
        _inproj_kernel,
        grid=(seq // tok,),
        in_specs=[nat(D_MODEL), _layer_spec(gain.shape, layer, 1), win(D_MODEL, 0),
                  win(SSD_WIDTH, IN_U), win(xw, IN_Z), win(xw, IN_Z + xw), win(xw, IN_Z + 2 * xw),
                  win(LANES, IN_X)],
        out_specs=[pl.BlockSpec((tok * b * D_MODEL // LANES, LANES), lambda i: (i, 0)),
                   nat(SSD_WIDTH), nat(SSD_CONV_DIM), nat(LANES)],
        out_shape=[jax.ShapeDtypeStruct((seq * b * D_MODEL // LANES, LANES), F32),
                   jax.ShapeDtypeStruct((b, seq, SSD_WIDTH), BF16),
                   jax.ShapeDtypeStruct((b, seq, SSD_CONV_DIM), BF16),
                   jax.ShapeDtypeStruct((b, seq, LANES), F32)],
        compiler_params=_params("parallel"),
        name="in_proj",
    )(x3, gain, *([w_all] * 6))


def _s5_prep_group(q, e_h, e_rev, ls_ref, lrc_ref, lic_ref, lrr_ref, lir_ref, cre_ref, cim_ref,
                   bre_ref, bim_ref, mt_ref, bt_ref, ct_ref, apow_ref):
    step = jnp.exp(ls_ref[q])
    lrc = lrc_ref[q]
    lic = lic_ref[q]
    lrr = lrr_ref[q]
    lir = lir_ref[q]
    p2 = 2 * S5_STATE

    n_pow = S5_CHUNK + SUBLANES
    st = step * lax.broadcasted_iota(jnp.int32, (n_pow, 1), 0).astype(F32)
    mag = jnp.exp(lrr * st)
    pre_k = mag * jnp.cos(lir * st)
    pim_k = mag * jnp.sin(lir * st)

    def to_cols(p_k):
        return jnp.concatenate(
            [p_k, jnp.zeros((LANES - n_pow, p2), F32)], axis=0).T[0:S5_STATE, :]

    pre_c = to_cols(pre_k)
    pim_c = to_cols(pim_k)
    are = pre_c[:, 1:2]
    aim = pim_c[:, 1:2]
    den = lrc * lrc + lic * lic
    nr = are - 1.0
    cfr = (nr * lrc + aim * lic) / den
    cfi = (aim * lrc - nr * lic) / den
    bre = bre_ref[q]
    bim = bim_ref[q]
    bbr = cfr * bre - cfi * bim
    bbi = cfr * bim + cfi * bre

    lane = lax.broadcasted_iota(jnp.int32, (1, S5_FLAT), 1)
    btr = _dot_sel_rhs(bbr, e_h, 2)
    bti = _dot_sel_rhs(bbi, e_h, 2)
    prr = _dot_sel_rhs(pre_c, e_rev, 2)
    pri = _dot_sel_rhs(pim_c, e_rev, 2)
    wre = prr * btr - pri * bti
    wim = prr * bti + pri * btr
    bt_ref[q] = jnp.concatenate([wre, wim, wim, wre], axis=0).astype(bt_ref.dtype)

    lane2 = lax.broadcasted_iota(jnp.int32, (1, p2), 1)
    first = lane2 < S5_STATE
    c1 = cre_ref[q]
    c2 = cim_ref[q]
    lhs = jnp.where(first, c1, c2)
    rev = _dot_f32(lhs, jnp.concatenate([wre, -wim], axis=0), 2)
    for t in range(S5_CHUNK):
        rows = slice(t * S5_GROUP, (t + 1) * S5_GROUP)
        shift = (S5_FLAT - (S5_CHUNK - 1 - t) * S5_GROUP) % S5_FLAT
        blk = rev if shift == 0 else pltpu.roll(rev, shift, axis=1)
        if t < S5_CHUNK - 1:
            blk = jnp.where(lane < (t + 1) * S5_GROUP, blk, 0.0)
        mt_ref[q, rows, :] = blk.astype(mt_ref.dtype)

    pre_r = pre_k[1:S5_CHUNK + 1, :]
    pim_r = pim_k[1:S5_CHUNK + 1, :]
    q1 = jnp.where(first, pre_r, pim_r)
    q2 = jnp.where(first, pim_r, pre_r)
    s1 = jnp.where(first, 1.0, -1.0).astype(F32)
    for t in range(S5_CHUNK):
        rows = slice(t * S5_GROUP, (t + 1) * S5_GROUP)
        ct_ref[q, rows, :] = (s1 * (c1 * q1[t:t + 1, :]) - c2 * q2[t:t + 1, :]).astype(ct_ref.dtype)
    atr = pre_r[S5_CHUNK - 1:S5_CHUNK, :]
    a2 = -s1 * pim_r[S5_CHUNK - 1:S5_CHUNK, :]
    apow_ref[q] = jnp.concatenate(
        [atr, a2, -a2, jnp.zeros((SUBLANES - 3, p2), F32)], axis=0)


def _s5_prep_kernel(*refs):
    h_of_lane = lax.broadcasted_iota(jnp.int32, (S5_GROUP, S5_FLAT), 1) % S5_GROUP
    e_h = (h_of_lane == lax.broadcasted_iota(jnp.int32, (S5_GROUP, S5_FLAT), 0)).astype(BF16)
    k_row = lax.broadcasted_iota(jnp.int32, (LANES, S5_FLAT), 0)
    j_lane = lax.broadcasted_iota(jnp.int32, (LANES, S5_FLAT), 1) // S5_GROUP
    e_rev = (k_row == (S5_CHUNK - 1) - j_lane).astype(BF16)
    for q in range(refs[0].shape[0]):
        _s5_prep_group(q, e_h, e_rev, *refs)


def _s5_prep(ls, lrc, lic, lrr, lir, cre, cim, bre, bim):
    dg = ls.shape[0]
    gps = 16
    spec = lambda a: pl.BlockSpec((gps,) + a.shape[1:], lambda i: (i, 0, 0))
    ins = (ls, lrc, lic, lrr, lir, cre, cim, bre, bim)
    shapes = [(dg, S5_FLAT, S5_FLAT), (dg, 4 * S5_STATE, S5_FLAT),
              (dg, S5_FLAT, 2 * S5_STATE), (dg, SUBLANES, 2 * S5_STATE)]
    dtypes = [BF16, BF16, BF16, F32]
    return pl.pallas_call(
        _s5_prep_kernel,
        grid=(dg // gps,),
        in_specs=[spec(a) for a in ins],
        out_specs=[pl.BlockSpec((gps,) + s[1:], lambda i: (i, 0, 0)) for s in shapes],
        out_shape=[jax.ShapeDtypeStruct(s, d) for s, d in zip(shapes, dtypes)],
        compiler_params=_params("parallel"),
        name="s5_prep",
    )(*ins)


def _s5_core_kernel(u_hbm, mt_ref, bt_ref, ct_ref, ap_ref, y_hbm,
                    io_buf, zt_scr, e_scr, sp_scr, yt_scr, sem_in, sem_out, *, batch):
    ng = mt_ref.shape[0]
    nchunk = u_hbm.shape[0]
    cols = nchunk * batch
    half = S5_FLAT // 2
    j = pl.program_id(0)
    last = pl.num_programs(0) - 1
    nbuf = io_buf.shape[0]
    slot = j % nbuf
    nxt = (j + 1) % nbuf

    def in_copy(step, s, t):
        return pltpu.make_async_copy(u_hbm.at[:, step, t], io_buf.at[s, t], sem_in.at[s, t])

    def out_copy(step, s, t):
        return pltpu.make_async_copy(io_buf.at[s, t], y_hbm.at[:, step, t], sem_out.at[s, t])

    @pl.when(j == 0)
    def _():
        for t in range(S5_CHUNK):
            in_copy(j, slot, t).start()

    @pl.when(jnp.logical_and(j >= nbuf - 1, j < last))
    def _():
        for t in range(S5_CHUNK):
            out_copy(j + 1 - nbuf, nxt, t).wait()

    @pl.when(j < last)
    def _():
        for t in range(S5_CHUNK):
            in_copy(j + 1, nxt, t).start()

    for t in range(S5_CHUNK):
        in_copy(j, slot, t).wait()
    for t in range(S5_CHUNK):
        blk_t = io_buf[slot, t].reshape(cols, LANES).T.astype(BF16)
        for g in range(ng):
            zt_scr[g, t * S5_GROUP:(t + 1) * S5_GROUP, :] = blk_t[g * S5_GROUP:(g + 1) * S5_GROUP, :]

    for g in range(ng):
        e_scr[g] = _dot(bt_ref[g], zt_scr[g]).T

    def body(c, carry):
        r = pl.multiple_of(c * batch, batch)
        out = []
        for g in range(ng):
            s, sw = carry[g]
            sp_scr[g, pl.ds(r, batch), :] = s
            e = e_scr[g, pl.ds(r, batch), :]
            a1 = ap_ref[g, 0:1, :]
            a2 = ap_ref[g, 1:2, :]
            a2w = ap_ref[g, 2:3, :]
            out.append((a1 * s + a2 * sw + e[:, :LANES], a1 * sw + a2w * s + e[:, LANES:]))
        return tuple(out)

    zero = jnp.zeros((batch, 2 * S5_STATE), F32)
    lax.fori_loop(0, nchunk, body, tuple((zero, zero) for _ in range(ng)), unroll=2)

    for g in range(ng):
        y_off = lax.dot_general(ct_ref[g], sp_scr[g].astype(BF16), (((1,), (1,)), ((), ())),
                                preferred_element_type=F32)
        top = _dot(mt_ref[g, :half, :half], zt_scr[g, :half, :])
        bot = _dot(mt_ref[g, half:, :], zt_scr[g])
        yt_scr[g, :half, :] = (top + y_off[:half]).astype(yt_scr.dtype)
        yt_scr[g, half:, :] = (bot + y_off[half:]).astype(yt_scr.dtype)

    for t in range(S5_CHUNK):
        rows = slice(t * S5_GROUP, (t + 1) * S5_GROUP)
        blk = jnp.concatenate([yt_scr[g, rows, :] for g in range(ng)], axis=0)
        io_buf[slot, t] = blk.astype(F32).T.reshape(nchunk, batch, LANES)
        out_copy(j, slot, t).start()

    @pl.when(j == last)
    def _():
        for back in range(nbuf):
            for t in range(S5_CHUNK):
                out_copy(j - back, (j - back) % nbuf, t).wait()


def _s5_core(u5, mt, bt, ct, apow, layer):
    nchunk, nblk, _, batch, _ = u5.shape
    cols = nchunk * batch
    ng = S5_GPB
    op = lambda a: pl.BlockSpec((ng,) + a.shape[1:], lambda i: (layer * nblk + i, 0, 0))
    hbm = pl.BlockSpec(memory_space=pl.ANY)
    return pl.pallas_call(
        functools.partial(_s5_core_kernel, batch=batch),
        grid=(nblk,),
        in_specs=[hbm, op(mt), op(bt), op(ct), op(apow)],
        out_specs=hbm,
        out_shape=jax.ShapeDtypeStruct(u5.shape, F32),
        scratch_shapes=[pltpu.VMEM((S5_BUFS, S5_CHUNK, nchunk, batch, LANES), F32),
                        pltpu.VMEM((ng, S5_FLAT, cols), BF16),
                        pltpu.VMEM((ng, cols, 4 * S5_STATE), F32),
                        pltpu.VMEM((ng, cols, 2 * S5_STATE), F32),
                        pltpu.VMEM((ng, S5_FLAT, cols), BF16),
                        pltpu.SemaphoreType.DMA((S5_BUFS, S5_CHUNK)),
                        pltpu.SemaphoreType.DMA((S5_BUFS, S5_CHUNK))],
        compiler_params=_params("arbitrary"),
        name="s5_core",
    )(u5, mt, bt, ct, apow)


def _ssd_kernel(z_ref, xbc_ref, dt_ref, cw_ref, cb_ref, dtb_ref, alog_ref, dsk_ref,
                gn_ref, e64_ref, y_ref, tail_scr, xc_scr, st_scr):
    for i in range(z_ref.shape[0]):
        _ssd_batch(z_ref.at[i], xbc_ref.at[i], dt_ref.at[i], cw_ref, cb_ref, dtb_ref, alog_ref,
                   dsk_ref, gn_ref, e64_ref, y_ref.at[i], tail_scr.at[i], xc_scr.at[i],
                   st_scr.at[i])


def _ssd_batch(z_ref, xbc_ref, dt_ref, cw_ref, cb_ref, dtb_ref, alog_ref, dsk_ref,
               gn_ref, e64_ref, y_ref, tail_scr, xc_scr, st_scr):
    lb = z_ref.shape[0]
    j = pl.program_id(1)

    @pl.when(j == 0)
    def _():
        tail_scr[...] = jnp.zeros_like(tail_scr)
        st_scr[...] = jnp.zeros_like(st_scr)

    xb = xbc_ref[...]
    x0 = xb.astype(F32)
    rr = lax.broadcasted_iota(jnp.int32, (lb, lb), 0)
    cc = lax.broadcasted_iota(jnp.int32, (lb, lb), 1)
    row8 = lax.broadcasted_iota(jnp.int32, (SUBLANES, 1), 0)
    tail = tail_scr[...]
    acc = cb_ref[...] + cw_ref[SSD_CONV - 1:SSD_CONV, :] * x0
    head = jnp.zeros((SUBLANES, SSD_CONV_DIM), F32)
    for k in range(1, SSD_CONV):
        wk = cw_ref[SSD_CONV - 1 - k:SSD_CONV - k, :]
        acc = acc + wk * _dot((rr - cc == k).astype(BF16), xb)
        head = head + wk * jnp.where(row8 < k, pltpu.roll(tail, k, axis=0), 0.0)
    tail_scr[...] = x0[lb - SUBLANES:, :]
    xc_scr[...] = acc * jax.nn.sigmoid(acc)
    acc0 = acc[0:SUBLANES, :] + head
    xc_scr[0:SUBLANES, :] = acc0 * jax.nn.sigmoid(acc0)

    dt_t = (dt_ref[...] + dtb_ref[...]).T[0:SSD_HEADS, :]
    dtp_t = jnp.maximum(dt_t, 0.0) + jnp.log(1.0 + jnp.exp(-jnp.abs(dt_t)))
    dtp_all = jnp.concatenate(
        [dtp_t, jnp.zeros((LANES - SSD_HEADS, lb), F32)], axis=0).T

    ri = lax.broadcasted_iota(jnp.int32, (SSD_CHUNK, SSD_CHUNK), 0)
    ci = lax.broadcasted_iota(jnp.int32, (SSD_CHUNK, SSD_CHUNK), 1)
    causal = ri >= ci
    ltri = causal.astype(BF16)
    lo_half = ci < SSD_HEAD_DIM
    a_neg = -jnp.exp(alog_ref[...])
    e64 = e64_ref[...]
    b0 = SSD_WIDTH
    c0 = SSD_WIDTH + SSD_GROUPS * SSD_STATE
    hpg = SSD_HEADS // SSD_GROUPS

    for c in range(lb // SSD_CHUNK):
        r0 = c * SSD_CHUNK
        xs = xc_scr[r0:r0 + SSD_CHUNK, 0:SSD_WIDTH]
        dtp = dtp_all[r0:r0 + SSD_CHUNK, :]
        dta = dtp * a_neg
        acum = _dot_sel_lhs(ltri, dta)
        acum_t = acum.T
        atot = acum[SSD_CHUNK - 1:SSD_CHUNK, :]
        fac = jnp.concatenate([dtp, dtp * jnp.exp(atot - acum), jnp.exp(acum)], axis=0)
        fac64 = _dot(fac.astype(BF16), e64)
        xdt = xs * fac64[0:SSD_CHUNK]
        xw = (xs * fac64[SSD_CHUNK:2 * SSD_CHUNK]).astype(BF16)
        ea64 = fac64[2 * SSD_CHUNK:]
        tail = SSD_CHUNK - SUBLANES
        etot64 = jnp.exp(_dot_sel_rhs(acum[tail:, :], e64)[SUBLANES - 1:SUBLANES, :])
        gmats = []
        for g in range(SSD_GROUPS):
            bm = xc_scr[r0:r0 + SSD_CHUNK, b0 + g * SSD_STATE:b0 + (g + 1) * SSD_STATE]
            cm = xc_scr[r0:r0 + SSD_CHUNK, c0 + g * SSD_STATE:c0 + (g + 1) * SSD_STATE].astype(BF16)
            bm_t = bm.T.astype(BF16)
            gmats.append((bm_t, cm, _dot(cm, bm_t)))
        ys = []
        for p in range(SSD_HEADS // 2):
            bm_t, cm, gm = gmats[(2 * p) // hpg]
            cols = slice(p * LANES, (p + 1) * LANES)
            xp = xdt[:, cols]
            y = None
            for hh in range(2):
                h = 2 * p + hh
                col = jnp.broadcast_to(acum[:, h:h + 1], (SSD_CHUNK, SSD_CHUNK))
                rowv = acum_t[h:h + 1, :]
                w = jnp.where(causal, gm * jnp.exp(col - rowv), 0.0).astype(BF16)
                xh = jnp.where(lo_half if hh == 0 else ~lo_half, xp, 0.0).astype(BF16)
                t = _dot(w, xh)
                y = t if y is None else y + t
            st = st_scr[p]
            y = y + _dot(cm, st.astype(BF16)) * ea64[:, cols]
            st_scr[p] = st * etot64[:, cols] + _dot(bm_t, xw[:, cols])
            ys.append(y + dsk_ref[:, cols] * xs[:, cols])
        yc = jnp.concatenate(ys, axis=1)
        zc = z_ref[r0:r0 + SSD_CHUNK, :].astype(F32)
        yc = yc * (zc * jax.nn.sigmoid(zc))
        y_ref[r0:r0 + SSD_CHUNK, :] = _rms(yc, gn_ref[...]).astype(y_ref.dtype)


def _ssd(z3, xbc3, dt3, cw, cb, dtb, alog, dsk, gn, e64, layer, lb, nb):
    b, l, _ = z3.shape
    blk = lambda w: pl.BlockSpec((nb, lb, w), lambda i, j: (i, j, 0))
    ls = lambda a: _layer_spec(a.shape, layer, 2)
    return pl.pallas_call(
        _ssd_kernel,
        grid=(b // nb, l // lb),
        in_specs=[blk(SSD_WIDTH), blk(SSD_CONV_DIM), blk(LANES), ls(cw), ls(cb),
                  ls(dtb), ls(alog), ls(dsk), ls(gn),
                  pl.BlockSpec(e64.shape, lambda i, j: (0, 0))],
        out_specs=blk(SSD_WIDTH),
        out_shape=jax.ShapeDtypeStruct((b, l, SSD_WIDTH), BF16),
        scratch_shapes=[pltpu.VMEM((nb, SUBLANES, SSD_CONV_DIM), F32),
                        pltpu.VMEM((nb, lb, SSD_CONV_DIM), F32),
                        pltpu.VMEM((nb, SSD_HEADS // 2, SSD_STATE, LANES), F32)],
        compiler_params=_params("parallel", "arbitrary"),
        name="ssd",
    )(z3, xbc3, dt3, cw, cb, dtb, alog, dsk, gn, e64)


FFN_CHUNK = MXU_WIDTH


def _gelu_tanh(v):
    return 0.5 * v * (1.0 + jnp.tanh(0.7978845608028654 * (v + 0.044715 * (v * v * v))))


def _post_kernel(x_ref, ys_ref, u_ref, yb_ref, d5_ref, wglu_ref, bglu_ref, g5_ref,
                 woa_ref, wob_ref, gf_ref, wg_ref, wu_ref, wd_ref, gfin_ref, o_ref, y_scr,
                 *, final):
    bsz, tok, _ = x_ref.shape
    for b in range(bsz):
        for c in range(tok // S5_CHUNK):
            r0 = b * tok + c * S5_CHUNK
            for j in range(D_MODEL // LANES):
                cols = slice(j * LANES, (j + 1) * LANES)
                rows = _s5_rows(c, j, b)
                y_scr[r0:r0 + S5_CHUNK, cols] = ys_ref[rows, :] + d5_ref[:, cols] * u_ref[rows, :]
    g = _gelu_tanh(y_scr[...])
    ya = g * jax.nn.sigmoid(_dot(g.astype(BF16), wglu_ref[...]) + bglu_ref[...])
    ya = _rms(ya, g5_ref[...]).astype(BF16)
    yb = yb_ref[...].reshape(bsz * tok, SSD_WIDTH)
    x = x_ref[...].reshape(bsz * tok, D_MODEL) + _dot(ya, woa_ref[...]) + _dot(yb, wob_ref[...])
    hb = _rms(x, gf_ref[...]).astype(BF16)
    y_scr[...] = x

    def ffn_chunk(k, carry):
        sl = pl.ds(pl.multiple_of(k * FFN_CHUNK, FFN_CHUNK), FFN_CHUNK)
        gate = _dot(hb, wg_ref[:, sl])
        act = (gate * jax.nn.sigmoid(gate)) * _dot(hb, wu_ref[:, sl])
        y_scr[...] += _dot(act.astype(BF16), wd_ref[sl, :])
        return carry

    lax.fori_loop(0, FFN_HIDDEN // FFN_CHUNK, ffn_chunk, 0, unroll=4)
    out = y_scr[...]
    if final:
        out = _rms(out, gfin_ref[...])
    o_ref[...] = out.reshape(o_ref.shape)


def _post(x3, ys, u, yb3, d5, wglu, bglu, g5, wo, gf, wg, wu, wd, gfin, layer, tok, final):
    b, seq, _ = x3.shape
    nat = pl.BlockSpec((b, tok, D_MODEL), lambda i: (0, i, 0))
    s5 = pl.BlockSpec((tok * b * D_MODEL // LANES, LANES), lambda i: (i, 0))
    one = pl.Buffered(1)
    ls = lambda a: pl.BlockSpec((None,) + a.shape[1:], lambda i: (layer, 0, 0), pipeline_mode=one)
    wo_half = lambda k: pl.BlockSpec((None, D_MODEL, D_MODEL), lambda i: (layer, k, 0),
                                     pipeline_mode=one)
    gfin_spec = pl.BlockSpec(gfin.shape, lambda i: (0, 0), pipeline_mode=one)
    return pl.pallas_call(
        functools.partial(_post_kernel, final=final),
        grid=(seq // tok,),
        in_specs=[nat, s5, s5, nat, ls(d5), ls(wglu), ls(bglu), ls(g5), wo_half(0), wo_half(1),
                  ls(gf), ls(wg), ls(wu), ls(wd), gfin_spec],
        out_specs=nat,
        out_shape=jax.ShapeDtypeStruct(x3.shape, F32),
        scratch_shapes=[pltpu.VMEM((b * tok, D_MODEL), F32)],
        compiler_params=_params("parallel"),
        name="post",
    )(x3, ys, u, yb3, d5, wglu, bglu, g5, wo, wo, gf, wg, wu, wd, gfin)


def kernel(x, norm_mix, w_in, s5_lam_re, s5_lam_im, s5_log_step, s5_b_re, s5_b_im, s5_c_re, s5_c_im, s5_d, s5_w_glu, s5_b_glu, s5_norm, ssd_conv_w, ssd_conv_b, ssd_dt_bias, ssd_a_log, ssd_d, ssd_norm, w_out, norm_ffn, w_gate, w_up, w_down, norm_final):
    bsz, seq, _ = x.shape
    depth = w_in.shape[0]
    assert bsz == SUBLANES and seq % S5_CHUNK == 0 and seq % SSD_CHUNK == 0
    nc5 = seq // S5_CHUNK
    tok = 2 * S5_CHUNK
    lb = min(2 * SSD_CHUNK, seq)
    dg = depth * S5_GROUPS
    rows = lambda a: a.reshape(depth, 1, -1).astype(F32)

    w_all = _cast_win(jnp.swapaxes(w_in, 1, 2), 2 * MXU_WIDTH)
    wglu = _cast_bf16(s5_w_glu, D_MODEL // 2)
    wo = _cast_bf16(w_out, D_MODEL)
    wg = _cast_bf16(w_gate, D_MODEL // 2)
    wup = _cast_bf16(w_up, D_MODEL // 2)
    wd = _cast_bf16(w_down, FFN_HIDDEN // 4)

    dbl = lambda a: jnp.concatenate([a, a], axis=-1)
    mt, bt, ct, apow = _s5_prep(
        s5_log_step.reshape(dg, 1, 1),
        s5_lam_re.reshape(dg, S5_STATE, 1), s5_lam_im.reshape(dg, S5_STATE, 1),
        dbl(s5_lam_re).reshape(dg, 1, 2 * S5_STATE), dbl(s5_lam_im).reshape(dg, 1, 2 * S5_STATE),
        dbl(s5_c_re).reshape(dg, S5_GROUP, 2 * S5_STATE),
        dbl(s5_c_im).reshape(dg, S5_GROUP, 2 * S5_STATE),
        s5_b_re.reshape(dg, S5_STATE, S5_GROUP), s5_b_im.reshape(dg, S5_STATE, S5_GROUP))

    head_of_col = jnp.arange(SSD_WIDTH) // SSD_HEAD_DIM
    e64 = (jnp.arange(LANES)[:, None] == head_of_col[None, :]).astype(BF16)
    pad_h = lambda a: jnp.pad(a.astype(F32), ((0, 0), (0, LANES - SSD_HEADS))).reshape(
        depth, 1, LANES)
    dtb, alog = pad_h(ssd_dt_bias), pad_h(ssd_a_log)
    dsk = rows(jnp.repeat(ssd_d, SSD_HEAD_DIM, axis=-1))
    g_mix, g_s5, g_ssd, g_ffn = rows(norm_mix), rows(s5_norm), rows(ssd_norm), rows(norm_ffn)
    d5, bglu, cb = rows(s5_d), rows(s5_b_glu), rows(ssd_conv_b)
    cw = ssd_conv_w.astype(F32)
    gfin = norm_final.reshape(1, -1).astype(F32)

    s5_shape = (nc5, D_MODEL // LANES, S5_CHUNK, bsz, LANES)
    for i in range(depth):
        u, z, xbc, dt = _in_proj(x, g_mix, w_all, i, 2 * tok)
        ys = _s5_core(u.reshape(s5_shape), mt, bt, ct, apow, i).reshape(u.shape)
        yb = _ssd(z, xbc, dt, cw, cb, dtb, alog, dsk, g_ssd, e64, i, lb, bsz)
        x = _post(x, ys, u, yb, d5, wglu, bglu, g_s5, wo, g_ffn, wg, wup, wd, gfin,
                  i, tok, i == depth - 1)
    return x
```
